```python
import jax
import jax.numpy as jnp
from jax import lax
import numpy as np

D_MODEL = 4096
BATCH = 1
SEQ = 16384
DEPTH = 2

HEAD_DIM = 128
CHUNK = 128
MIX_WIDTH = D_MODEL
A_GROUPS = MIX_WIDTH // (2 * HEAD_DIM)
A_WIDTH = A_GROUPS * HEAD_DIM
B_HEADS = MIX_WIDTH // (2 * HEAD_DIM)
B_KV_HEADS = B_HEADS // 4
B_WIDTH = B_HEADS * HEAD_DIM
B_KV_WIDTH = B_KV_HEADS * HEAD_DIM
WINDOW = 128
EVEN_SPLITS = (A_WIDTH, 2 * A_WIDTH, 2 * A_WIDTH + B_WIDTH, 2 * A_WIDTH + B_WIDTH + B_KV_WIDTH)
EVEN_IN = 2 * A_WIDTH + B_WIDTH + 2 * B_KV_WIDTH
EVEN_OUT = A_WIDTH + B_WIDTH
C_WIDTH = MIX_WIDTH // 2
CONV_WIDTH = 31
D_HEADS = MIX_WIDTH // (2 * HEAD_DIM)
D_WIDTH = D_HEADS * HEAD_DIM
ODD_SPLITS = (C_WIDTH, 2 * C_WIDTH, 2 * C_WIDTH + D_WIDTH, 2 * C_WIDTH + 2 * D_WIDTH)
ODD_IN = 2 * C_WIDTH + 3 * D_WIDTH
ODD_OUT = C_WIDTH + D_WIDTH
N_GROUPS = 8
EXPERTS_PER_GROUP = 8
N_EXPERTS = N_GROUPS * EXPERTS_PER_GROUP
TOP_K = 2
D_EXPERT = 3 * D_MODEL // 32
EXPERT_BLOCK = 128
EPS = 1e-6

kernel_name = 'hybrid_gmlp_swa_conformer_stickbreak_hmoe'


def rms_norm(x, g):
    xf = x.astype(jnp.float32)
    y = xf * lax.rsqrt(jnp.mean(xf * xf, axis=-1, keepdims=True) + EPS)
    return (y * g.astype(jnp.float32)).astype(x.dtype)


def layer_norm(x, g, b):
    xf = x.astype(jnp.float32)
    mu = jnp.mean(xf, axis=-1, keepdims=True)
    xc = xf - mu
    var = jnp.mean(xc * xc, axis=-1, keepdims=True)
    return (xc * lax.rsqrt(var + EPS) * g.astype(jnp.float32) + b.astype(jnp.float32)).astype(x.dtype)


def alibi_slopes(n_heads):
    return 2.0 ** (-8.0 * jnp.arange(1, n_heads + 1, dtype=jnp.float32) / n_heads)


def chunked_gmlp(u, v, ln_g, ln_b, w_s, b_s):
    bsz, seq, _ = v.shape
    v = layer_norm(v, ln_g, ln_b)
    vc = v.reshape(bsz, seq // CHUNK, CHUNK, A_GROUPS, HEAD_DIM)
    causal = jnp.tril(jnp.ones((CHUNK, CHUNK), dtype=bool))
    w = jnp.where(causal[None], w_s, 0.0).astype(v.dtype)
    s = jnp.einsum('gts,bcsgd->bctgd', w, vc) + b_s.T.astype(v.dtype)[None, None, :, :, None]
    return u * s.reshape(bsz, seq, A_WIDTH)


def sliding_window_gqa(q, k, v, q_gain, k_gain, sinks):
    bsz, seq = q.shape[:2]
    nblk = seq // CHUNK
    grp = B_HEADS // B_KV_HEADS
    q = rms_norm(q, q_gain)
    k = rms_norm(k, k_gain)
    qb = q.reshape(bsz, nblk, CHUNK, B_KV_HEADS, grp, HEAD_DIM)

    def band(t):
        tb = t.reshape(bsz, nblk, CHUNK, B_KV_HEADS, HEAD_DIM)
        prev = jnp.pad(tb, ((0, 0), (1, 0), (0, 0), (0, 0), (0, 0)))[:, :-1]
        return jnp.concatenate([prev, tb], axis=2)

    kb, vb = band(k), band(v)
    scores = jnp.einsum('bnqhgd,bnkhd->bnhgqk', qb, kb).astype(jnp.float32) * (HEAD_DIM ** -0.5)
    qi = jnp.arange(CHUNK)[:, None]
    kj = jnp.arange(2 * CHUNK)[None, :]
    dist = qi + CHUNK - kj
    blk = jnp.arange(nblk)[:, None, None]
    valid = (dist >= 0) & (dist < WINDOW) & (blk * CHUNK - CHUNK + kj >= 0)
    slopes = alibi_slopes(B_HEADS).reshape(B_KV_HEADS, grp)
    scores = scores - slopes[:, :, None, None] * dist.astype(jnp.float32)
    scores = jnp.where(valid[None, :, None, None], scores, -jnp.inf)
    sink = sinks.astype(jnp.float32).reshape(B_KV_HEADS, grp)[:, :, None, None]
    m = jnp.maximum(jnp.max(scores, axis=-1, keepdims=True), sink)
    p = jnp.exp(scores - m)
    p = p / (jnp.sum(p, axis=-1, keepdims=True) + jnp.exp(sink - m))
    out = jnp.einsum('bnhgqk,bnkhd->bnqhgd', p.astype(vb.dtype), vb)
    return out.reshape(bsz, seq, B_WIDTH)


def conformer_conv(a, g, conv_w, conv_b, ln_g, ln_b):
    h = a * jax.nn.sigmoid(g)
    y = lax.conv_general_dilated(
        h, conv_w.astype(h.dtype)[:, None, :], window_strides=(1,),
        padding=((CONV_WIDTH - 1, 0),), dimension_numbers=('NWC', 'WIO', 'NWC'),
        feature_group_count=C_WIDTH)
    y = layer_norm(y + conv_b.astype(y.dtype), ln_g, ln_b)
    return jax.nn.silu(y)


def stick_breaking_attention(q, k, v):
    bsz, seq = q.shape[:2]
    nblk = seq // CHUNK
    qb = q.reshape(bsz, nblk, CHUNK, D_HEADS, HEAD_DIM).transpose(1, 0, 2, 3, 4)
    key_pos = jnp.arange(seq)

    def block(args):
        q_blk, n = args
        z = jnp.einsum('bqhd,bkhd->bhqk', q_blk, k).astype(jnp.float32) * (HEAD_DIM ** -0.5)
        q_pos = n * CHUNK + jnp.arange(CHUNK)
        mask = key_pos[None, :] < q_pos[:, None]
        log_fail = jnp.where(mask, jax.nn.log_sigmoid(-z), 0.0)
        later = lax.cumsum(log_fail, axis=3, reverse=True) - log_fail
        a = jnp.where(mask, jnp.exp(jax.nn.log_sigmoid(z) + later), 0.0)
        return jnp.einsum('bhqk,bkhd->bqhd', a.astype(v.dtype), v)

    out = lax.map(block, (qb, jnp.arange(nblk)))
    return out.transpose(1, 0, 2, 3, 4).reshape(bsz, seq, D_WIDTH)


def hierarchical_moe(h, wr_g, br_g, wr_e, br_e, w_gate, w_up, w_down):
    bsz, seq, dm = h.shape
    tokens = h.reshape(-1, dm)
    n_tok = tokens.shape[0]
    tok_idx = jnp.arange(n_tok)
    group_logits = (tokens @ wr_g).astype(jnp.float32) + br_g.astype(jnp.float32)
    group_prob = jax.nn.softmax(group_logits, axis=-1)
    g_sel = jnp.argmax(group_logits, axis=-1)
    g_weight = group_prob[tok_idx, g_sel]
    exp_logits = ((tokens @ wr_e).astype(jnp.float32) + br_e.astype(jnp.float32)).reshape(n_tok, N_GROUPS, EXPERTS_PER_GROUP)
    in_group = exp_logits[tok_idx, g_sel]
    top_val, top_idx = lax.top_k(in_group, TOP_K)
    top_w = jax.nn.softmax(top_val, axis=-1) * g_weight[:, None]
    expert_id = g_sel[:, None] * EXPERTS_PER_GROUP + top_idx

    n_assign = n_tok * TOP_K
    flat_e = expert_id.reshape(-1).astype(jnp.int32)
    flat_w = top_w.reshape(-1)
    flat_tok = jnp.repeat(tok_idx.astype(jnp.int32), TOP_K)
    order = jnp.argsort(flat_e)
    sorted_e = flat_e[order]
    counts = jnp.bincount(flat_e, length=N_EXPERTS).astype(jnp.int32)
    padded = (counts + EXPERT_BLOCK - 1) // EXPERT_BLOCK * EXPERT_BLOCK
    start = jnp.cumsum(counts) - counts
    padded_end = jnp.cumsum(padded)
    padded_start = padded_end - padded
    slot = padded_start[sorted_e] + jnp.arange(n_assign, dtype=jnp.int32) - start[sorted_e]
    n_blocks = -(-n_assign // EXPERT_BLOCK) + N_EXPERTS
    n_slots = n_blocks * EXPERT_BLOCK
    slot_tok = jnp.zeros((n_slots,), jnp.int32).at[slot].set(flat_tok[order])
    slot_w = jnp.zeros((n_slots,), jnp.float32).at[slot].set(flat_w[order])
    block_start = jnp.arange(n_blocks, dtype=jnp.int32) * EXPERT_BLOCK
    block_e = jnp.minimum(jnp.searchsorted(padded_end, block_start, side='right'), N_EXPERTS - 1)

    def run_block(args):
        tok, e = args
        xb = tokens[tok]
        hid = jax.nn.silu(xb @ w_gate[e]) * (xb @ w_up[e])
        return hid @ w_down[e]

    y = lax.map(run_block, (slot_tok.reshape(n_blocks, EXPERT_BLOCK), block_e))
    y = y.reshape(n_slots, dm) * slot_w[:, None].astype(y.dtype)
    out = jnp.zeros_like(tokens).at[slot_tok].add(y)
    return out.reshape(bsz, seq, dm)


def even_mixer(h, w_in, a_ln_g, a_ln_b, a_ws, a_bs, b_q_norm, b_k_norm, b_sinks, w_out):
    bsz, seq, _ = h.shape
    proj = h @ w_in
    a_u, a_v, b_q, b_k, b_v = jnp.split(proj, EVEN_SPLITS, axis=-1)
    y_a = chunked_gmlp(jax.nn.gelu(a_u, approximate=False), jax.nn.gelu(a_v, approximate=False),
                       a_ln_g, a_ln_b, a_ws, a_bs)
    y_b = sliding_window_gqa(b_q.reshape(bsz, seq, B_HEADS, HEAD_DIM),
                             b_k.reshape(bsz, seq, B_KV_HEADS, HEAD_DIM),
                             b_v.reshape(bsz, seq, B_KV_HEADS, HEAD_DIM),
                             b_q_norm, b_k_norm, b_sinks)
    return jnp.concatenate([y_a, y_b], axis=-1) @ w_out


def odd_mixer(h, w_in, c_conv_w, c_conv_b, c_ln_g, c_ln_b, w_out):
    bsz, seq, _ = h.shape
    proj = h @ w_in
    c_a, c_g, d_q, d_k, d_v = jnp.split(proj, ODD_SPLITS, axis=-1)
    y_c = conformer_conv(c_a, c_g, c_conv_w, c_conv_b, c_ln_g, c_ln_b)
    y_d = stick_breaking_attention(d_q.reshape(bsz, seq, D_HEADS, HEAD_DIM),
                                   d_k.reshape(bsz, seq, D_HEADS, HEAD_DIM),
                                   d_v.reshape(bsz, seq, D_HEADS, HEAD_DIM))
    return jnp.concatenate([y_c, y_d], axis=-1) @ w_out


def setup_inputs(seed: int = 0) -> dict:
    key = jax.random.key(seed)
    keys = iter(jax.random.split(key, 64))

    def nrm(shape, scale):
        return jax.random.normal(next(keys), shape, jnp.float32) * scale

    def gain(n):
        return 1.0 + nrm((n,), 0.02)

    def moe_params(prefix):
        return {
            prefix + 'ffn_norm': gain(D_MODEL),
            prefix + 'router_group_w': nrm((D_MODEL, N_GROUPS), D_MODEL ** -0.5),
            prefix + 'router_group_b': nrm((N_GROUPS,), 0.01),
            prefix + 'router_expert_w': nrm((D_MODEL, N_EXPERTS), D_MODEL ** -0.5),
            prefix + 'router_expert_b': nrm((N_EXPERTS,), 0.01),
            prefix + 'w_gate': nrm((N_EXPERTS, D_MODEL, D_EXPERT), D_MODEL ** -0.5),
            prefix + 'w_up': nrm((N_EXPERTS, D_MODEL, D_EXPERT), D_MODEL ** -0.5),
            prefix + 'w_down': nrm((N_EXPERTS, D_EXPERT, D_MODEL), D_EXPERT ** -0.5),
        }

    inputs = {'x': nrm((BATCH, SEQ, D_MODEL), 1.0)}
    inputs.update({
        'l0_mix_norm': gain(D_MODEL),
        'l0_w_in': nrm((D_MODEL, EVEN_IN), D_MODEL ** -0.5),
        'l0_a_ln_g': gain(A_WIDTH),
        'l0_a_ln_b': nrm((A_WIDTH,), 0.02),
        'l0_a_ws': nrm((A_GROUPS, CHUNK, CHUNK), CHUNK ** -0.5),
        'l0_a_bs': 1.0 + nrm((A_GROUPS, CHUNK), 0.02),
        'l0_b_q_norm': gain(HEAD_DIM),
        'l0_b_k_norm': gain(HEAD_DIM),
        'l0_b_sinks': nrm((B_HEADS,), 0.5),
        'l0_w_out': nrm((EVEN_OUT, D_MODEL), EVEN_OUT ** -0.5),
    })
    inputs.update(moe_params('l0_'))
    inputs.update({
        'l1_mix_norm': gain(D_MODEL),
        'l1_w_in': nrm((D_MODEL, ODD_IN), D_MODEL ** -0.5),
        'l1_c_conv_w': nrm((CONV_WIDTH, C_WIDTH), CONV_WIDTH ** -0.5),
        'l1_c_conv_b': nrm((C_WIDTH,), 0.02),
        'l1_c_ln_g': gain(C_WIDTH),
        'l1_c_ln_b': nrm((C_WIDTH,), 0.02),
        'l1_w_out': nrm((ODD_OUT, D_MODEL), ODD_OUT ** -0.5),
    })
    inputs.update(moe_params('l1_'))
    return inputs


def reference(x,
              l0_mix_norm, l0_w_in, l0_a_ln_g, l0_a_ln_b, l0_a_ws, l0_a_bs,
              l0_b_q_norm, l0_b_k_norm, l0_b_sinks, l0_w_out,
              l0_ffn_norm, l0_router_group_w, l0_router_group_b, l0_router_expert_w,
              l0_router_expert_b, l0_w_gate, l0_w_up, l0_w_down,
              l1_mix_norm, l1_w_in, l1_c_conv_w, l1_c_conv_b, l1_c_ln_g, l1_c_ln_b, l1_w_out,
              l1_ffn_norm, l1_router_group_w, l1_router_group_b, l1_router_expert_w,
              l1_router_expert_b, l1_w_gate, l1_w_up, l1_w_down):
    mix_norms = (l0_mix_norm, l1_mix_norm)
    ffn_norms = (l0_ffn_norm, l1_ffn_norm)
    moe = ((l0_router_group_w, l0_router_group_b, l0_router_expert_w, l0_router_expert_b,
            l0_w_gate, l0_w_up, l0_w_down),
           (l1_router_group_w, l1_router_group_b, l1_router_expert_w, l1_router_expert_b,
            l1_w_gate, l1_w_up, l1_w_down))
    for layer in range(DEPTH):
        h = rms_norm(x, mix_norms[layer])
        if layer % 2 == 0:
            x = x + even_mixer(h, l0_w_in, l0_a_ln_g, l0_a_ln_b, l0_a_ws, l0_a_bs,
                               l0_b_q_norm, l0_b_k_norm, l0_b_sinks, l0_w_out)
        else:
            x = x + odd_mixer(h, l1_w_in, l1_c_conv_w, l1_c_conv_b, l1_c_ln_g, l1_c_ln_b, l1_w_out)
        x = x + hierarchical_moe(rms_norm(x, ffn_norms[layer]), *moe[layer])
    return x
```

```python
import functools

import jax
import jax.numpy as jnp
from jax import lax
from jax.experimental import pallas as pl
from jax.experimental.pallas import tpu as pltpu

F32 = jnp.float32
BF16 = jnp.bfloat16

HEAD_DIM = 128
CHUNK = 128
EPS = 1e-6
N_GROUPS = 8
EXPERTS_PER_GROUP = 8
N_EXPERTS = N_GROUPS * EXPERTS_PER_GROUP
EXPERT_BLOCK = 128
CONV_WIDTH = 31
CONV_HALO = 32
ROUTE_LANES = 128
SB_LOG_ZERO = -104.0
V7X_VMEM_LIMIT = 56 * 1024 * 1024


def _params(*sem):
    return pltpu.CompilerParams(dimension_semantics=sem, vmem_limit_bytes=V7X_VMEM_LIMIT)


def _rmsnorm_kernel(x_ref, g_ref, o_ref):
    x = x_ref[...]
    ms = jnp.mean(x * x, axis=-1, keepdims=True)
    o_ref[...] = (x * lax.rsqrt(ms + EPS) * g_ref[...]).astype(o_ref.dtype)


def _rmsnorm(x, g, tm=256):
    t, d = x.shape
    return pl.pallas_call(
        _rmsnorm_kernel,
        grid=(t // tm,),
        in_specs=[pl.BlockSpec((tm, d), lambda i: (i, 0)),
                  pl.BlockSpec((1, d), lambda i: (0, 0))],
        out_specs=pl.BlockSpec((tm, d), lambda i: (i, 0)),
        out_shape=jax.ShapeDtypeStruct((t, d), BF16),
        compiler_params=_params("parallel"),
        name="rmsnorm",
    )(x, g.reshape(1, d))


def _matmul_kernel(a_ref, w_ref, o_ref):
    o_ref[...] = jnp.dot(a_ref[...], w_ref[...], preferred_element_type=F32).astype(o_ref.dtype)


def _matmul(a, w, tm, tn):
    m, k = a.shape
    n = w.shape[1]
    tm = min(tm, m)
    return pl.pallas_call(
        _matmul_kernel,
        grid=(m // tm, n // tn),
        in_specs=[pl.BlockSpec((tm, k), lambda i, j: (i, 0)),
                  pl.BlockSpec((k, tn), lambda i, j: (0, j))],
        out_specs=pl.BlockSpec((tm, tn), lambda i, j: (i, j)),
        out_shape=jax.ShapeDtypeStruct((m, n), BF16),
        compiler_params=_params("parallel", "arbitrary"),
        name="in_proj",
    )(a, w)


def _out_proj_kernel(x_ref, a_ref, b_ref, wa_ref, wb_ref, o_ref):
    acc = jnp.dot(a_ref[...], wa_ref[...], preferred_element_type=F32)
    acc = acc + jnp.dot(b_ref[...], wb_ref[...], preferred_element_type=F32)
    o_ref[...] = x_ref[...] + acc


def _out_proj(x, ya, yb, w, tm=1024, tn=512):
    t, d = x.shape
    ka, kb = ya.shape[1], yb.shape[1]
    assert ka == kb and w.shape == (ka + kb, d)
    tm = min(tm, t)
    return pl.pallas_call(
        _out_proj_kernel,
        grid=(t // tm, d // tn),
        in_specs=[pl.BlockSpec((tm, tn), lambda i, j: (i, j)),
                  pl.BlockSpec((tm, ka), lambda i, j: (i, 0)),
                  pl.BlockSpec((tm, kb), lambda i, j: (i, 0)),
                  pl.BlockSpec((ka, tn), lambda i, j: (0, j)),
                  pl.BlockSpec((kb, tn), lambda i, j: (1, j))],
        out_specs=pl.BlockSpec((tm, tn), lambda i, j: (i, j)),
        out_shape=jax.ShapeDtypeStruct((t, d), F32),
        compiler_params=_params("parallel", "arbitrary"),
        name="out_proj",
    )(x, ya, yb, w, w)


def _layer_norm(x, g, b):
    mu = jnp.mean(x, axis=-1, keepdims=True)
    xc = x - mu
    var = jnp.mean(xc * xc, axis=-1, keepdims=True)
    return xc * lax.rsqrt(var + EPS) * g + b


def _gelu(x):
    return 0.5 * x * (1.0 + lax.erf(x * (2.0 ** -0.5)))


def _gmlp_kernel(u_ref, v_ref, lng_ref, lnb_ref, w_ref, bs_ref, o_ref):
    n_groups = w_ref.shape[0]
    v = _gelu(v_ref[...].astype(F32))
    vn = _layer_norm(v, lng_ref[...], lnb_ref[...]).astype(BF16)
    for g in range(n_groups):
        cols = slice(g * HEAD_DIM, (g + 1) * HEAD_DIM)
        s = jnp.dot(w_ref[g], vn[:, cols], preferred_element_type=F32) + bs_ref[:, cols]
        u = _gelu(u_ref[:, cols].astype(F32))
        o_ref[:, cols] = (u * s).astype(o_ref.dtype)


def _gmlp(proj, ln_g, ln_b, w_s, b_s):
    t = proj.shape[0]
    n_groups = w_s.shape[0]
    width = n_groups * HEAD_DIM
    causal = jnp.tril(jnp.ones((CHUNK, CHUNK), dtype=bool))
    w = jnp.where(causal[None], w_s, 0.0).astype(BF16)
    bs = jnp.repeat(b_s.T.astype(F32), HEAD_DIM, axis=1)
    return pl.pallas_call(
        _gmlp_kernel,
        grid=(t // CHUNK,),
        in_specs=[pl.BlockSpec((CHUNK, width), lambda i: (i, 0)),
                  pl.BlockSpec((CHUNK, width), lambda i: (i, 1)),
                  pl.BlockSpec((1, width), lambda i: (0, 0)),
                  pl.BlockSpec((1, width), lambda i: (0, 0)),
                  pl.BlockSpec((n_groups, CHUNK, CHUNK), lambda i: (0, 0, 0)),
                  pl.BlockSpec((CHUNK, width), lambda i: (0, 0))],
        out_specs=pl.BlockSpec((CHUNK, width), lambda i: (i, 0)),
        out_shape=jax.ShapeDtypeStruct((t, width), BF16),
        compiler_params=_params("parallel"),
        name="gmlp",
    )(proj, proj, ln_g.reshape(1, width), ln_b.reshape(1, width), w, bs)


def _swa_kernel(slopes_ref, sinks_ref, q_ref, kc_ref, kp_ref, vc_ref, vp_ref, qg_ref, kg_ref, o_ref, *, grp):
    n = pl.program_id(0)
    h = pl.program_id(1)
    k = jnp.concatenate([kp_ref[...], kc_ref[...]], axis=0).astype(F32)
    k = k * lax.rsqrt(jnp.mean(k * k, axis=-1, keepdims=True) + EPS) * kg_ref[...]
    kb = k.astype(BF16)
    vb = jnp.concatenate([vp_ref[...], vc_ref[...]], axis=0)
    qi = lax.broadcasted_iota(jnp.int32, (CHUNK, 2 * CHUNK), 0)
    kj = lax.broadcasted_iota(jnp.int32, (CHUNK, 2 * CHUNK), 1)
    dist = qi + CHUNK - kj
    valid = (dist >= 0) & (dist < CHUNK) & (n * CHUNK - CHUNK + kj >= 0)
    distf = dist.astype(F32)
    for g in range(grp):
        cols = slice(g * HEAD_DIM, (g + 1) * HEAD_DIM)
        q = q_ref[:, cols].astype(F32)
        q = q * lax.rsqrt(jnp.mean(q * q, axis=-1, keepdims=True) + EPS) * qg_ref[...]
        s = lax.dot_general(q.astype(BF16), kb, (((1,), (1,)), ((), ())),
                            preferred_element_type=F32) * (HEAD_DIM ** -0.5)
        slope = slopes_ref[h * grp + g]
        sink = sinks_ref[h * grp + g]
        s = jnp.where(valid, s - slope * distf, -jnp.inf)
        m = jnp.maximum(jnp.max(s, axis=-1, keepdims=True), sink)
        p = jnp.exp(s - m)
        denom = jnp.sum(p, axis=-1, keepdims=True) + jnp.exp(sink - m)
        o = jnp.dot(p.astype(BF16), vb, preferred_element_type=F32) / denom
        o_ref[:, cols] = o.astype(o_ref.dtype)


def _swa(proj, q_gain, k_gain, sinks, n_heads, n_kv, col0):
    t = proj.shape[0]
    grp = n_heads // n_kv
    qb0 = col0 // (grp * HEAD_DIM)
    kb0 = (col0 + n_heads * HEAD_DIM) // HEAD_DIM
    vb0 = kb0 + n_kv
    slopes = 2.0 ** (-8.0 * jnp.arange(1, n_heads + 1, dtype=F32) / n_heads)
    prev = lambda n: jnp.maximum(n - 1, 0)
    return pl.pallas_call(
        functools.partial(_swa_kernel, grp=grp),
        grid_spec=pltpu.PrefetchScalarGridSpec(
            num_scalar_prefetch=2,
            grid=(t // CHUNK, n_kv),
            in_specs=[pl.BlockSpec((CHUNK, grp * HEAD_DIM), lambda n, h, *_: (n, qb0 + h)),
                      pl.BlockSpec((CHUNK, HEAD_DIM), lambda n, h, *_: (n, kb0 + h)),
                      pl.BlockSpec((CHUNK, HEAD_DIM), lambda n, h, *_: (prev(n), kb0 + h)),
                      pl.BlockSpec((CHUNK, HEAD_DIM), lambda n, h, *_: (n, vb0 + h)),
                      pl.BlockSpec((CHUNK, HEAD_DIM), lambda n, h, *_: (prev(n), vb0 + h)),
                      pl.BlockSpec((1, HEAD_DIM), lambda n, h, *_: (0, 0)),
                      pl.BlockSpec((1, HEAD_DIM), lambda n, h, *_: (0, 0))],
            out_specs=pl.BlockSpec((CHUNK, grp * HEAD_DIM), lambda n, h, *_: (n, h)),
        ),
        out_shape=jax.ShapeDtypeStruct((t, n_heads * HEAD_DIM), BF16),
        compiler_params=_params("parallel", "arbitrary"),
        name="swa",
    )(slopes, sinks.astype(F32), proj, proj, proj, proj, proj,
      q_gain.reshape(1, HEAD_DIM).astype(F32), k_gain.reshape(1, HEAD_DIM).astype(F32))


def _conformer_kernel(a_ref, g_ref, ap_ref, gp_ref, w_ref, cb_ref, lg_ref, lb_ref, o_ref, hbuf, ybuf):
    i = pl.program_id(0)
    tc, width = a_ref.shape
    hp = ap_ref[...].astype(F32) * jax.nn.sigmoid(gp_ref[...].astype(F32))
    hbuf[0:CONV_HALO, :] = jnp.where(i > 0, hp, 0.0)
    hbuf[CONV_HALO:CONV_HALO + tc, :] = a_ref[...].astype(F32) * jax.nn.sigmoid(g_ref[...].astype(F32))
    base = CONV_HALO - (CONV_WIDTH - 1)
    for c in range(width // HEAD_DIM):
        cols = slice(c * HEAD_DIM, (c + 1) * HEAD_DIM)
        acc = jnp.zeros((tc, HEAD_DIM), F32)
        for j in range(CONV_WIDTH):
            acc = acc + w_ref[j:j + 1, cols] * hbuf[base + j:base + j + tc, cols]
        ybuf[:, cols] = acc
    y = _layer_norm(ybuf[...] + cb_ref[...], lg_ref[...], lb_ref[...])
    o_ref[...] = (y * jax.nn.sigmoid(y)).astype(o_ref.dtype)


def _conformer(proj, conv_w, conv_b, ln_g, ln_b, tc=256):
    t = proj.shape[0]
    width = conv_w.shape[1]
    tc = min(tc, t)
    halo_blocks = tc // CONV_HALO
    prev = lambda i: jnp.maximum(i * halo_blocks - 1, 0)
    vec = lambda: pl.BlockSpec((1, width), lambda i: (0, 0))
    return pl.pallas_call(
        _conformer_kernel,
        grid=(t // tc,),
        in_specs=[pl.BlockSpec((tc, width), lambda i: (i, 0)),
                  pl.BlockSpec((tc, width), lambda i: (i, 1)),
                  pl.BlockSpec((CONV_HALO, width), lambda i: (prev(i), 0)),
                  pl.BlockSpec((CONV_HALO, width), lambda i: (prev(i), 1)),
                  pl.BlockSpec((CONV_WIDTH, width), lambda i: (0, 0)),
                  vec(), vec(), vec()],
        out_specs=pl.BlockSpec((tc, width), lambda i: (i, 0)),
        out_shape=jax.ShapeDtypeStruct((t, width), BF16),
        scratch_shapes=[pltpu.VMEM((CONV_HALO + tc, width), F32),
                        pltpu.VMEM((tc, width), F32)],
        compiler_params=_params("parallel"),
        name="conformer",
    )(proj, proj, proj, proj, conv_w.astype(F32), conv_b.reshape(1, width).astype(F32),
      ln_g.reshape(1, width).astype(F32), ln_b.reshape(1, width).astype(F32))


def _stickbreak_kernel(q_ref, k_ref, v_ref, o_ref):
    i = pl.program_id(1)
    tq = q_ref.shape[0]
    q = q_ref[...]
    row = lax.broadcasted_iota(jnp.int32, (tq, tq), 0)
    col = lax.broadcasted_iota(jnp.int32, (tq, tq), 1)
    later_keys = (row > col).astype(BF16)
    strictly_causal = col < row

    def key_block(j, c, acc, diagonal):
        start = pl.multiple_of(j * tq, tq)
        kb = k_ref[pl.ds(start, tq), :]
        vb = v_ref[pl.ds(start, tq), :]
        z = lax.dot_general(q, kb, (((1,), (1,)), ((), ())),
                            preferred_element_type=F32) * (HEAD_DIM ** -0.5)
        softplus = jnp.maximum(z, 0.0) + jnp.log1p(jnp.exp(-jnp.abs(z)))
        log_fail = -softplus
        if diagonal:
            log_fail = jnp.where(strictly_causal, log_fail, 0.0)
        hi = log_fail.astype(BF16)
        lo = (log_fail - hi.astype(F32)).astype(BF16)
        later = (jnp.dot(hi, later_keys, preferred_element_type=F32)
                 + jnp.dot(lo, later_keys, preferred_element_type=F32) + c)
        a = jnp.exp(z - softplus + later)
        if diagonal:
            a = jnp.where(strictly_causal, a, 0.0)
        acc = acc + jnp.dot(a.astype(BF16), vb, preferred_element_type=F32)
        c = c + jnp.sum(log_fail, axis=-1, keepdims=True)
        return c, acc

    c, acc = key_block(i, jnp.zeros((tq, 1), F32), jnp.zeros((tq, HEAD_DIM), F32), True)

    def cond(state):
        j, c, _ = state
        return jnp.logical_and(j >= 0, jnp.max(c) > SB_LOG_ZERO)

    def body(state):
        j, c, acc = state
        c, acc = key_block(j, c, acc, False)
        return j - 1, c, acc

    _, _, acc = lax.while_loop(cond, body, (i - 1, c, acc))
    o_ref[...] = acc.astype(o_ref.dtype)


def _stickbreak(proj, n_heads, col0):
    t = proj.shape[0]
    qb0 = col0 // HEAD_DIM
    kb0 = qb0 + n_heads
    vb0 = kb0 + n_heads
    return pl.pallas_call(
        _stickbreak_kernel,
        grid=(n_heads, t // CHUNK),
        in_specs=[pl.BlockSpec((CHUNK, HEAD_DIM), lambda h, i: (i, qb0 + h)),
                  pl.BlockSpec((t, HEAD_DIM), lambda h, i: (0, kb0 + h)),
                  pl.BlockSpec((t, HEAD_DIM), lambda h, i: (0, vb0 + h))],
        out_specs=pl.BlockSpec((CHUNK, HEAD_DIM), lambda h, i: (i, h)),
        out_shape=jax.ShapeDtypeStruct((t, n_heads * HEAD_DIM), BF16),
        compiler_params=_params("parallel", "arbitrary"),
        name="stickbreak",
    )(proj, proj, proj)


def _router_kernel(x_ref, g_ref, whi_ref, wlo_ref, b_ref, hp_ref, route_ref):
    x = x_ref[...]
    d = x.shape[1]
    h = x * lax.rsqrt(jnp.mean(x * x, axis=-1, keepdims=True) + EPS) * g_ref[...]
    h_hi = h.astype(BF16)
    h_hif = h_hi.astype(F32)
    h_lo = (h - h_hif).astype(BF16)
    logits = (jnp.dot(h_hi, whi_ref[...], preferred_element_type=F32)
              + jnp.dot(h_lo, whi_ref[...], preferred_element_type=F32)
              + jnp.dot(h_hi, wlo_ref[...], preferred_element_type=F32)) + b_ref[...]
    lane = lax.broadcasted_iota(jnp.int32, logits.shape, 1)
    neg = -jnp.inf

    def first_max(vals):
        m = jnp.max(vals, axis=-1, keepdims=True)
        idx = jnp.min(jnp.where(vals == m, lane, ROUTE_LANES), axis=-1, keepdims=True)
        return m, idx

    gl = jnp.where(lane < N_GROUPS, logits, neg)
    gmax, g_sel = first_max(gl)
    g_weight = 1.0 / jnp.sum(jnp.exp(gl - gmax), axis=-1, keepdims=True)
    expert = lane - N_GROUPS
    in_group = (expert >= 0) & (expert < N_EXPERTS) & ((expert >> 3) == g_sel)
    el = jnp.where(in_group, logits, neg)
    m1, i1 = first_max(el)
    m2, i2 = first_max(jnp.where(lane == i1, neg, el))
    e21 = jnp.exp(m2 - m1)
    w1 = g_weight / (1.0 + e21)
    w2 = w1 * e21
    rec = jnp.where(lane == 0, (i1 - N_GROUPS).astype(F32),
                    jnp.where(lane == 1, (i2 - N_GROUPS).astype(F32),
                              jnp.where(lane == 2, w1, jnp.where(lane == 3, w2, 0.0))))
    route_ref[...] = rec
    bits = lax.bitcast_convert_type(h_hif, jnp.uint32)
    hp_ref[...] = (bits[:, :d // 2] >> 16) | (bits[:, d // 2:] & jnp.uint32(0xFFFF0000))


def _router(x, norm_g, wr_g, br_g, wr_e, br_e, tm=256):
    t, d = x.shape
    w = jnp.zeros((d, ROUTE_LANES), F32).at[:, :N_GROUPS].set(wr_g).at[:, N_GROUPS:N_GROUPS + N_EXPERTS].set(wr_e)
    b = jnp.zeros((1, ROUTE_LANES), F32).at[0, :N_GROUPS].set(br_g).at[0, N_GROUPS:N_GROUPS + N_EXPERTS].set(br_e)
    w_hi = w.astype(BF16)
    w_lo = (w - w_hi.astype(F32)).astype(BF16)
    tm = min(tm, t)
    return pl.pallas_call(
        _router_kernel,
        grid=(t // tm,),
        in_specs=[pl.BlockSpec((tm, d), lambda i: (i, 0)),
                  pl.BlockSpec((1, d), lambda i: (0, 0)),
                  pl.BlockSpec((d, ROUTE_LANES), lambda i: (0, 0)),
                  pl.BlockSpec((d, ROUTE_LANES), lambda i: (0, 0)),
                  pl.BlockSpec((1, ROUTE_LANES), lambda i: (0, 0))],
        out_specs=[pl.BlockSpec((tm, d // 2), lambda i: (i, 0)),
                   pl.BlockSpec((tm, ROUTE_LANES), lambda i: (i, 0))],
        out_shape=[jax.ShapeDtypeStruct((t, d // 2), jnp.uint32),
                   jax.ShapeDtypeStruct((t, ROUTE_LANES), F32)],
        compiler_params=_params("parallel"),
        name="router",
    )(x, norm_g.reshape(1, d), w_hi, w_lo, b)


def _dispatch_tables(route, t):
    n_assign = 2 * t
    flat_e = route[:, 0:2].astype(jnp.int32).reshape(-1)
    flat_w = route[:, 2:4].reshape(-1)
    flat_tok = jnp.repeat(jnp.arange(t, dtype=jnp.int32), 2)
    order = jnp.argsort(flat_e)
    sorted_e = flat_e[order]
    counts = jnp.bincount(flat_e, length=N_EXPERTS).astype(jnp.int32)
    padded = (counts + EXPERT_BLOCK - 1) // EXPERT_BLOCK * EXPERT_BLOCK
    start = jnp.cumsum(counts) - counts
    padded_end = jnp.cumsum(padded)
    padded_start = padded_end - padded
    slot = padded_start[sorted_e] + jnp.arange(n_assign, dtype=jnp.int32) - start[sorted_e]
    n_blocks = -(-n_assign // EXPERT_BLOCK) + N_EXPERTS
    n_slots = n_blocks * EXPERT_BLOCK
    slot_tok = jnp.zeros((n_slots,), jnp.int32).at[slot].set(flat_tok[order])
    slot_w = jnp.zeros((n_slots,), F32).at[slot].set(flat_w[order])
    assign_slot = jnp.zeros((n_assign,), jnp.int32).at[order].set(slot)
    n_used = (padded_end[-1] // EXPERT_BLOCK).astype(jnp.int32)
    block_start = jnp.arange(n_blocks, dtype=jnp.int32) * EXPERT_BLOCK
    block_e = jnp.minimum(jnp.searchsorted(padded_end, block_start, side='right'), N_EXPERTS - 1).astype(jnp.int32)
    block_e = jnp.where(jnp.arange(n_blocks) < n_used, block_e, block_e[jnp.maximum(n_used - 1, 0)])
    return block_e, slot_tok, slot_w.reshape(n_slots, 1), assign_slot, n_used.reshape(1), n_blocks


def _expert_changed(be_ref, b):
    return jnp.logical_or(b == 0, be_ref[b] != be_ref[jnp.maximum(b - 1, 0)])


def _moe_up_kernel(be_ref, st_ref, nu_ref, hp_hbm, wg_ref, wu_ref, hid_ref, xbuf, sem, wg_bf, wu_bf):
    b = pl.program_id(0)
    n_used = nu_ref[0]

    def start_gather(blk, buf):
        def body(r, carry):
            tok = st_ref[blk * EXPERT_BLOCK + r]
            pltpu.make_async_copy(hp_hbm.at[pl.ds(tok, 1)], xbuf.at[buf, pl.ds(r, 1)], sem.at[buf]).start()
            return carry
        lax.fori_loop(0, EXPERT_BLOCK, body, 0)

    @pl.when(b == 0)
    def _():
        start_gather(0, 0)

    @pl.when(b + 1 < n_used)
    def _():
        start_gather(b + 1, (b + 1) % 2)

    @pl.when(b < n_used)
    def _():
        buf = b % 2
        pltpu.make_async_copy(hp_hbm.at[pl.ds(0, EXPERT_BLOCK)], xbuf.at[buf], sem.at[buf]).wait()

        @pl.when(_expert_changed(be_ref, b))
        def _():
            wg_bf[...] = wg_ref[...].astype(BF16)
            wu_bf[...] = wu_ref[...].astype(BF16)

        words = xbuf[buf]
        lo = lax.bitcast_convert_type(words << 16, F32)
        hi = lax.bitcast_convert_type(words & jnp.uint32(0xFFFF0000), F32)
        x = jnp.concatenate([lo, hi], axis=1).astype(BF16)
        gate = jnp.dot(x, wg_bf[...], preferred_element_type=F32)
        up = jnp.dot(x, wu_bf[...], preferred_element_type=F32)
        hid_ref[...] = (gate * jax.nn.sigmoid(gate) * up).astype(hid_ref.dtype)

    @pl.when(b >= n_used)
    def _():
        hid_ref[...] = jnp.zeros_like(hid_ref)


def _moe_down_kernel(be_ref, nu_ref, hid_ref, sw_ref, wd_ref, y_ref, wd_bf):
    b = pl.program_id(0)

    @pl.when(b < nu_ref[0])
    def _():
        @pl.when(_expert_changed(be_ref, b))
        def _():
            wd_bf[...] = wd_ref[...].astype(BF16)
        y_ref[...] = jnp.dot(hid_ref[...], wd_bf[...], preferred_element_type=F32) * sw_ref[...]

    @pl.when(b >= nu_ref[0])
    def _():
        y_ref[...] = jnp.zeros_like(y_ref)


def _combine_kernel(as_ref, x_ref, y_hbm, o_ref, ybuf, sem):
    i = pl.program_id(0)
    tm = x_ref.shape[0]

    def start_gather(tile, buf):
        def body(r, carry):
            a = 2 * (tile * tm + r)
            for k in range(2):
                pltpu.make_async_copy(y_hbm.at[pl.ds(as_ref[a + k], 1)], ybuf.at[buf, k, pl.ds(r, 1)],
                                      sem.at[buf]).start()
            return carry
        lax.fori_loop(0, tm, body, 0)

    @pl.when(i == 0)
    def _():
        start_gather(0, 0)

    @pl.when(i + 1 < pl.num_programs(0))
    def _():
        start_gather(i + 1, (i + 1) % 2)

    buf = i % 2
    for k in range(2):
        pltpu.make_async_copy(y_hbm.at[pl.ds(0, tm)], ybuf.at[buf, k], sem.at[buf]).wait()
    o_ref[...] = x_ref[...] + ybuf[buf, 0] + ybuf[buf, 1]


def _moe(x, norm_g, wr_g, br_g, wr_e, br_e, w_gate, w_up, w_down, tm_combine=128):
    t, d = x.shape
    d_exp = w_gate.shape[2]
    hp, route = _router(x, norm_g, wr_g, br_g, wr_e, br_e)
    block_e, slot_tok, slot_w, assign_slot, n_used, n_blocks = _dispatch_tables(route, t)
    n_slots = n_blocks * EXPERT_BLOCK

    hid = pl.pallas_call(
        _moe_up_kernel,
        grid_spec=pltpu.PrefetchScalarGridSpec(
            num_scalar_prefetch=3,
            grid=(n_blocks,),
            in_specs=[pl.BlockSpec(memory_space=pl.ANY),
                      pl.BlockSpec((None, d, d_exp), lambda b, be, st, nu: (be[b], 0, 0)),
                      pl.BlockSpec((None, d, d_exp), lambda b, be, st, nu: (be[b], 0, 0))],
            out_specs=pl.BlockSpec((EXPERT_BLOCK, d_exp), lambda b, be, st, nu: (b, 0)),
            scratch_shapes=[pltpu.VMEM((2, EXPERT_BLOCK, d // 2), jnp.uint32),
                            pltpu.SemaphoreType.DMA((2,)),
                            pltpu.VMEM((d, d_exp), BF16),
                            pltpu.VMEM((d, d_exp), BF16)],
        ),
        out_shape=jax.ShapeDtypeStruct((n_slots, d_exp), BF16),
        compiler_params=_params("arbitrary"),
        name="moe_up",
    )(block_e, slot_tok, n_used, hp, w_gate, w_up)

    y = pl.pallas_call(
        _moe_down_kernel,
        grid_spec=pltpu.PrefetchScalarGridSpec(
            num_scalar_prefetch=2,
            grid=(n_blocks,),
            in_specs=[pl.BlockSpec((EXPERT_BLOCK, d_exp), lambda b, be, nu: (b, 0)),
                      pl.BlockSpec((EXPERT_BLOCK, 1), lambda b, be, nu: (b, 0)),
                      pl.BlockSpec((None, d_exp, d), lambda b, be, nu: (be[b], 0, 0))],
            out_specs=pl.BlockSpec((EXPERT_BLOCK, d), lambda b, be, nu: (b, 0)),
            scratch_shapes=[pltpu.VMEM((d_exp, d), BF16)],
        ),
        out_shape=jax.ShapeDtypeStruct((n_slots, d), F32),
        compiler_params=_params("arbitrary"),
        name="moe_down",
    )(block_e, n_used, hid, slot_w, w_down)

    tm = min(tm_combine, t)
    return pl.pallas_call(
        _combine_kernel,
        grid_spec=pltpu.PrefetchScalarGridSpec(
            num_scalar_prefetch=1,
            grid=(t // tm,),
            in_specs=[pl.BlockSpec((tm, d), lambda i, a: (i, 0)),
                      pl.BlockSpec(memory_space=pl.ANY)],
            out_specs=pl.BlockSpec((tm, d), lambda i, a: (i, 0)),
            scratch_shapes=[pltpu.VMEM((2, 2, tm, d), F32),
                            pltpu.SemaphoreType.DMA((2,))],
        ),
        out_shape=jax.ShapeDtypeStruct((t, d), F32),
        compiler_params=_params("arbitrary"),
        name="moe_combine",
    )(assign_slot, x, y)


def _even_layer_mixer(x, mix_norm, w_in, a_ln_g, a_ln_b, a_ws, a_bs, b_q_norm, b_k_norm, b_sinks, w_out):
    n_groups = a_ws.shape[0]
    a_width = n_groups * HEAD_DIM
    n_heads = b_sinks.shape[0]
    n_kv = (w_in.shape[1] - 2 * a_width - n_heads * HEAD_DIM) // (2 * HEAD_DIM)
    h = _rmsnorm(x, mix_norm)
    proj = _matmul(h, w_in.astype(BF16), tm=1024, tn=1024)
    y_a = _gmlp(proj, a_ln_g, a_ln_b, a_ws, a_bs)
    y_b = _swa(proj, b_q_norm, b_k_norm, b_sinks, n_heads, n_kv, col0=2 * a_width)
    return _out_proj(x, y_a, y_b, w_out.astype(BF16))


def _odd_layer_mixer(x, mix_norm, w_in, c_conv_w, c_conv_b, c_ln_g, c_ln_b, w_out):
    c_width = c_conv_w.shape[1]
    n_heads = (w_in.shape[1] - 2 * c_width) // (3 * HEAD_DIM)
    h = _rmsnorm(x, mix_norm)
    proj = _matmul(h, w_in.astype(BF16), tm=1024, tn=1024)
    y_c = _conformer(proj, c_conv_w, c_conv_b, c_ln_g, c_ln_b)
    y_d = _stickbreak(proj, n_heads, col0=2 * c_width)
    return _out_proj(x, y_c, y_d, w_out.astype(BF16))


def kernel(x, l0_mix_norm, l0_w_in, l0_a_ln_g, l0_a_ln_b, l0_a_ws, l0_a_bs, l0_b_q_norm, l0_b_k_norm, l0_b_sinks, l0_w_out, l0_ffn_norm, l0_router_group_w, l0_router_group_b, l0_router_expert_w, l0_router_expert_b, l0_w_gate, l0_w_up, l0_w_down, l1_mix_norm, l1_w_in, l1_c_conv_w, l1_c_conv_b, l1_c_ln_g, l1_c_ln_b, l1_w_out, l1_ffn_norm, l1_router_group_w, l1_router_group_b, l1_router_expert_w, l1_router_expert_b, l1_w_gate, l1_w_up, l1_w_down):
    bsz, seq, d = x.shape
    xt = x.reshape(bsz * seq, d)
    assert bsz == 1, "the mixers index positions by row; one sequence per call"
    xt = _even_layer_mixer(xt, l0_mix_norm, l0_w_in, l0_a_ln_g, l0_a_ln_b, l0_a_ws, l0_a_bs,
                           l0_b_q_norm, l0_b_k_norm, l0_b_sinks, l0_w_out)
    xt = _moe(xt, l0_ffn_norm, l0_router_group_w, l0_router_group_b, l0_router_expert_w,
              l0_router_expert_b, l0_w_gate, l0_w_up, l0_w_down)
    xt = _odd_layer_mixer(xt, l1_mix_norm, l1_w_in, l1_c_conv_w, l1_c_conv_b, l1_c_ln_g, l1_c_ln_b, l1_w_out)
    xt = _moe(xt, l1_ffn_norm, l1_router_group_w, l1_router_group_b, l1_router_expert_w,
              l1_router_expert_b, l1_w_gate, l1_w_up, l1_w_down)
    return xt.reshape(bsz, seq, d)
```

```python
import functools

import jax
import jax.numpy as jnp
from jax import lax
from jax.experimental import pallas as pl
from jax.experimental.pallas import tpu as pltpu

F32 = jnp.float32
BF16 = jnp.bfloat16

HEAD_DIM = 128
CHUNK = 128
EPS = 1e-6
N_GROUPS = 8
EXPERTS_PER_GROUP = 8
N_EXPERTS = N_GROUPS * EXPERTS_PER_GROUP
EXPERT_BLOCK = 128
CONV_WIDTH = 31
CONV_HALO = 32
ROUTE_LANES = 128
SB_LOG_ZERO = -104.0
MOE_MAX_ROWS = 768
V7X_VMEM_LIMIT = 56 * 1024 * 1024
V7X_VMEM_LIMIT_MOE = 60 * 1024 * 1024


def _params(*sem):
    return pltpu.CompilerParams(dimension_semantics=sem, vmem_limit_bytes=V7X_VMEM_LIMIT)


def _rmsnorm_kernel(x_ref, g_ref, o_ref):
    x = x_ref[...]
    ms = jnp.mean(x * x, axis=-1, keepdims=True)
    o_ref[...] = (x * lax.rsqrt(ms + EPS) * g_ref[...]).astype(o_ref.dtype)


def _rmsnorm(x, g, tm=256):
    t, d = x.shape
    return pl.pallas_call(
        _rmsnorm_kernel,
        grid=(t // tm,),
        in_specs=[pl.BlockSpec((tm, d), lambda i: (i, 0)),
                  pl.BlockSpec((1, d), lambda i: (0, 0))],
        out_specs=pl.BlockSpec((tm, d), lambda i: (i, 0)),
        out_shape=jax.ShapeDtypeStruct((t, d), BF16),
        compiler_params=_params("parallel"),
        name="rmsnorm",
    )(x, g.reshape(1, d))


def _matmul_kernel(a_ref, w_ref, o_ref):
    o_ref[...] = jnp.dot(a_ref[...], w_ref[...], preferred_element_type=F32).astype(o_ref.dtype)


def _matmul(a, w, tm, tn):
    m, k = a.shape
    n = w.shape[1]
    tm = min(tm, m)
    return pl.pallas_call(
        _matmul_kernel,
        grid=(m // tm, n // tn),
        in_specs=[pl.BlockSpec((tm, k), lambda i, j: (i, 0)),
                  pl.BlockSpec((k, tn), lambda i, j: (0, j))],
        out_specs=pl.BlockSpec((tm, tn), lambda i, j: (i, j)),
        out_shape=jax.ShapeDtypeStruct((m, n), BF16),
        compiler_params=_params("parallel", "arbitrary"),
        name="in_proj",
    )(a, w)


def _out_proj_kernel(x_ref, a_ref, b_ref, wa_ref, wb_ref, o_ref):
    acc = jnp.dot(a_ref[...], wa_ref[...], preferred_element_type=F32)
    acc = acc + jnp.dot(b_ref[...], wb_ref[...], preferred_element_type=F32)
    o_ref[...] = x_ref[...] + acc


def _out_proj(x, ya, yb, w, tm=1024, tn=512):
    t, d = x.shape
    ka, kb = ya.shape[1], yb.shape[1]
    assert ka == kb and w.shape == (ka + kb, d)
    tm = min(tm, t)
    return pl.pallas_call(
        _out_proj_kernel,
        grid=(t // tm, d // tn),
        in_specs=[pl.BlockSpec((tm, tn), lambda i, j: (i, j)),
                  pl.BlockSpec((tm, ka), lambda i, j: (i, 0)),
                  pl.BlockSpec((tm, kb), lambda i, j: (i, 0)),
                  pl.BlockSpec((ka, tn), lambda i, j: (0, j)),
                  pl.BlockSpec((kb, tn), lambda i, j: (1, j))],
        out_specs=pl.BlockSpec((tm, tn), lambda i, j: (i, j)),
        out_shape=jax.ShapeDtypeStruct((t, d), F32),
        compiler_params=_params("parallel", "arbitrary"),
        name="out_proj",
    )(x, ya, yb, w, w)


def _layer_norm(x, g, b):
    mu = jnp.mean(x, axis=-1, keepdims=True)
    xc = x - mu
    var = jnp.mean(xc * xc, axis=-1, keepdims=True)
    return xc * lax.rsqrt(var + EPS) * g + b


def _gelu(x):
    return 0.5 * x * (1.0 + lax.erf(x * (2.0 ** -0.5)))


def _gmlp_kernel(u_ref, v_ref, lng_ref, lnb_ref, w_ref, bs_ref, o_ref):
    n_groups = w_ref.shape[0]
    v = _gelu(v_ref[...].astype(F32))
    vn = _layer_norm(v, lng_ref[...], lnb_ref[...]).astype(BF16)
    for g in range(n_groups):
        cols = slice(g * HEAD_DIM, (g + 1) * HEAD_DIM)
        s = jnp.dot(w_ref[g], vn[:, cols], preferred_element_type=F32) + bs_ref[:, cols]
        u = _gelu(u_ref[:, cols].astype(F32))
        o_ref[:, cols] = (u * s).astype(o_ref.dtype)


def _gmlp(proj, ln_g, ln_b, w_s, b_s):
    t = proj.shape[0]
    n_groups = w_s.shape[0]
    width = n_groups * HEAD_DIM
    causal = jnp.tril(jnp.ones((CHUNK, CHUNK), dtype=bool))
    w = jnp.where(causal[None], w_s, 0.0).astype(BF16)
    bs = jnp.repeat(b_s.T.astype(F32), HEAD_DIM, axis=1)
    return pl.pallas_call(
        _gmlp_kernel,
        grid=(t // CHUNK,),
        in_specs=[pl.BlockSpec((CHUNK, width), lambda i: (i, 0)),
                  pl.BlockSpec((CHUNK, width), lambda i: (i, 1)),
                  pl.BlockSpec((1, width), lambda i: (0, 0)),
                  pl.BlockSpec((1, width), lambda i: (0, 0)),
                  pl.BlockSpec((n_groups, CHUNK, CHUNK), lambda i: (0, 0, 0)),
                  pl.BlockSpec((CHUNK, width), lambda i: (0, 0))],
        out_specs=pl.BlockSpec((CHUNK, width), lambda i: (i, 0)),
        out_shape=jax.ShapeDtypeStruct((t, width), BF16),
        compiler_params=_params("parallel"),
        name="gmlp",
    )(proj, proj, ln_g.reshape(1, width), ln_b.reshape(1, width), w, bs)


def _swa_kernel(slopes_ref, sinks_ref, q_ref, kc_ref, kp_ref, vc_ref, vp_ref, qg_ref, kg_ref, o_ref, *, grp):
    n = pl.program_id(0)
    h = pl.program_id(1)
    k = jnp.concatenate([kp_ref[...], kc_ref[...]], axis=0).astype(F32)
    k = k * lax.rsqrt(jnp.mean(k * k, axis=-1, keepdims=True) + EPS) * kg_ref[...]
    kb = k.astype(BF16)
    vb = jnp.concatenate([vp_ref[...], vc_ref[...]], axis=0)
    qi = lax.broadcasted_iota(jnp.int32, (CHUNK, 2 * CHUNK), 0)
    kj = lax.broadcasted_iota(jnp.int32, (CHUNK, 2 * CHUNK), 1)
    dist = qi + CHUNK - kj
    valid = (dist >= 0) & (dist < CHUNK) & (n * CHUNK - CHUNK + kj >= 0)
    distf = dist.astype(F32)
    for g in range(grp):
        cols = slice(g * HEAD_DIM, (g + 1) * HEAD_DIM)
        q = q_ref[:, cols].astype(F32)
        q = q * lax.rsqrt(jnp.mean(q * q, axis=-1, keepdims=True) + EPS) * qg_ref[...]
        s = lax.dot_general(q.astype(BF16), kb, (((1,), (1,)), ((), ())),
                            preferred_element_type=F32) * (HEAD_DIM ** -0.5)
        slope = slopes_ref[h * grp + g]
        sink = sinks_ref[h * grp + g]
        s = jnp.where(valid, s - slope * distf, -jnp.inf)
        m = jnp.maximum(jnp.max(s, axis=-1, keepdims=True), sink)
        p = jnp.exp(s - m)
        denom = jnp.sum(p, axis=-1, keepdims=True) + jnp.exp(sink - m)
        o = jnp.dot(p.astype(BF16), vb, preferred_element_type=F32) / denom
        o_ref[:, cols] = o.astype(o_ref.dtype)


def _swa(proj, q_gain, k_gain, sinks, n_heads, n_kv, col0):
    t = proj.shape[0]
    grp = n_heads // n_kv
    qb0 = col0 // (grp * HEAD_DIM)
    kb0 = (col0 + n_heads * HEAD_DIM) // HEAD_DIM
    vb0 = kb0 + n_kv
    slopes = 2.0 ** (-8.0 * jnp.arange(1, n_heads + 1, dtype=F32) / n_heads)
    prev = lambda n: jnp.maximum(n - 1, 0)
    return pl.pallas_call(
        functools.partial(_swa_kernel, grp=grp),
        grid_spec=pltpu.PrefetchScalarGridSpec(
            num_scalar_prefetch=2,
            grid=(t // CHUNK, n_kv),
            in_specs=[pl.BlockSpec((CHUNK, grp * HEAD_DIM), lambda n, h, *_: (n, qb0 + h)),
                      pl.BlockSpec((CHUNK, HEAD_DIM), lambda n, h, *_: (n, kb0 + h)),
                      pl.BlockSpec((CHUNK, HEAD_DIM), lambda n, h, *_: (prev(n), kb0 + h)),
                      pl.BlockSpec((CHUNK, HEAD_DIM), lambda n, h, *_: (n, vb0 + h)),
                      pl.BlockSpec((CHUNK, HEAD_DIM), lambda n, h, *_: (prev(n), vb0 + h)),
                      pl.BlockSpec((1, HEAD_DIM), lambda n, h, *_: (0, 0)),
                      pl.BlockSpec((1, HEAD_DIM), lambda n, h, *_: (0, 0))],
            out_specs=pl.BlockSpec((CHUNK, grp * HEAD_DIM), lambda n, h, *_: (n, h)),
        ),
        out_shape=jax.ShapeDtypeStruct((t, n_heads * HEAD_DIM), BF16),
        compiler_params=_params("parallel", "arbitrary"),
        name="swa",
    )(slopes, sinks.astype(F32), proj, proj, proj, proj, proj,
      q_gain.reshape(1, HEAD_DIM).astype(F32), k_gain.reshape(1, HEAD_DIM).astype(F32))


def _conformer_kernel(a_ref, g_ref, ap_ref, gp_ref, w_ref, cb_ref, lg_ref, lb_ref, o_ref, hbuf, ybuf):
    i = pl.program_id(0)
    tc, width = a_ref.shape
    hp = ap_ref[...].astype(F32) * jax.nn.sigmoid(gp_ref[...].astype(F32))
    hbuf[0:CONV_HALO, :] = jnp.where(i > 0, hp, 0.0)
    hbuf[CONV_HALO:CONV_HALO + tc, :] = a_ref[...].astype(F32) * jax.nn.sigmoid(g_ref[...].astype(F32))
    base = CONV_HALO - (CONV_WIDTH - 1)
    for c in range(width // HEAD_DIM):
        cols = slice(c * HEAD_DIM, (c + 1) * HEAD_DIM)
        acc = jnp.zeros((tc, HEAD_DIM), F32)
        for j in range(CONV_WIDTH):
            acc = acc + w_ref[j:j + 1, cols] * hbuf[base + j:base + j + tc, cols]
        ybuf[:, cols] = acc
    y = _layer_norm(ybuf[...] + cb_ref[...], lg_ref[...], lb_ref[...])
    o_ref[...] = (y * jax.nn.sigmoid(y)).astype(o_ref.dtype)


def _conformer(proj, conv_w, conv_b, ln_g, ln_b, tc=256):
    t = proj.shape[0]
    width = conv_w.shape[1]
    tc = min(tc, t)
    halo_blocks = tc // CONV_HALO
    prev = lambda i: jnp.maximum(i * halo_blocks - 1, 0)
    vec = lambda: pl.BlockSpec((1, width), lambda i: (0, 0))
    return pl.pallas_call(
        _conformer_kernel,
        grid=(t // tc,),
        in_specs=[pl.BlockSpec((tc, width), lambda i: (i, 0)),
                  pl.BlockSpec((tc, width), lambda i: (i, 1)),
                  pl.BlockSpec((CONV_HALO, width), lambda i: (prev(i), 0)),
                  pl.BlockSpec((CONV_HALO, width), lambda i: (prev(i), 1)),
                  pl.BlockSpec((CONV_WIDTH, width), lambda i: (0, 0)),
                  vec(), vec(), vec()],
        out_specs=pl.BlockSpec((tc, width), lambda i: (i, 0)),
        out_shape=jax.ShapeDtypeStruct((t, width), BF16),
        scratch_shapes=[pltpu.VMEM((CONV_HALO + tc, width), F32),
                        pltpu.VMEM((tc, width), F32)],
        compiler_params=_params("parallel"),
        name="conformer",
    )(proj, proj, proj, proj, conv_w.astype(F32), conv_b.reshape(1, width).astype(F32),
      ln_g.reshape(1, width).astype(F32), ln_b.reshape(1, width).astype(F32))


def _stickbreak_kernel(q_ref, k_ref, v_ref, o_ref):
    i = pl.program_id(1)
    tq = q_ref.shape[0]
    q = q_ref[...]
    row = lax.broadcasted_iota(jnp.int32, (tq, tq), 0)
    col = lax.broadcasted_iota(jnp.int32, (tq, tq), 1)
    later_keys = (row > col).astype(BF16)
    strictly_causal = col < row

    def key_block(j, c, acc, diagonal):
        start = pl.multiple_of(j * tq, tq)
        kb = k_ref[pl.ds(start, tq), :]
        vb = v_ref[pl.ds(start, tq), :]
        z = lax.dot_general(q, kb, (((1,), (1,)), ((), ())),
                            preferred_element_type=F32) * (HEAD_DIM ** -0.5)
        softplus = jnp.maximum(z, 0.0) + jnp.log1p(jnp.exp(-jnp.abs(z)))
        log_fail = -softplus
        if diagonal:
            log_fail = jnp.where(strictly_causal, log_fail, 0.0)
        hi = log_fail.astype(BF16)
        lo = (log_fail - hi.astype(F32)).astype(BF16)
        later = (jnp.dot(hi, later_keys, preferred_element_type=F32)
                 + jnp.dot(lo, later_keys, preferred_element_type=F32) + c)
        a = jnp.exp(z - softplus + later)
        if diagonal:
            a = jnp.where(strictly_causal, a, 0.0)
        acc = acc + jnp.dot(a.astype(BF16), vb, preferred_element_type=F32)
        c = c + jnp.sum(log_fail, axis=-1, keepdims=True)
        return c, acc

    c, acc = key_block(i, jnp.zeros((tq, 1), F32), jnp.zeros((tq, HEAD_DIM), F32), True)

    def cond(state):
        j, c, _ = state
        return jnp.logical_and(j >= 0, jnp.max(c) > SB_LOG_ZERO)

    def body(state):
        j, c, acc = state
        c, acc = key_block(j, c, acc, False)
        return j - 1, c, acc

    _, _, acc = lax.while_loop(cond, body, (i - 1, c, acc))
    o_ref[...] = acc.astype(o_ref.dtype)


def _stickbreak(proj, n_heads, col0):
    t = proj.shape[0]
    qb0 = col0 // HEAD_DIM
    kb0 = qb0 + n_heads
    vb0 = kb0 + n_heads
    return pl.pallas_call(
        _stickbreak_kernel,
        grid=(n_heads, t // CHUNK),
        in_specs=[pl.BlockSpec((CHUNK, HEAD_DIM), lambda h, i: (i, qb0 + h)),
                  pl.BlockSpec((t, HEAD_DIM), lambda h, i: (0, kb0 + h)),
                  pl.BlockSpec((t, HEAD_DIM), lambda h, i: (0, vb0 + h))],
        out_specs=pl.BlockSpec((CHUNK, HEAD_DIM), lambda h, i: (i, h)),
        out_shape=jax.ShapeDtypeStruct((t, n_heads * HEAD_DIM), BF16),
        compiler_params=_params("parallel", "arbitrary"),
        name="stickbreak",
    )(proj, proj, proj)


def _router_kernel(x_ref, g_ref, whi_ref, wlo_ref, b_ref, hp_ref, route_ref, counts_ref, carry):
    @pl.when(pl.program_id(0) == 0)
    def _():
        carry[...] = jnp.zeros_like(carry)

    x = x_ref[...]
    tm, d = x.shape
    h = x * lax.rsqrt(jnp.mean(x * x, axis=-1, keepdims=True) + EPS) * g_ref[...]
    h_hi = h.astype(BF16)
    h_hif = h_hi.astype(F32)
    h_lo = (h - h_hif).astype(BF16)
    logits = (jnp.dot(h_hi, whi_ref[...], preferred_element_type=F32)
              + jnp.dot(h_lo, whi_ref[...], preferred_element_type=F32)
              + jnp.dot(h_hi, wlo_ref[...], preferred_element_type=F32)) + b_ref[...]
    lane = lax.broadcasted_iota(jnp.int32, logits.shape, 1)
    neg = -jnp.inf

    def first_max(vals):
        m = jnp.max(vals, axis=-1, keepdims=True)
        idx = jnp.min(jnp.where(vals == m, lane, ROUTE_LANES), axis=-1, keepdims=True)
        return m, idx

    gl = jnp.where(lane < N_GROUPS, logits, neg)
    gmax, g_sel = first_max(gl)
    g_weight = 1.0 / jnp.sum(jnp.exp(gl - gmax), axis=-1, keepdims=True)
    expert = lane - N_GROUPS
    in_group = (expert >= 0) & (expert < N_EXPERTS) & ((expert >> 3) == g_sel)
    el = jnp.where(in_group, logits, neg)
    m1, i1 = first_max(el)
    m2, i2 = first_max(jnp.where(lane == i1, neg, el))
    e21 = jnp.exp(m2 - m1)
    w1 = g_weight / (1.0 + e21)
    w2 = w1 * e21
    sel1 = lane == i1
    sel2 = lane == i2
    chosen = jnp.where(jnp.logical_or(sel1, sel2), 1.0, 0.0)
    row = lax.broadcasted_iota(jnp.int32, (tm, tm), 0)
    col = lax.broadcasted_iota(jnp.int32, (tm, tm), 1)
    earlier = jnp.where(col < row, 1.0, 0.0).astype(BF16)
    before = jnp.dot(earlier, chosen.astype(BF16), preferred_element_type=F32) + carry[...]
    rank1 = jnp.sum(jnp.where(sel1, before, 0.0), axis=-1, keepdims=True)
    rank2 = jnp.sum(jnp.where(sel2, before, 0.0), axis=-1, keepdims=True)
    carry[...] = carry[...] + jnp.sum(chosen, axis=0, keepdims=True)
    counts_ref[...] = carry[...]
    fields = ((i1 - N_GROUPS).astype(F32), (i2 - N_GROUPS).astype(F32), w1, w2, rank1, rank2)
    rec = jnp.zeros_like(logits)
    for k, val in enumerate(fields):
        rec = jnp.where(lane == k, val, rec)
    route_ref[...] = rec
    bits = lax.bitcast_convert_type(h_hif, jnp.uint32)
    hp_ref[...] = (bits[:, :d // 2] >> 16) | (bits[:, d // 2:] & jnp.uint32(0xFFFF0000))


def _router(x, norm_g, wr_g, br_g, wr_e, br_e, tm=256):
    t, d = x.shape
    w = jnp.zeros((d, ROUTE_LANES), F32).at[:, :N_GROUPS].set(wr_g).at[:, N_GROUPS:N_GROUPS + N_EXPERTS].set(wr_e)
    b = jnp.zeros((1, ROUTE_LANES), F32).at[0, :N_GROUPS].set(br_g).at[0, N_GROUPS:N_GROUPS + N_EXPERTS].set(br_e)
    w_hi = w.astype(BF16)
    w_lo = (w - w_hi.astype(F32)).astype(BF16)
    tm = min(tm, t)
    return pl.pallas_call(
        _router_kernel,
        grid=(t // tm,),
        in_specs=[pl.BlockSpec((tm, d), lambda i: (i, 0)),
                  pl.BlockSpec((1, d), lambda i: (0, 0)),
                  pl.BlockSpec((d, ROUTE_LANES), lambda i: (0, 0)),
                  pl.BlockSpec((d, ROUTE_LANES), lambda i: (0, 0)),
                  pl.BlockSpec((1, ROUTE_LANES), lambda i: (0, 0))],
        out_specs=[pl.BlockSpec((tm, d // 2), lambda i: (i, 0)),
                   pl.BlockSpec((tm, ROUTE_LANES), lambda i: (i, 0)),
                   pl.BlockSpec((1, ROUTE_LANES), lambda i: (0, 0))],
        out_shape=[jax.ShapeDtypeStruct((t, d // 2), jnp.uint32),
                   jax.ShapeDtypeStruct((t, ROUTE_LANES), F32),
                   jax.ShapeDtypeStruct((1, ROUTE_LANES), F32)],
        scratch_shapes=[pltpu.VMEM((1, ROUTE_LANES), F32)],
        compiler_params=_params("arbitrary"),
        name="router",
    )(x, norm_g.reshape(1, d), w_hi, w_lo, b)


def _lookup(table, idx):
    hit = idx[..., None] == jnp.arange(table.shape[0], dtype=jnp.int32)
    return jnp.sum(jnp.where(hit, table, 0), axis=-1)


def _dispatch_tables(route, counts, t):
    counts = counts[0, N_GROUPS:N_GROUPS + N_EXPERTS].astype(jnp.int32)
    padded = (counts + EXPERT_BLOCK - 1) // EXPERT_BLOCK * EXPERT_BLOCK
    padded_end = jnp.cumsum(padded)
    padded_start = padded_end - padded
    expert = route[:, 0:2].astype(jnp.int32)
    rank = route[:, 4:6].astype(jnp.int32)
    assign_slot = (_lookup(padded_start, expert) + rank).reshape(-1)

    n_slots = (-(-2 * t // EXPERT_BLOCK) + N_EXPERTS) * EXPERT_BLOCK
    n_items = N_EXPERTS + n_slots // MOE_MAX_ROWS
    chunks = (padded + MOE_MAX_ROWS - 1) // MOE_MAX_ROWS
    chunk_end = jnp.cumsum(chunks)
    n_used = chunk_end[-1]
    ids = jnp.arange(n_items, dtype=jnp.int32)
    item_e = jnp.minimum(jnp.sum(chunk_end[None, :] <= ids[:, None], axis=1), N_EXPERTS - 1).astype(jnp.int32)
    j = ids - _lookup(chunk_end - chunks, item_e)
    item_start = _lookup(padded_start, item_e) + j * MOE_MAX_ROWS
    item_rows = jnp.clip(_lookup(padded, item_e) - j * MOE_MAX_ROWS, 0, MOE_MAX_ROWS)
    used = ids < n_used
    last_e = jnp.sum(jnp.where(ids == n_used - 1, item_e, 0))
    item_e = jnp.where(used, item_e, last_e).astype(jnp.int32)
    item_start = jnp.where(used, item_start, 0).astype(jnp.int32)
    item_rows = jnp.where(used, item_rows, 0).astype(jnp.int32)
    total = padded_end[-1].reshape(1).astype(jnp.int32)
    return (assign_slot.astype(jnp.int32), padded_start.astype(jnp.int32), padded.astype(jnp.int32), total,
            item_e, item_start, item_rows, n_slots, n_items)


def _dispatch_kernel(as_ref, ps_ref, pd_ref, tot_ref, hp_ref, xs_hbm, zbuf, sem, zsem):
    i = pl.program_id(0)
    tm = hp_ref.shape[0]
    n_blocks = xs_hbm.shape[0] // EXPERT_BLOCK

    def zero_block(first_row):
        return pltpu.make_async_copy(zbuf, xs_hbm.at[pl.ds(pl.multiple_of(first_row, EXPERT_BLOCK), EXPERT_BLOCK)],
                                     zsem)

    def zero_fill(act):
        def per_expert(e, carry):
            @pl.when(pd_ref[e] > 0)
            def _():
                act(zero_block(ps_ref[e] + pd_ref[e] - EXPERT_BLOCK))
            return carry
        lax.fori_loop(0, N_EXPERTS, per_expert, 0)

        def per_tail_block(b, carry):
            act(zero_block(b * EXPERT_BLOCK))
            return carry
        lax.fori_loop(tot_ref[0] // EXPERT_BLOCK, n_blocks, per_tail_block, 0)

    @pl.when(i == 0)
    def _():
        zbuf[...] = jnp.zeros_like(zbuf)
        zero_fill(lambda cp: cp.start())
        zero_fill(lambda cp: cp.wait())

    def body(r, carry):
        a = 2 * (i * tm + r)
        for k in range(2):
            pltpu.make_async_copy(hp_ref.at[pl.ds(r, 1)], xs_hbm.at[pl.ds(as_ref[a + k], 1)], sem).start()
        return carry
    lax.fori_loop(0, tm, body, 0, unroll=8)
    for k in range(2):
        pltpu.make_async_copy(hp_ref, xs_hbm.at[pl.ds(0, tm)], sem).wait()


def _dispatch(hp, assign_slot, padded_start, padded, total, n_slots, tm=256):
    t, words = hp.shape
    tm = min(tm, t)
    return pl.pallas_call(
        _dispatch_kernel,
        grid_spec=pltpu.PrefetchScalarGridSpec(
            num_scalar_prefetch=4,
            grid=(t // tm,),
            in_specs=[pl.BlockSpec((tm, words), lambda i, *_: (i, 0))],
            out_specs=pl.BlockSpec(memory_space=pl.ANY),
            scratch_shapes=[pltpu.VMEM((EXPERT_BLOCK, words), hp.dtype),
                            pltpu.SemaphoreType.DMA(()),
                            pltpu.SemaphoreType.DMA(())],
        ),
        out_shape=jax.ShapeDtypeStruct((n_slots, words), hp.dtype),
        compiler_params=_params("arbitrary"),
        name="moe_dispatch",
    )(assign_slot, padded_start, padded, total, hp)


def _unpack_pairs(words):
    lo = lax.bitcast_convert_type(words << 16, F32)
    hi = lax.bitcast_convert_type(words & jnp.uint32(0xFFFF0000), F32)
    return lo, hi


def _pack_pairs(lo, hi):
    lo_bits = lax.bitcast_convert_type(lo.astype(BF16).astype(F32), jnp.uint32)
    hi_bits = lax.bitcast_convert_type(hi.astype(BF16).astype(F32), jnp.uint32)
    return (lo_bits >> 16) | (hi_bits & jnp.uint32(0xFFFF0000))


def _expert_items_kernel(ie_ref, is_ref, ir_ref, tot_ref, in_hbm, *refs, n_weights, cast_weights, compute):
    w_refs = refs[:n_weights]
    out_hbm = refs[n_weights]
    inbuf, outbuf, zbuf, w_bf, sem_in, sem_out, sem_z = refs[n_weights + 1:]
    i = pl.program_id(0)
    n_items = pl.num_programs(0)
    max_blocks = inbuf.shape[1] // EXPERT_BLOCK
    n_out_blocks = out_hbm.shape[0] // EXPERT_BLOCK

    def rows_of(hbm, first_row):
        return hbm.at[pl.ds(pl.multiple_of(first_row, EXPERT_BLOCK), EXPERT_BLOCK)]

    def for_blocks(item, act):
        for c in range(max_blocks):
            @pl.when(c * EXPERT_BLOCK < ir_ref[item])
            def _():
                act(c, is_ref[item] + c * EXPERT_BLOCK)

    def in_copy(buf, act):
        return lambda c, row: act(pltpu.make_async_copy(
            rows_of(in_hbm, row), inbuf.at[buf, pl.ds(c * EXPERT_BLOCK, EXPERT_BLOCK)], sem_in.at[buf]))

    def out_copy(buf, act):
        return lambda c, row: act(pltpu.make_async_copy(
            outbuf.at[buf, pl.ds(c * EXPERT_BLOCK, EXPERT_BLOCK)], rows_of(out_hbm, row), sem_out.at[buf]))

    start = lambda cp: cp.start()
    wait = lambda cp: cp.wait()
    buf = i % 2

    @pl.when(i == 0)
    def _():
        for_blocks(0, in_copy(0, start))
        zbuf[...] = jnp.zeros_like(zbuf)

        def tail(act):
            def body(b, carry):
                act(pltpu.make_async_copy(zbuf, rows_of(out_hbm, b * EXPERT_BLOCK), sem_z))
                return carry
            lax.fori_loop(tot_ref[0] // EXPERT_BLOCK, n_out_blocks, body, 0)
        tail(start)
        tail(wait)

    @pl.when(i + 1 < n_items)
    def _():
        for_blocks(i + 1, in_copy(1 - buf, start))

    for_blocks(i, in_copy(buf, wait))

    @pl.when(jnp.logical_or(i == 0, ie_ref[i] != ie_ref[jnp.maximum(i - 1, 0)]))
    def _():
        cast_weights(w_refs, w_bf)

    @pl.when(i >= 2)
    def _():
        for_blocks(i - 2, out_copy(buf, wait))

    n_blk = ir_ref[i] // EXPERT_BLOCK
    for c in range(1, max_blocks + 1):
        @pl.when(n_blk == c)
        def _():
            rows = c * EXPERT_BLOCK
            outbuf[buf, 0:rows, :] = compute(inbuf[buf, 0:rows, :], w_bf)

    for_blocks(i, out_copy(buf, start))

    @pl.when(i == n_items - 1)
    def _():
        @pl.when(i >= 1)
        def _():
            for_blocks(i - 1, out_copy(1 - buf, wait))
        for_blocks(i, out_copy(buf, wait))


def _expert_items_call(name, items, in_rows, weights, out_width, out_dtype, n_w_cols, cast_weights, compute):
    item_e, item_start, item_rows, total = items
    n_slots, in_width = in_rows.shape
    k_dim = weights[0].shape[1]
    w_spec = lambda w: pl.BlockSpec((None,) + w.shape[1:], lambda i, ie, *_: (ie[i], 0, 0))
    return pl.pallas_call(
        functools.partial(_expert_items_kernel, n_weights=len(weights), cast_weights=cast_weights, compute=compute),
        grid_spec=pltpu.PrefetchScalarGridSpec(
            num_scalar_prefetch=4,
            grid=(item_e.shape[0],),
            in_specs=[pl.BlockSpec(memory_space=pl.ANY)] + [w_spec(w) for w in weights],
            out_specs=pl.BlockSpec(memory_space=pl.ANY),
            scratch_shapes=[pltpu.VMEM((2, MOE_MAX_ROWS, in_width), in_rows.dtype),
                            pltpu.VMEM((2, MOE_MAX_ROWS, out_width), out_dtype),
                            pltpu.VMEM((EXPERT_BLOCK, out_width), out_dtype),
                            pltpu.VMEM((k_dim, n_w_cols), BF16),
                            pltpu.SemaphoreType.DMA((2,)),
                            pltpu.SemaphoreType.DMA((2,)),
                            pltpu.SemaphoreType.DMA(())],
        ),
        out_shape=jax.ShapeDtypeStruct((n_slots, out_width), out_dtype),
        compiler_params=pltpu.CompilerParams(dimension_semantics=("arbitrary",),
                                             vmem_limit_bytes=V7X_VMEM_LIMIT_MOE),
        name=name,
    )(item_e, item_start, item_rows, total, in_rows, *weights)


def _cast_gate_up(w_refs, w_bf):
    d_exp = w_refs[0].shape[1]
    w_bf[:, 0:d_exp] = w_refs[0][...].astype(BF16)
    w_bf[:, d_exp:2 * d_exp] = w_refs[1][...].astype(BF16)


def _gate_up(words, w_bf):
    half = words.shape[1]
    d_exp = w_bf.shape[1] // 2
    lo, hi = _unpack_pairs(words)
    gu = (jnp.dot(lo.astype(BF16), w_bf[0:half, :], preferred_element_type=F32)
          + jnp.dot(hi.astype(BF16), w_bf[half:2 * half, :], preferred_element_type=F32))
    gate, up = gu[:, 0:d_exp], gu[:, d_exp:2 * d_exp]
    return (gate * jax.nn.sigmoid(gate) * up).astype(BF16)


def _cast_down(w_refs, w_bf):
    w_bf[...] = w_refs[0][...].astype(BF16)


def _down(hid, w_bf):
    half = w_bf.shape[1] // 2
    y = jnp.dot(hid, w_bf[...], preferred_element_type=F32)
    return _pack_pairs(y[:, 0:half], y[:, half:2 * half])


def _combine_kernel(as_ref, x_ref, route_ref, y_hbm, o_ref, ybuf, sem):
    i = pl.program_id(0)
    tm, d = x_ref.shape
    half = d // 2

    def start_gather(tile, buf):
        def body(r, carry):
            a = 2 * (tile * tm + r)
            for k in range(2):
                pltpu.make_async_copy(y_hbm.at[pl.ds(as_ref[a + k], 1)], ybuf.at[buf, k, pl.ds(r, 1)],
                                      sem.at[buf]).start()
            return carry
        lax.fori_loop(0, tm, body, 0, unroll=8)

    @pl.when(i == 0)
    def _():
        start_gather(0, 0)

    @pl.when(i + 1 < pl.num_programs(0))
    def _():
        start_gather(i + 1, (i + 1) % 2)

    buf = i % 2
    for k in range(2):
        pltpu.make_async_copy(y_hbm.at[pl.ds(0, tm)], ybuf.at[buf, k], sem.at[buf]).wait()
    w1 = route_ref[:, 2:3]
    w2 = route_ref[:, 3:4]
    lo1, hi1 = _unpack_pairs(ybuf[buf, 0])
    lo2, hi2 = _unpack_pairs(ybuf[buf, 1])
    o_ref[:, 0:half] = x_ref[:, 0:half] + w1 * lo1 + w2 * lo2
    o_ref[:, half:d] = x_ref[:, half:d] + w1 * hi1 + w2 * hi2


def _moe(x, norm_g, wr_g, br_g, wr_e, br_e, w_gate, w_up, w_down, tm_combine=256):
    t, d = x.shape
    d_exp = w_gate.shape[2]
    hp, route, counts = _router(x, norm_g, wr_g, br_g, wr_e, br_e)
    (assign_slot, padded_start, padded, total, item_e, item_start, item_rows,
     n_slots, _) = _dispatch_tables(route, counts, t)
    items = (item_e, item_start, item_rows, total)
    xs = _dispatch(hp, assign_slot, padded_start, padded, total, n_slots)
    hid = _expert_items_call("moe_up", items, xs, (w_gate, w_up), d_exp, BF16, 2 * d_exp,
                             _cast_gate_up, _gate_up)
    y = _expert_items_call("moe_down", items, hid, (w_down,), d // 2, jnp.uint32, d, _cast_down, _down)

    tm = min(tm_combine, t)
    return pl.pallas_call(
        _combine_kernel,
        grid_spec=pltpu.PrefetchScalarGridSpec(
            num_scalar_prefetch=1,
            grid=(t // tm,),
            in_specs=[pl.BlockSpec((tm, d), lambda i, a: (i, 0)),
                      pl.BlockSpec((tm, ROUTE_LANES), lambda i, a: (i, 0)),
                      pl.BlockSpec(memory_space=pl.ANY)],
            out_specs=pl.BlockSpec((tm, d), lambda i, a: (i, 0)),
            scratch_shapes=[pltpu.VMEM((2, 2, tm, d // 2), jnp.uint32),
                            pltpu.SemaphoreType.DMA((2,))],
        ),
        out_shape=jax.ShapeDtypeStruct((t, d), F32),
        compiler_params=_params("arbitrary"),
        name="moe_combine",
    )(assign_slot, x, route, y)


def _even_layer_mixer(x, mix_norm, w_in, a_ln_g, a_ln_b, a_ws, a_bs, b_q_norm, b_k_norm, b_sinks, w_out):
    n_groups = a_ws.shape[0]
    a_width = n_groups * HEAD_DIM
    n_heads = b_sinks.shape[0]
    n_kv = (w_in.shape[1] - 2 * a_width - n_heads * HEAD_DIM) // (2 * HEAD_DIM)
    h = _rmsnorm(x, mix_norm)
    proj = _matmul(h, w_in.astype(BF16), tm=1024, tn=1024)
    y_a = _gmlp(proj, a_ln_g, a_ln_b, a_ws, a_bs)
    y_b = _swa(proj, b_q_norm, b_k_norm, b_sinks, n_heads, n_kv, col0=2 * a_width)
    return _out_proj(x, y_a, y_b, w_out.astype(BF16))


def _odd_layer_mixer(x, mix_norm, w_in, c_conv_w, c_conv_b, c_ln_g, c_ln_b, w_out):
    c_width = c_conv_w.shape[1]
    n_heads = (w_in.shape[1] - 2 * c_width) // (3 * HEAD_DIM)
    h = _rmsnorm(x, mix_norm)
    proj = _matmul(h, w_in.astype(BF16), tm=1024, tn=1024)
    y_c = _conformer(proj, c_conv_w, c_conv_b, c_ln_g, c_ln_b)
    y_d = _stickbreak(proj, n_heads, col0=2 * c_width)
    return _out_proj(x, y_c, y_d, w_out.astype(BF16))


def kernel(x, l0_mix_norm, l0_w_in, l0_a_ln_g, l0_a_ln_b, l0_a_ws, l0_a_bs, l0_b_q_norm, l0_b_k_norm, l0_b_sinks, l0_w_out, l0_ffn_norm, l0_router_group_w, l0_router_group_b, l0_router_expert_w, l0_router_expert_b, l0_w_gate, l0_w_up, l0_w_down, l1_mix_norm, l1_w_in, l1_c_conv_w, l1_c_conv_b, l1_c_ln_g, l1_c_ln_b, l1_w_out, l1_ffn_norm, l1_router_group_w, l1_router_group_b, l1_router_expert_w, l1_router_expert_b, l1_w_gate, l1_w_up, l1_w_down):
    bsz, seq, d = x.shape
    xt = x.reshape(bsz * seq, d)
    assert bsz == 1, "the mixers index positions by row; one sequence per call"
    xt = _even_layer_mixer(xt, l0_mix_norm, l0_w_in, l0_a_ln_g, l0_a_ln_b, l0_a_ws, l0_a_bs,
                           l0_b_q_norm, l0_b_k_norm, l0_b_sinks, l0_w_out)
    xt = _moe(xt, l0_ffn_norm, l0_router_group_w, l0_router_group_b, l0_router_expert_w,
              l0_router_expert_b, l0_w_gate, l0_w_up, l0_w_down)
    xt = _odd_layer_mixer(xt, l1_mix_norm, l1_w_in, l1_c_conv_w, l1_c_conv_b, l1_c_ln_g, l1_c_ln_b, l1_w_out)
    xt = _moe(xt, l1_ffn_norm, l1_router_group_w, l1_router_group_b, l1_router_expert_w,
              l1_router_expert_b, l1_w_gate, l1_w_up, l1_w_down)
    return xt.reshape(bsz, seq, d)
```

```python
import functools

import jax
import jax.numpy as jnp
from jax import lax
from jax.experimental import pallas as pl
from jax.experimental.pallas import tpu as pltpu

F32 = jnp.float32
BF16 = jnp.bfloat16

HEAD_DIM = 128
CHUNK = 128
EPS = 1e-6
N_GROUPS = 8
EXPERTS_PER_GROUP = 8
N_EXPERTS = N_GROUPS * EXPERTS_PER_GROUP
EXPERT_BLOCK = 128
CONV_WIDTH = 31
SUBLANES = 8
CONV_ROWS = 128
CONV_HALO = 32
ROUTE_LANES = 128
SB_LOG_ZERO = -104.0
SB_HEADS = 4
SB_KEYS = 256
MOE_MAX_ROWS = 768
V7X_VMEM_LIMIT = 56 * 1024 * 1024
V7X_VMEM_LIMIT_MOE = 60 * 1024 * 1024


def _params(*sem):
    return pltpu.CompilerParams(dimension_semantics=sem, vmem_limit_bytes=V7X_VMEM_LIMIT)


def _rmsnorm_kernel(x_ref, g_ref, o_ref):
    x = x_ref[...]
    ms = jnp.mean(x * x, axis=-1, keepdims=True)
    o_ref[...] = (x * lax.rsqrt(ms + EPS) * g_ref[...]).astype(o_ref.dtype)


def _rmsnorm(x, g, tm=256):
    t, d = x.shape
    return pl.pallas_call(
        _rmsnorm_kernel,
        grid=(t // tm,),
        in_specs=[pl.BlockSpec((tm, d), lambda i: (i, 0)),
                  pl.BlockSpec((1, d), lambda i: (0, 0))],
        out_specs=pl.BlockSpec((tm, d), lambda i: (i, 0)),
        out_shape=jax.ShapeDtypeStruct((t, d), BF16),
        compiler_params=_params("parallel"),
        name="rmsnorm",
    )(x, g.reshape(1, d))


def _matmul_kernel(a_ref, w_ref, o_ref):
    o_ref[...] = jnp.dot(a_ref[...], w_ref[...], preferred_element_type=F32).astype(o_ref.dtype)


def _matmul(a, w, tm, tn):
    m, k = a.shape
    n = w.shape[1]
    tm = min(tm, m)
    return pl.pallas_call(
        _matmul_kernel,
        grid=(m // tm, n // tn),
        in_specs=[pl.BlockSpec((tm, k), lambda i, j: (i, 0)),
                  pl.BlockSpec((k, tn), lambda i, j: (0, j))],
        out_specs=pl.BlockSpec((tm, tn), lambda i, j: (i, j)),
        out_shape=jax.ShapeDtypeStruct((m, n), BF16),
        compiler_params=_params("parallel", "arbitrary"),
        name="in_proj",
    )(a, w)


def _out_proj_kernel(x_ref, a_ref, b_ref, wa_ref, wb_ref, o_ref):
    acc = jnp.dot(a_ref[...], wa_ref[...], preferred_element_type=F32)
    acc = acc + jnp.dot(b_ref[...], wb_ref[...], preferred_element_type=F32)
    o_ref[...] = x_ref[...] + acc


def _out_proj(x, ya, yb, w, tm=1024, tn=512):
    t, d = x.shape
    ka, kb = ya.shape[1], yb.shape[1]
    assert ka == kb and w.shape == (ka + kb, d)
    tm = min(tm, t)
    return pl.pallas_call(
        _out_proj_kernel,
        grid=(t // tm, d // tn),
        in_specs=[pl.BlockSpec((tm, tn), lambda i, j: (i, j)),
                  pl.BlockSpec((tm, ka), lambda i, j: (i, 0)),
                  pl.BlockSpec((tm, kb), lambda i, j: (i, 0)),
                  pl.BlockSpec((ka, tn), lambda i, j: (0, j)),
                  pl.BlockSpec((kb, tn), lambda i, j: (1, j))],
        out_specs=pl.BlockSpec((tm, tn), lambda i, j: (i, j)),
        out_shape=jax.ShapeDtypeStruct((t, d), F32),
        compiler_params=_params("parallel", "arbitrary"),
        name="out_proj",
    )(x, ya, yb, w, w)


def _layer_norm(x, g, b):
    mu = jnp.mean(x, axis=-1, keepdims=True)
    xc = x - mu
    var = jnp.mean(xc * xc, axis=-1, keepdims=True)
    return xc * lax.rsqrt(var + EPS) * g + b


def _gelu(x):
    return 0.5 * x * (1.0 + lax.erf(x * (2.0 ** -0.5)))


def _gmlp_kernel(u_ref, v_ref, lng_ref, lnb_ref, w_ref, bs_ref, o_ref):
    n_groups = w_ref.shape[0]
    v = _gelu(v_ref[...].astype(F32))
    vn = _layer_norm(v, lng_ref[...], lnb_ref[...]).astype(BF16)
    group_cols = [slice(g * HEAD_DIM, (g + 1) * HEAD_DIM) for g in range(n_groups)]
    mixed = [jnp.dot(w_ref[g], vn[:, cols], preferred_element_type=F32) for g, cols in enumerate(group_cols)]
    for g, cols in enumerate(group_cols):
        u = _gelu(u_ref[:, cols].astype(F32))
        o_ref[:, cols] = (u * (mixed[g] + bs_ref[:, cols])).astype(o_ref.dtype)


def _gmlp(proj, ln_g, ln_b, w_s, b_s):
    t = proj.shape[0]
    n_groups = w_s.shape[0]
    width = n_groups * HEAD_DIM
    causal = jnp.tril(jnp.ones((CHUNK, CHUNK), dtype=bool))
    w = jnp.where(causal[None], w_s, 0.0).astype(BF16)
    bs = jnp.repeat(b_s.T.astype(F32), HEAD_DIM, axis=1)
    return pl.pallas_call(
        _gmlp_kernel,
        grid=(t // CHUNK,),
        in_specs=[pl.BlockSpec((CHUNK, width), lambda i: (i, 0)),
                  pl.BlockSpec((CHUNK, width), lambda i: (i, 1)),
                  pl.BlockSpec((1, width), lambda i: (0, 0)),
                  pl.BlockSpec((1, width), lambda i: (0, 0)),
                  pl.BlockSpec((n_groups, CHUNK, CHUNK), lambda i: (0, 0, 0)),
                  pl.BlockSpec((CHUNK, width), lambda i: (0, 0))],
        out_specs=pl.BlockSpec((CHUNK, width), lambda i: (i, 0)),
        out_shape=jax.ShapeDtypeStruct((t, width), BF16),
        compiler_params=_params("parallel"),
        name="gmlp",
    )(proj, proj, ln_g.reshape(1, width), ln_b.reshape(1, width), w, bs)


def _swa_kernel(slopes_ref, sinks_ref, q_ref, kc_ref, kp_ref, vc_ref, vp_ref, qg_ref, kg_ref, o_ref, *, grp):
    n = pl.program_id(0)
    h = pl.program_id(1)
    k = jnp.concatenate([kp_ref[...], kc_ref[...]], axis=0).astype(F32)
    k = k * lax.rsqrt(jnp.mean(k * k, axis=-1, keepdims=True) + EPS) * kg_ref[...]
    kb = k.astype(BF16)
    vb = jnp.concatenate([vp_ref[...], vc_ref[...]], axis=0)
    qi = lax.broadcasted_iota(jnp.int32, (CHUNK, 2 * CHUNK), 0)
    kj = lax.broadcasted_iota(jnp.int32, (CHUNK, 2 * CHUNK), 1)
    dist = qi + CHUNK - kj
    valid = (dist >= 0) & (dist < CHUNK) & (n * CHUNK - CHUNK + kj >= 0)
    distf = dist.astype(F32)
    head_cols = [slice(g * HEAD_DIM, (g + 1) * HEAD_DIM) for g in range(grp)]
    scores = []
    for cols in head_cols:
        q = q_ref[:, cols].astype(F32)
        q = q * lax.rsqrt(jnp.mean(q * q, axis=-1, keepdims=True) + EPS) * qg_ref[...]
        scores.append(lax.dot_general(q.astype(BF16), kb, (((1,), (1,)), ((), ())),
                                      preferred_element_type=F32) * (HEAD_DIM ** -0.5))
    probs, denoms = [], []
    for g in range(grp):
        slope = slopes_ref[h * grp + g]
        sink = sinks_ref[h * grp + g]
        s = jnp.where(valid, scores[g] - slope * distf, -jnp.inf)
        m = jnp.maximum(jnp.max(s, axis=-1, keepdims=True), sink)
        p = jnp.exp(s - m)
        denoms.append(jnp.sum(p, axis=-1, keepdims=True) + jnp.exp(sink - m))
        probs.append(p.astype(BF16))
    for g in range(grp):
        o = jnp.dot(probs[g], vb, preferred_element_type=F32) / denoms[g]
        o_ref[:, head_cols[g]] = o.astype(o_ref.dtype)


def _swa(proj, q_gain, k_gain, sinks, n_heads, n_kv, col0):
    t = proj.shape[0]
    grp = n_heads // n_kv
    qb0 = col0 // (grp * HEAD_DIM)
    kb0 = (col0 + n_heads * HEAD_DIM) // HEAD_DIM
    vb0 = kb0 + n_kv
    slopes = 2.0 ** (-8.0 * jnp.arange(1, n_heads + 1, dtype=F32) / n_heads)
    prev = lambda n: jnp.maximum(n - 1, 0)
    return pl.pallas_call(
        functools.partial(_swa_kernel, grp=grp),
        grid_spec=pltpu.PrefetchScalarGridSpec(
            num_scalar_prefetch=2,
            grid=(t // CHUNK, n_kv),
            in_specs=[pl.BlockSpec((CHUNK, grp * HEAD_DIM), lambda n, h, *_: (n, qb0 + h)),
                      pl.BlockSpec((CHUNK, HEAD_DIM), lambda n, h, *_: (n, kb0 + h)),
                      pl.BlockSpec((CHUNK, HEAD_DIM), lambda n, h, *_: (prev(n), kb0 + h)),
                      pl.BlockSpec((CHUNK, HEAD_DIM), lambda n, h, *_: (n, vb0 + h)),
                      pl.BlockSpec((CHUNK, HEAD_DIM), lambda n, h, *_: (prev(n), vb0 + h)),
                      pl.BlockSpec((1, HEAD_DIM), lambda n, h, *_: (0, 0)),
                      pl.BlockSpec((1, HEAD_DIM), lambda n, h, *_: (0, 0))],
            out_specs=pl.BlockSpec((CHUNK, grp * HEAD_DIM), lambda n, h, *_: (n, h)),
        ),
        out_shape=jax.ShapeDtypeStruct((t, n_heads * HEAD_DIM), BF16),
        compiler_params=_params("parallel", "arbitrary"),
        name="swa",
    )(slopes, sinks.astype(F32), proj, proj, proj, proj, proj,
      q_gain.reshape(1, HEAD_DIM).astype(F32), k_gain.reshape(1, HEAD_DIM).astype(F32))


def _conformer_kernel(a_ref, g_ref, ap_ref, gp_ref, w_ref, cb_ref, lg_ref, lb_ref, o_ref, hbuf, ybuf):
    i = pl.program_id(0)
    tc, width = a_ref.shape
    span = CONV_HALO + tc
    hp = ap_ref[...].astype(F32) * jax.nn.sigmoid(gp_ref[...].astype(F32))
    hbuf[0, 0:CONV_HALO, :] = jnp.where(i > 0, hp, 0.0)
    hbuf[0, CONV_HALO:span, :] = a_ref[...].astype(F32) * jax.nn.sigmoid(g_ref[...].astype(F32))
    hbuf[0, span:span + SUBLANES, :] = jnp.zeros((SUBLANES, width), F32)
    for r in range(1, SUBLANES):
        hbuf[r, 0:span, :] = hbuf[0, r:r + span, :]
    base = CONV_HALO - (CONV_WIDTH - 1)
    for c in range(width // HEAD_DIM):
        cols = slice(c * HEAD_DIM, (c + 1) * HEAD_DIM)
        for t0 in range(0, tc, CONV_ROWS):
            acc = jnp.zeros((CONV_ROWS, HEAD_DIM), F32)
            for j in range(CONV_WIDTH):
                r = (base + j) % SUBLANES
                row0 = t0 + base + j - r
                acc = acc + w_ref[j:j + 1, cols] * hbuf[r, row0:row0 + CONV_ROWS, cols]
            ybuf[t0:t0 + CONV_ROWS, cols] = acc
    y = _layer_norm(ybuf[...] + cb_ref[...], lg_ref[...], lb_ref[...])
    o_ref[...] = (y * jax.nn.sigmoid(y)).astype(o_ref.dtype)


def _conformer(proj, conv_w, conv_b, ln_g, ln_b, tc=256):
    t = proj.shape[0]
    width = conv_w.shape[1]
    tc = min(tc, t)
    halo_blocks = tc // CONV_HALO
    prev = lambda i: jnp.maximum(i * halo_blocks - 1, 0)
    vec = lambda: pl.BlockSpec((1, width), lambda i: (0, 0))
    return pl.pallas_call(
        _conformer_kernel,
        grid=(t // tc,),
        in_specs=[pl.BlockSpec((tc, width), lambda i: (i, 0)),
                  pl.BlockSpec((tc, width), lambda i: (i, 1)),
                  pl.BlockSpec((CONV_HALO, width), lambda i: (prev(i), 0)),
                  pl.BlockSpec((CONV_HALO, width), lambda i: (prev(i), 1)),
                  pl.BlockSpec((CONV_WIDTH, width), lambda i: (0, 0)),
                  vec(), vec(), vec()],
        out_specs=pl.BlockSpec((tc, width), lambda i: (i, 0)),
        out_shape=jax.ShapeDtypeStruct((t, width), BF16),
        scratch_shapes=[pltpu.VMEM((SUBLANES, CONV_HALO + tc + SUBLANES, width), F32),
                        pltpu.VMEM((tc, width), F32)],
        compiler_params=_params("parallel"),
        name="conformer",
    )(proj, proj, proj, proj, conv_w.astype(F32), conv_b.reshape(1, width).astype(F32),
      ln_g.reshape(1, width).astype(F32), ln_b.reshape(1, width).astype(F32))


def _stickbreak_kernel(q_ref, k_ref, v_ref, tri_ref, o_ref, *, heads):
    i = pl.program_id(1)
    tq = q_ref.shape[0]
    col = lax.broadcasted_iota(jnp.int32, (tq, SB_KEYS), 1)
    q_pos = i * tq + lax.broadcasted_iota(jnp.int32, (tq, 1), 0)

    def walk_tile(upper, cs, accs):
        start = pl.multiple_of(jnp.maximum(upper - SB_KEYS, 0), tq)
        valid = col < jnp.minimum(q_pos, upper) - start
        head_cols = [slice(g * HEAD_DIM, (g + 1) * HEAD_DIM) for g in range(heads)]
        zs = [lax.dot_general(q_ref[:, cols], k_ref[pl.ds(start, SB_KEYS), cols], (((1,), (1,)), ((), ())),
                              preferred_element_type=F32) * (HEAD_DIM ** -0.5) for cols in head_cols]
        log_wins, splits, new_cs = [], [], []
        for g in range(heads):
            z = zs[g]
            softplus = jnp.maximum(z, 0.0) + jnp.log(1.0 + jnp.exp(-jnp.abs(z)))
            log_fail = jnp.where(valid, -softplus, 0.0)
            hi = log_fail.astype(BF16)
            lo = (log_fail - hi.astype(F32)).astype(BF16)
            splits.append(jnp.concatenate([hi, lo], axis=1))
            log_wins.append(z - softplus)
            new_cs.append(cs[g] + jnp.sum(log_fail, axis=-1, keepdims=True))
        later = jnp.dot(jnp.concatenate(splits, axis=0), tri_ref[...], preferred_element_type=F32)
        new_accs = []
        for g in range(heads):
            a = jnp.where(valid, jnp.exp(log_wins[g] + later[g * tq:(g + 1) * tq, :] + cs[g]), 0.0)
            new_accs.append(accs[g] + jnp.dot(a.astype(BF16), v_ref[pl.ds(start, SB_KEYS), head_cols[g]],
                                              preferred_element_type=F32))
        return start, new_cs, new_accs

    def cond(state):
        upper, cs, _ = state
        alive = functools.reduce(jnp.maximum, [jnp.max(c) for c in cs])
        return jnp.logical_and(upper > 0, alive > SB_LOG_ZERO)

    def body(state):
        return walk_tile(*state)

    init = ((i + 1) * tq,
            [jnp.zeros((tq, 1), F32) for _ in range(heads)],
            [jnp.zeros((tq, HEAD_DIM), F32) for _ in range(heads)])
    _, _, accs = lax.while_loop(cond, body, init)
    for g in range(heads):
        o_ref[:, g * HEAD_DIM:(g + 1) * HEAD_DIM] = accs[g].astype(o_ref.dtype)


def _stickbreak(proj, n_heads, col0):
    t = proj.shape[0]
    width = SB_HEADS * HEAD_DIM
    qb0 = col0 // width
    kb0 = qb0 + n_heads // SB_HEADS
    vb0 = kb0 + n_heads // SB_HEADS
    s_later = jnp.arange(2 * SB_KEYS)[:, None] % SB_KEYS
    tri = (s_later > jnp.arange(SB_KEYS)[None, :]).astype(BF16)
    resident = lambda c0: pl.BlockSpec((t, width), lambda h, i: (0, c0 + h), pipeline_mode=pl.Buffered(1))
    return pl.pallas_call(
        functools.partial(_stickbreak_kernel, heads=SB_HEADS),
        grid=(n_heads // SB_HEADS, t // CHUNK),
        in_specs=[pl.BlockSpec((CHUNK, width), lambda h, i: (i, qb0 + h)),
                  resident(kb0), resident(vb0),
                  pl.BlockSpec((2 * SB_KEYS, SB_KEYS), lambda h, i: (0, 0))],
        out_specs=pl.BlockSpec((CHUNK, width), lambda h, i: (i, h)),
        out_shape=jax.ShapeDtypeStruct((t, n_heads * HEAD_DIM), BF16),
        compiler_params=_params("parallel", "arbitrary"),
        name="stickbreak",
    )(proj, proj, proj, tri)


def _router_kernel(x_ref, g_ref, whi_ref, wlo_ref, b_ref, hp_ref, route_ref, counts_ref, carry):
    @pl.when(pl.program_id(0) == 0)
    def _():
        carry[...] = jnp.zeros_like(carry)

    x = x_ref[...]
    tm, d = x.shape
    h = x * lax.rsqrt(jnp.mean(x * x, axis=-1, keepdims=True) + EPS) * g_ref[...]
    h_hi = h.astype(BF16)
    h_hif = h_hi.astype(F32)
    h_lo = (h - h_hif).astype(BF16)
    logits = (jnp.dot(h_hi, whi_ref[...], preferred_element_type=F32)
              + jnp.dot(h_lo, whi_ref[...], preferred_element_type=F32)
              + jnp.dot(h_hi, wlo_ref[...], preferred_element_type=F32)) + b_ref[...]
    lane = lax.broadcasted_iota(jnp.int32, logits.shape, 1)
    neg = -jnp.inf

    def first_max(vals):
        m = jnp.max(vals, axis=-1, keepdims=True)
        idx = jnp.min(jnp.where(vals == m, lane, ROUTE_LANES), axis=-1, keepdims=True)
        return m, idx

    gl = jnp.where(lane < N_GROUPS, logits, neg)
    gmax, g_sel = first_max(gl)
    g_weight = 1.0 / jnp.sum(jnp.exp(gl - gmax), axis=-1, keepdims=True)
    expert = lane - N_GROUPS
    in_group = (expert >= 0) & (expert < N_EXPERTS) & ((expert >> 3) == g_sel)
    el = jnp.where(in_group, logits, neg)
    m1, i1 = first_max(el)
    m2, i2 = first_max(jnp.where(lane == i1, neg, el))
    e21 = jnp.exp(m2 - m1)
    w1 = g_weight / (1.0 + e21)
    w2 = w1 * e21
    sel1 = lane == i1
    sel2 = lane == i2
    chosen = jnp.where(jnp.logical_or(sel1, sel2), 1.0, 0.0)
    row = lax.broadcasted_iota(jnp.int32, (tm, tm), 0)
    col = lax.broadcasted_iota(jnp.int32, (tm, tm), 1)
    earlier = jnp.where(col < row, 1.0, 0.0).astype(BF16)
    before = jnp.dot(earlier, chosen.astype(BF16), preferred_element_type=F32) + carry[...]
    rank1 = jnp.sum(jnp.where(sel1, before, 0.0), axis=-1, keepdims=True)
    rank2 = jnp.sum(jnp.where(sel2, before, 0.0), axis=-1, keepdims=True)
    carry[...] = carry[...] + jnp.sum(chosen, axis=0, keepdims=True)
    counts_ref[...] = carry[...]
    fields = ((i1 - N_GROUPS).astype(F32), (i2 - N_GROUPS).astype(F32), w1, w2, rank1, rank2)
    rec = jnp.zeros_like(logits)
    for k, val in enumerate(fields):
        rec = jnp.where(lane == k, val, rec)
    route_ref[...] = rec
    bits = lax.bitcast_convert_type(h_hif, jnp.uint32)
    hp_ref[...] = (bits[:, :d // 2] >> 16) | (bits[:, d // 2:] & jnp.uint32(0xFFFF0000))


def _router(x, norm_g, wr_g, br_g, wr_e, br_e, tm=512):
    t, d = x.shape
    w = jnp.zeros((d, ROUTE_LANES), F32).at[:, :N_GROUPS].set(wr_g).at[:, N_GROUPS:N_GROUPS + N_EXPERTS].set(wr_e)
    b = jnp.zeros((1, ROUTE_LANES), F32).at[0, :N_GROUPS].set(br_g).at[0, N_GROUPS:N_GROUPS + N_EXPERTS].set(br_e)
    w_hi = w.astype(BF16)
    w_lo = (w - w_hi.astype(F32)).astype(BF16)
    tm = min(tm, t)
    return pl.pallas_call(
        _router_kernel,
        grid=(t // tm,),
        in_specs=[pl.BlockSpec((tm, d), lambda i: (i, 0)),
                  pl.BlockSpec((1, d), lambda i: (0, 0)),
                  pl.BlockSpec((d, ROUTE_LANES), lambda i: (0, 0)),
                  pl.BlockSpec((d, ROUTE_LANES), lambda i: (0, 0)),
                  pl.BlockSpec((1, ROUTE_LANES), lambda i: (0, 0))],
        out_specs=[pl.BlockSpec((tm, d // 2), lambda i: (i, 0)),
                   pl.BlockSpec((tm, ROUTE_LANES), lambda i: (i, 0)),
                   pl.BlockSpec((1, ROUTE_LANES), lambda i: (0, 0))],
        out_shape=[jax.ShapeDtypeStruct((t, d // 2), jnp.uint32),
                   jax.ShapeDtypeStruct((t, ROUTE_LANES), F32),
                   jax.ShapeDtypeStruct((1, ROUTE_LANES), F32)],
        scratch_shapes=[pltpu.VMEM((1, ROUTE_LANES), F32)],
        compiler_params=_params("arbitrary"),
        name="router",
    )(x, norm_g.reshape(1, d), w_hi, w_lo, b)


def _lookup(table, idx):
    hit = idx[..., None] == jnp.arange(table.shape[0], dtype=jnp.int32)
    return jnp.sum(jnp.where(hit, table, 0), axis=-1)


def _dispatch_tables(route, counts, t):
    counts = counts[0, N_GROUPS:N_GROUPS + N_EXPERTS].astype(jnp.int32)
    padded = (counts + EXPERT_BLOCK - 1) // EXPERT_BLOCK * EXPERT_BLOCK
    padded_end = jnp.cumsum(padded)
    padded_start = padded_end - padded
    expert = route[:, 0:2].astype(jnp.int32)
    rank = route[:, 4:6].astype(jnp.int32)
    assign_slot = (_lookup(padded_start, expert) + rank).reshape(-1)

    n_slots = (-(-2 * t // EXPERT_BLOCK) + N_EXPERTS) * EXPERT_BLOCK
    n_items = N_EXPERTS + n_slots // MOE_MAX_ROWS
    chunks = (padded + MOE_MAX_ROWS - 1) // MOE_MAX_ROWS
    chunk_end = jnp.cumsum(chunks)
    n_used = chunk_end[-1]
    ids = jnp.arange(n_items, dtype=jnp.int32)
    item_e = jnp.minimum(jnp.sum(chunk_end[None, :] <= ids[:, None], axis=1), N_EXPERTS - 1).astype(jnp.int32)
    j = ids - _lookup(chunk_end - chunks, item_e)
    item_start = _lookup(padded_start, item_e) + j * MOE_MAX_ROWS
    item_rows = jnp.clip(_lookup(padded, item_e) - j * MOE_MAX_ROWS, 0, MOE_MAX_ROWS)
    used = ids < n_used
    last_e = jnp.sum(jnp.where(ids == n_used - 1, item_e, 0))
    item_e = jnp.where(used, item_e, last_e).astype(jnp.int32)
    item_start = jnp.where(used, item_start, 0).astype(jnp.int32)
    item_rows = jnp.where(used, item_rows, 0).astype(jnp.int32)
    total = padded_end[-1].reshape(1).astype(jnp.int32)
    return (assign_slot.astype(jnp.int32), padded_start.astype(jnp.int32), padded.astype(jnp.int32), total,
            item_e, item_start, item_rows, n_slots, n_items)


def _dispatch_kernel(as_ref, ps_ref, pd_ref, tot_ref, hp_ref, xs_hbm, zbuf, sem, zsem):
    i = pl.program_id(0)
    tm = hp_ref.shape[0]
    n_blocks = xs_hbm.shape[0] // EXPERT_BLOCK

    def zero_block(first_row):
        return pltpu.make_async_copy(zbuf, xs_hbm.at[pl.ds(pl.multiple_of(first_row, EXPERT_BLOCK), EXPERT_BLOCK)],
                                     zsem)

    def zero_fill(act):
        def per_expert(e, carry):
            @pl.when(pd_ref[e] > 0)
            def _():
                act(zero_block(ps_ref[e] + pd_ref[e] - EXPERT_BLOCK))
            return carry
        lax.fori_loop(0, N_EXPERTS, per_expert, 0)

        def per_tail_block(b, carry):
            act(zero_block(b * EXPERT_BLOCK))
            return carry
        lax.fori_loop(tot_ref[0] // EXPERT_BLOCK, n_blocks, per_tail_block, 0)

    @pl.when(i == 0)
    def _():
        zbuf[...] = jnp.zeros_like(zbuf)
        zero_fill(lambda cp: cp.start())
        zero_fill(lambda cp: cp.wait())

    def body(r, carry):
        a = 2 * (i * tm + r)
        for k in range(2):
            pltpu.make_async_copy(hp_ref.at[pl.ds(r, 1)], xs_hbm.at[pl.ds(as_ref[a + k], 1)], sem).start()
        return carry
    lax.fori_loop(0, tm, body, 0, unroll=8)
    for k in range(2):
        pltpu.make_async_copy(hp_ref, xs_hbm.at[pl.ds(0, tm)], sem).wait()


def _dispatch(hp, assign_slot, padded_start, padded, total, n_slots, tm=256):
    t, words = hp.shape
    tm = min(tm, t)
    return pl.pallas_call(
        _dispatch_kernel,
        grid_spec=pltpu.PrefetchScalarGridSpec(
            num_scalar_prefetch=4,
            grid=(t // tm,),
            in_specs=[pl.BlockSpec((tm, words), lambda i, *_: (i, 0))],
            out_specs=pl.BlockSpec(memory_space=pl.ANY),
            scratch_shapes=[pltpu.VMEM((EXPERT_BLOCK, words), hp.dtype),
                            pltpu.SemaphoreType.DMA(()),
                            pltpu.SemaphoreType.DMA(())],
        ),
        out_shape=jax.ShapeDtypeStruct((n_slots, words), hp.dtype),
        compiler_params=_params("arbitrary"),
        name="moe_dispatch",
    )(assign_slot, padded_start, padded, total, hp)


def _unpack_pairs(words):
    lo = lax.bitcast_convert_type(words << 16, F32)
    hi = lax.bitcast_convert_type(words & jnp.uint32(0xFFFF0000), F32)
    return lo, hi


def _pack_pairs(lo, hi):
    lo_bits = lax.bitcast_convert_type(lo.astype(BF16).astype(F32), jnp.uint32)
    hi_bits = lax.bitcast_convert_type(hi.astype(BF16).astype(F32), jnp.uint32)
    return (lo_bits >> 16) | (hi_bits & jnp.uint32(0xFFFF0000))


def _expert_items_kernel(ie_ref, is_ref, ir_ref, tot_ref, in_hbm, *refs, n_weights, cast_weights, compute):
    w_refs = refs[:n_weights]
    out_hbm = refs[n_weights]
    inbuf, outbuf, zbuf, w_bf, sem_in, sem_out, sem_z = refs[n_weights + 1:]
    i = pl.program_id(0)
    n_items = pl.num_programs(0)
    max_blocks = inbuf.shape[1] // EXPERT_BLOCK
    n_out_blocks = out_hbm.shape[0] // EXPERT_BLOCK

    def rows_of(hbm, first_row):
        return hbm.at[pl.ds(pl.multiple_of(first_row, EXPERT_BLOCK), EXPERT_BLOCK)]

    def for_blocks(item, act):
        for c in range(max_blocks):
            @pl.when(c * EXPERT_BLOCK < ir_ref[item])
            def _():
                act(c, is_ref[item] + c * EXPERT_BLOCK)

    def in_copy(buf, act):
        return lambda c, row: act(pltpu.make_async_copy(
            rows_of(in_hbm, row), inbuf.at[buf, pl.ds(c * EXPERT_BLOCK, EXPERT_BLOCK)], sem_in.at[buf]))

    def out_copy(buf, act):
        return lambda c, row: act(pltpu.make_async_copy(
            outbuf.at[buf, pl.ds(c * EXPERT_BLOCK, EXPERT_BLOCK)], rows_of(out_hbm, row), sem_out.at[buf]))

    start = lambda cp: cp.start()
    wait = lambda cp: cp.wait()
    buf = i % 2

    @pl.when(i == 0)
    def _():
        for_blocks(0, in_copy(0, start))
        zbuf[...] = jnp.zeros_like(zbuf)

        def tail(act):
            def body(b, carry):
                act(pltpu.make_async_copy(zbuf, rows_of(out_hbm, b * EXPERT_BLOCK), sem_z))
                return carry
            lax.fori_loop(tot_ref[0] // EXPERT_BLOCK, n_out_blocks, body, 0)
        tail(start)
        tail(wait)

    @pl.when(i + 1 < n_items)
    def _():
        for_blocks(i + 1, in_copy(1 - buf, start))

    for_blocks(i, in_copy(buf, wait))

    @pl.when(jnp.logical_or(i == 0, ie_ref[i] != ie_ref[jnp.maximum(i - 1, 0)]))
    def _():
        cast_weights(w_refs, w_bf)

    @pl.when(i >= 2)
    def _():
        for_blocks(i - 2, out_copy(buf, wait))

    n_blk = ir_ref[i] // EXPERT_BLOCK
    for c in range(1, max_blocks + 1):
        @pl.when(n_blk == c)
        def _():
            rows = c * EXPERT_BLOCK
            outbuf[buf, 0:rows, :] = compute(inbuf[buf, 0:rows, :], w_bf)

    for_blocks(i, out_copy(buf, start))

    @pl.when(i == n_items - 1)
    def _():
        @pl.when(i >= 1)
        def _():
            for_blocks(i - 1, out_copy(1 - buf, wait))
        for_blocks(i, out_copy(buf, wait))


def _expert_items_call(name, items, in_rows, weights, out_width, out_dtype, n_w_cols, cast_weights, compute):
    item_e, item_start, item_rows, total = items
    n_slots, in_width = in_rows.shape
    k_dim = weights[0].shape[1]
    w_spec = lambda w: pl.BlockSpec((None,) + w.shape[1:], lambda i, ie, *_: (ie[i], 0, 0))
    return pl.pallas_call(
        functools.partial(_expert_items_kernel, n_weights=len(weights), cast_weights=cast_weights, compute=compute),
        grid_spec=pltpu.PrefetchScalarGridSpec(
            num_scalar_prefetch=4,
            grid=(item_e.shape[0],),
            in_specs=[pl.BlockSpec(memory_space=pl.ANY)] + [w_spec(w) for w in weights],
            out_specs=pl.BlockSpec(memory_space=pl.ANY),
            scratch_shapes=[pltpu.VMEM((2, MOE_MAX_ROWS, in_width), in_rows.dtype),
                            pltpu.VMEM((2, MOE_MAX_ROWS, out_width), out_dtype),
                            pltpu.VMEM((EXPERT_BLOCK, out_width), out_dtype),
                            pltpu.VMEM((k_dim, n_w_cols), BF16),
                            pltpu.SemaphoreType.DMA((2,)),
                            pltpu.SemaphoreType.DMA((2,)),
                            pltpu.SemaphoreType.DMA(())],
        ),
        out_shape=jax.ShapeDtypeStruct((n_slots, out_width), out_dtype),
        compiler_params=pltpu.CompilerParams(dimension_semantics=("arbitrary",),
                                             vmem_limit_bytes=V7X_VMEM_LIMIT_MOE),
        name=name,
    )(item_e, item_start, item_rows, total, in_rows, *weights)


def _cast_gate_up(w_refs, w_bf):
    d_exp = w_refs[0].shape[1]
    w_bf[:, 0:d_exp] = w_refs[0][...].astype(BF16)
    w_bf[:, d_exp:2 * d_exp] = w_refs[1][...].astype(BF16)


def _gate_up(words, w_bf):
    half = words.shape[1]
    d_exp = w_bf.shape[1] // 2
    lo, hi = _unpack_pairs(words)
    gu = (jnp.dot(lo.astype(BF16), w_bf[0:half, :], preferred_element_type=F32)
          + jnp.dot(hi.astype(BF16), w_bf[half:2 * half, :], preferred_element_type=F32))
    gate, up = gu[:, 0:d_exp], gu[:, d_exp:2 * d_exp]
    return (gate * jax.nn.sigmoid(gate) * up).astype(BF16)


def _cast_down(w_refs, w_bf):
    w_bf[...] = w_refs[0][...].astype(BF16)


def _down(hid, w_bf):
    half = w_bf.shape[1] // 2
    y = jnp.dot(hid, w_bf[...], preferred_element_type=F32)
    return _pack_pairs(y[:, 0:half], y[:, half:2 * half])


def _combine_kernel(as_ref, x_ref, route_ref, y_hbm, o_ref, ybuf, sem):
    i = pl.program_id(0)
    tm, d = x_ref.shape
    half = d // 2

    def start_gather(tile, buf):
        def body(r, carry):
            a = 2 * (tile * tm + r)
            for k in range(2):
                pltpu.make_async_copy(y_hbm.at[pl.ds(as_ref[a + k], 1)], ybuf.at[buf, k, pl.ds(r, 1)],
                                      sem.at[buf]).start()
            return carry
        lax.fori_loop(0, tm, body, 0, unroll=8)

    @pl.when(i == 0)
    def _():
        start_gather(0, 0)

    @pl.when(i + 1 < pl.num_programs(0))
    def _():
        start_gather(i + 1, (i + 1) % 2)

    buf = i % 2
    for k in range(2):
        pltpu.make_async_copy(y_hbm.at[pl.ds(0, tm)], ybuf.at[buf, k], sem.at[buf]).wait()
    w1 = route_ref[:, 2:3]
    w2 = route_ref[:, 3:4]
    lo1, hi1 = _unpack_pairs(ybuf[buf, 0])
    lo2, hi2 = _unpack_pairs(ybuf[buf, 1])
    o_ref[:, 0:half] = x_ref[:, 0:half] + w1 * lo1 + w2 * lo2
    o_ref[:, half:d] = x_ref[:, half:d] + w1 * hi1 + w2 * hi2


def _moe(x, norm_g, wr_g, br_g, wr_e, br_e, w_gate, w_up, w_down, tm_combine=256):
    t, d = x.shape
    d_exp = w_gate.shape[2]
    hp, route, counts = _router(x, norm_g, wr_g, br_g, wr_e, br_e)
    (assign_slot, padded_start, padded, total, item_e, item_start, item_rows,
     n_slots, _) = _dispatch_tables(route, counts, t)
    items = (item_e, item_start, item_rows, total)
    xs = _dispatch(hp, assign_slot, padded_start, padded, total, n_slots)
    hid = _expert_items_call("moe_up", items, xs, (w_gate, w_up), d_exp, BF16, 2 * d_exp,
                             _cast_gate_up, _gate_up)
    y = _expert_items_call("moe_down", items, hid, (w_down,), d // 2, jnp.uint32, d, _cast_down, _down)

    tm = min(tm_combine, t)
    return pl.pallas_call(
        _combine_kernel,
        grid_spec=pltpu.PrefetchScalarGridSpec(
            num_scalar_prefetch=1,
            grid=(t // tm,),
            in_specs=[pl.BlockSpec((tm, d), lambda i, a: (i, 0)),
                      pl.BlockSpec((tm, ROUTE_LANES), lambda i, a: (i, 0)),
                      pl.BlockSpec(memory_space=pl.ANY)],
            out_specs=pl.BlockSpec((tm, d), lambda i, a: (i, 0)),
            scratch_shapes=[pltpu.VMEM((2, 2, tm, d // 2), jnp.uint32),
                            pltpu.SemaphoreType.DMA((2,))],
        ),
        out_shape=jax.ShapeDtypeStruct((t, d), F32),
        compiler_params=_params("arbitrary"),
        name="moe_combine",
    )(assign_slot, x, route, y)


def _even_layer_mixer(x, mix_norm, w_in, a_ln_g, a_ln_b, a_ws, a_bs, b_q_norm, b_k_norm, b_sinks, w_out):
    n_groups = a_ws.shape[0]
    a_width = n_groups * HEAD_DIM
    n_heads = b_sinks.shape[0]
    n_kv = (w_in.shape[1] - 2 * a_width - n_heads * HEAD_DIM) // (2 * HEAD_DIM)
    h = _rmsnorm(x, mix_norm)
    proj = _matmul(h, w_in.astype(BF16), tm=1024, tn=1024)
    y_a = _gmlp(proj, a_ln_g, a_ln_b, a_ws, a_bs)
    y_b = _swa(proj, b_q_norm, b_k_norm, b_sinks, n_heads, n_kv, col0=2 * a_width)
    return _out_proj(x, y_a, y_b, w_out.astype(BF16))


def _odd_layer_mixer(x, mix_norm, w_in, c_conv_w, c_conv_b, c_ln_g, c_ln_b, w_out):
    c_width = c_conv_w.shape[1]
    n_heads = (w_in.shape[1] - 2 * c_width) // (3 * HEAD_DIM)
    h = _rmsnorm(x, mix_norm)
    proj = _matmul(h, w_in.astype(BF16), tm=1024, tn=1024)
    y_c = _conformer(proj, c_conv_w, c_conv_b, c_ln_g, c_ln_b)
    y_d = _stickbreak(proj, n_heads, col0=2 * c_width)
    return _out_proj(x, y_c, y_d, w_out.astype(BF16))


def kernel(x, l0_mix_norm, l0_w_in, l0_a_ln_g, l0_a_ln_b, l0_a_ws, l0_a_bs, l0_b_q_norm, l0_b_k_norm, l0_b_sinks, l0_w_out, l0_ffn_norm, l0_router_group_w, l0_router_group_b, l0_router_expert_w, l0_router_expert_b, l0_w_gate, l0_w_up, l0_w_down, l1_mix_norm, l1_w_in, l1_c_conv_w, l1_c_conv_b, l1_c_ln_g, l1_c_ln_b, l1_w_out, l1_ffn_norm, l1_router_group_w, l1_router_group_b, l1_router_expert_w, l1_router_expert_b, l1_w_gate, l1_w_up, l1_w_down):
    bsz, seq, d = x.shape
    xt = x.reshape(bsz * seq, d)
    assert bsz == 1, "the mixers index positions by row; one sequence per call"
    xt = _even_layer_mixer(xt, l0_mix_norm, l0_w_in, l0_a_ln_g, l0_a_ln_b, l0_a_ws, l0_a_bs,
                           l0_b_q_norm, l0_b_k_norm, l0_b_sinks, l0_w_out)
    xt = _moe(xt, l0_ffn_norm, l0_router_group_w, l0_router_group_b, l0_router_expert_w,
              l0_router_expert_b, l0_w_gate, l0_w_up, l0_w_down)
    xt = _odd_layer_mixer(xt, l1_mix_norm, l1_w_in, l1_c_conv_w, l1_c_conv_b, l1_c_ln_g, l1_c_ln_b, l1_w_out)
    xt = _moe(xt, l1_ffn_norm, l1_router_group_w, l1_router_group_b, l1_router_expert_w,
              l1_router_expert_b, l1_w_gate, l1_w_up, l1_w_down)
    return xt.reshape(bsz, seq, d)
```

```python
import functools

import jax
import jax.numpy as jnp
from jax import lax
from jax.experimental import pallas as pl
from jax.experimental.pallas import tpu as pltpu

F32 = jnp.float32
BF16 = jnp.bfloat16

HEAD_DIM = 128
CHUNK = 128
EPS = 1e-6
N_GROUPS = 8
EXPERTS_PER_GROUP = 8
N_EXPERTS = N_GROUPS * EXPERTS_PER_GROUP
EXPERT_BLOCK = 128
CONV_WIDTH = 31
SUBLANES = 8
CONV_ROWS = 128
CONV_HALO = 32
ROUTE_LANES = 128
SB_LOG_ZERO = -104.0
SB_HEADS = 4
SB_KEYS = 384
COMBINE_ROWS = 16
COMBINE_COLS = 512
MOE_MAX_ROWS = 768
V7X_VMEM_LIMIT = 56 * 1024 * 1024
V7X_VMEM_LIMIT_MOE = 60 * 1024 * 1024


def _params(*sem):
    return pltpu.CompilerParams(dimension_semantics=sem, vmem_limit_bytes=V7X_VMEM_LIMIT)


def _rmsnorm_kernel(x_ref, g_ref, o_ref):
    x = x_ref[...]
    ms = jnp.mean(x * x, axis=-1, keepdims=True)
    o_ref[...] = (x * lax.rsqrt(ms + EPS) * g_ref[...]).astype(o_ref.dtype)


def _rmsnorm(x, g, tm=256):
    t, d = x.shape
    return pl.pallas_call(
        _rmsnorm_kernel,
        grid=(t // tm,),
        in_specs=[pl.BlockSpec((tm, d), lambda i: (i, 0)),
                  pl.BlockSpec((1, d), lambda i: (0, 0))],
        out_specs=pl.BlockSpec((tm, d), lambda i: (i, 0)),
        out_shape=jax.ShapeDtypeStruct((t, d), BF16),
        compiler_params=_params("parallel"),
        name="rmsnorm",
    )(x, g.reshape(1, d))


def _matmul_kernel(a_ref, w_ref, o_ref):
    o_ref[...] = jnp.dot(a_ref[...], w_ref[...], preferred_element_type=F32).astype(o_ref.dtype)


def _matmul(a, w, tm, tn):
    m, k = a.shape
    n = w.shape[1]
    tm = min(tm, m)
    return pl.pallas_call(
        _matmul_kernel,
        grid=(m // tm, n // tn),
        in_specs=[pl.BlockSpec((tm, k), lambda i, j: (i, 0)),
                  pl.BlockSpec((k, tn), lambda i, j: (0, j))],
        out_specs=pl.BlockSpec((tm, tn), lambda i, j: (i, j)),
        out_shape=jax.ShapeDtypeStruct((m, n), BF16),
        compiler_params=_params("parallel", "arbitrary"),
        name="in_proj",
    )(a, w)


def _out_proj_kernel(x_ref, a_ref, b_ref, wa_ref, wb_ref, o_ref):
    acc = jnp.dot(a_ref[...], wa_ref[...], preferred_element_type=F32)
    acc = acc + jnp.dot(b_ref[...], wb_ref[...], preferred_element_type=F32)
    o_ref[...] = x_ref[...] + acc


def _out_proj(x, ya, yb, w, tm=1024, tn=512):
    t, d = x.shape
    ka, kb = ya.shape[1], yb.shape[1]
    assert ka == kb and w.shape == (ka + kb, d)
    tm = min(tm, t)
    return pl.pallas_call(
        _out_proj_kernel,
        grid=(t // tm, d // tn),
        in_specs=[pl.BlockSpec((tm, tn), lambda i, j: (i, j)),
                  pl.BlockSpec((tm, ka), lambda i, j: (i, 0)),
                  pl.BlockSpec((tm, kb), lambda i, j: (i, 0)),
                  pl.BlockSpec((ka, tn), lambda i, j: (0, j)),
                  pl.BlockSpec((kb, tn), lambda i, j: (1, j))],
        out_specs=pl.BlockSpec((tm, tn), lambda i, j: (i, j)),
        out_shape=jax.ShapeDtypeStruct((t, d), F32),
        compiler_params=_params("parallel", "arbitrary"),
        name="out_proj",
    )(x, ya, yb, w, w)


def _layer_norm(x, g, b):
    mu = jnp.mean(x, axis=-1, keepdims=True)
    xc = x - mu
    var = jnp.mean(xc * xc, axis=-1, keepdims=True)
    return xc * lax.rsqrt(var + EPS) * g + b


def _gelu(x):
    return 0.5 * x * (1.0 + lax.erf(x * (2.0 ** -0.5)))


def _gmlp_kernel(u_ref, v_ref, lng_ref, lnb_ref, w_ref, bs_ref, o_ref):
    n_groups = w_ref.shape[0]
    v = _gelu(v_ref[...].astype(F32))
    vn = _layer_norm(v, lng_ref[...], lnb_ref[...]).astype(BF16)
    group_cols = [slice(g * HEAD_DIM, (g + 1) * HEAD_DIM) for g in range(n_groups)]
    mixed = [jnp.dot(w_ref[g], vn[:, cols], preferred_element_type=F32) for g, cols in enumerate(group_cols)]
    for g, cols in enumerate(group_cols):
        u = _gelu(u_ref[:, cols].astype(F32))
        o_ref[:, cols] = (u * (mixed[g] + bs_ref[:, cols])).astype(o_ref.dtype)


def _gmlp(proj, ln_g, ln_b, w_s, b_s):
    t = proj.shape[0]
    n_groups = w_s.shape[0]
    width = n_groups * HEAD_DIM
    causal = jnp.tril(jnp.ones((CHUNK, CHUNK), dtype=bool))
    w = jnp.where(causal[None], w_s, 0.0).astype(BF16)
    bs = jnp.repeat(b_s.T.astype(F32), HEAD_DIM, axis=1)
    return pl.pallas_call(
        _gmlp_kernel,
        grid=(t // CHUNK,),
        in_specs=[pl.BlockSpec((CHUNK, width), lambda i: (i, 0)),
                  pl.BlockSpec((CHUNK, width), lambda i: (i, 1)),
                  pl.BlockSpec((1, width), lambda i: (0, 0)),
                  pl.BlockSpec((1, width), lambda i: (0, 0)),
                  pl.BlockSpec((n_groups, CHUNK, CHUNK), lambda i: (0, 0, 0)),
                  pl.BlockSpec((CHUNK, width), lambda i: (0, 0))],
        out_specs=pl.BlockSpec((CHUNK, width), lambda i: (i, 0)),
        out_shape=jax.ShapeDtypeStruct((t, width), BF16),
        compiler_params=_params("parallel"),
        name="gmlp",
    )(proj, proj, ln_g.reshape(1, width), ln_b.reshape(1, width), w, bs)


def _swa_kernel(slopes_ref, sinks_ref, q_ref, kc_ref, kp_ref, vc_ref, vp_ref, qg_ref, kg_ref, o_ref, *, grp):
    n = pl.program_id(0)
    h = pl.program_id(1)
    k = jnp.concatenate([kp_ref[...], kc_ref[...]], axis=0).astype(F32)
    k = k * lax.rsqrt(jnp.mean(k * k, axis=-1, keepdims=True) + EPS) * kg_ref[...]
    kb = k.astype(BF16)
    vb = jnp.concatenate([vp_ref[...], vc_ref[...]], axis=0)
    qi = lax.broadcasted_iota(jnp.int32, (CHUNK, 2 * CHUNK), 0)
    kj = lax.broadcasted_iota(jnp.int32, (CHUNK, 2 * CHUNK), 1)
    dist = qi + CHUNK - kj
    valid = (dist >= 0) & (dist < CHUNK) & (n * CHUNK - CHUNK + kj >= 0)
    distf = dist.astype(F32)
    head_cols = [slice(g * HEAD_DIM, (g + 1) * HEAD_DIM) for g in range(grp)]
    scores = []
    for cols in head_cols:
        q = q_ref[:, cols].astype(F32)
        q = q * lax.rsqrt(jnp.mean(q * q, axis=-1, keepdims=True) + EPS) * qg_ref[...]
        scores.append(lax.dot_general(q.astype(BF16), kb, (((1,), (1,)), ((), ())),
                                      preferred_element_type=F32) * (HEAD_DIM ** -0.5))
    probs, denoms = [], []
    for g in range(grp):
        slope = slopes_ref[h * grp + g]
        sink = sinks_ref[h * grp + g]
        s = jnp.where(valid, scores[g] - slope * distf, -jnp.inf)
        m = jnp.maximum(jnp.max(s, axis=-1, keepdims=True), sink)
        p = jnp.exp(s - m)
        denoms.append(jnp.sum(p, axis=-1, keepdims=True) + jnp.exp(sink - m))
        probs.append(p.astype(BF16))
    for g in range(grp):
        o = jnp.dot(probs[g], vb, preferred_element_type=F32) / denoms[g]
        o_ref[:, head_cols[g]] = o.astype(o_ref.dtype)


def _swa(proj, q_gain, k_gain, sinks, n_heads, n_kv, col0):
    t = proj.shape[0]
    grp = n_heads // n_kv
    qb0 = col0 // (grp * HEAD_DIM)
    kb0 = (col0 + n_heads * HEAD_DIM) // HEAD_DIM
    vb0 = kb0 + n_kv
    slopes = 2.0 ** (-8.0 * jnp.arange(1, n_heads + 1, dtype=F32) / n_heads)
    prev = lambda n: jnp.maximum(n - 1, 0)
    return pl.pallas_call(
        functools.partial(_swa_kernel, grp=grp),
        grid_spec=pltpu.PrefetchScalarGridSpec(
            num_scalar_prefetch=2,
            grid=(t // CHUNK, n_kv),
            in_specs=[pl.BlockSpec((CHUNK, grp * HEAD_DIM), lambda n, h, *_: (n, qb0 + h)),
                      pl.BlockSpec((CHUNK, HEAD_DIM), lambda n, h, *_: (n, kb0 + h)),
                      pl.BlockSpec((CHUNK, HEAD_DIM), lambda n, h, *_: (prev(n), kb0 + h)),
                      pl.BlockSpec((CHUNK, HEAD_DIM), lambda n, h, *_: (n, vb0 + h)),
                      pl.BlockSpec((CHUNK, HEAD_DIM), lambda n, h, *_: (prev(n), vb0 + h)),
                      pl.BlockSpec((1, HEAD_DIM), lambda n, h, *_: (0, 0)),
                      pl.BlockSpec((1, HEAD_DIM), lambda n, h, *_: (0, 0))],
            out_specs=pl.BlockSpec((CHUNK, grp * HEAD_DIM), lambda n, h, *_: (n, h)),
        ),
        out_shape=jax.ShapeDtypeStruct((t, n_heads * HEAD_DIM), BF16),
        compiler_params=_params("parallel", "arbitrary"),
        name="swa",
    )(slopes, sinks.astype(F32), proj, proj, proj, proj, proj,
      q_gain.reshape(1, HEAD_DIM).astype(F32), k_gain.reshape(1, HEAD_DIM).astype(F32))


def _conformer_kernel(a_ref, g_ref, ap_ref, gp_ref, w_ref, cb_ref, lg_ref, lb_ref, o_ref, hbuf, ybuf):
    i = pl.program_id(0)
    tc, width = a_ref.shape
    span = CONV_HALO + tc
    hp = ap_ref[...].astype(F32) * jax.nn.sigmoid(gp_ref[...].astype(F32))
    hbuf[0, 0:CONV_HALO, :] = jnp.where(i > 0, hp, 0.0)
    hbuf[0, CONV_HALO:span, :] = a_ref[...].astype(F32) * jax.nn.sigmoid(g_ref[...].astype(F32))
    hbuf[0, span:span + SUBLANES, :] = jnp.zeros((SUBLANES, width), F32)
    for r in range(1, SUBLANES):
        hbuf[r, 0:span, :] = hbuf[0, r:r + span, :]
    base = CONV_HALO - (CONV_WIDTH - 1)
    for c in range(width // HEAD_DIM):
        cols = slice(c * HEAD_DIM, (c + 1) * HEAD_DIM)
        for t0 in range(0, tc, CONV_ROWS):
            acc = jnp.zeros((CONV_ROWS, HEAD_DIM), F32)
            for j in range(CONV_WIDTH):
                r = (base + j) % SUBLANES
                row0 = t0 + base + j - r
                acc = acc + w_ref[j:j + 1, cols] * hbuf[r, row0:row0 + CONV_ROWS, cols]
            ybuf[t0:t0 + CONV_ROWS, cols] = acc
    y = _layer_norm(ybuf[...] + cb_ref[...], lg_ref[...], lb_ref[...])
    o_ref[...] = (y * jax.nn.sigmoid(y)).astype(o_ref.dtype)


def _conformer(proj, conv_w, conv_b, ln_g, ln_b, tc=256):
    t = proj.shape[0]
    width = conv_w.shape[1]
    tc = min(tc, t)
    halo_blocks = tc // CONV_HALO
    prev = lambda i: jnp.maximum(i * halo_blocks - 1, 0)
    vec = lambda: pl.BlockSpec((1, width), lambda i: (0, 0))
    return pl.pallas_call(
        _conformer_kernel,
        grid=(t // tc,),
        in_specs=[pl.BlockSpec((tc, width), lambda i: (i, 0)),
                  pl.BlockSpec((tc, width), lambda i: (i, 1)),
                  pl.BlockSpec((CONV_HALO, width), lambda i: (prev(i), 0)),
                  pl.BlockSpec((CONV_HALO, width), lambda i: (prev(i), 1)),
                  pl.BlockSpec((CONV_WIDTH, width), lambda i: (0, 0)),
                  vec(), vec(), vec()],
        out_specs=pl.BlockSpec((tc, width), lambda i: (i, 0)),
        out_shape=jax.ShapeDtypeStruct((t, width), BF16),
        scratch_shapes=[pltpu.VMEM((SUBLANES, CONV_HALO + tc + SUBLANES, width), F32),
                        pltpu.VMEM((tc, width), F32)],
        compiler_params=_params("parallel"),
        name="conformer",
    )(proj, proj, proj, proj, conv_w.astype(F32), conv_b.reshape(1, width).astype(F32),
      ln_g.reshape(1, width).astype(F32), ln_b.reshape(1, width).astype(F32))


def _stickbreak_kernel(q_ref, k_ref, v_ref, tri_ref, o_ref, *, heads):
    i = pl.program_id(1)
    tq = q_ref.shape[0]
    col = lax.broadcasted_iota(jnp.int32, (tq, SB_KEYS), 1)
    q_pos = i * tq + lax.broadcasted_iota(jnp.int32, (tq, 1), 0)

    def walk_tile(upper, cs, accs):
        start = pl.multiple_of(jnp.maximum(upper - SB_KEYS, 0), tq)
        valid = col < jnp.minimum(q_pos, upper) - start
        head_cols = [slice(g * HEAD_DIM, (g + 1) * HEAD_DIM) for g in range(heads)]
        zs = [lax.dot_general(q_ref[:, cols], k_ref[pl.ds(start, SB_KEYS), cols], (((1,), (1,)), ((), ())),
                              preferred_element_type=F32) * (HEAD_DIM ** -0.5) for cols in head_cols]
        log_wins, splits, new_cs = [], [], []
        for g in range(heads):
            z = zs[g]
            softplus = jnp.maximum(z, 0.0) + jnp.log(1.0 + jnp.exp(-jnp.abs(z)))
            log_fail = jnp.where(valid, -softplus, 0.0)
            hi = log_fail.astype(BF16)
            lo = (log_fail - hi.astype(F32)).astype(BF16)
            splits.append(jnp.concatenate([hi, lo], axis=1))
            log_wins.append(z - softplus)
            new_cs.append(cs[g] + jnp.sum(log_fail, axis=-1, keepdims=True))
        later = jnp.dot(jnp.concatenate(splits, axis=0), tri_ref[...], preferred_element_type=F32)
        new_accs = []
        for g in range(heads):
            a = jnp.where(valid, jnp.exp(log_wins[g] + later[g * tq:(g + 1) * tq, :] + cs[g]), 0.0)
            new_accs.append(accs[g] + jnp.dot(a.astype(BF16), v_ref[pl.ds(start, SB_KEYS), head_cols[g]],
                                              preferred_element_type=F32))
        return start, new_cs, new_accs

    def cond(state):
        upper, cs, _ = state
        alive = functools.reduce(jnp.maximum, [jnp.max(c) for c in cs])
        return jnp.logical_and(upper > 0, alive > SB_LOG_ZERO)

    def body(state):
        return walk_tile(*state)

    init = ((i + 1) * tq,
            [jnp.zeros((tq, 1), F32) for _ in range(heads)],
            [jnp.zeros((tq, HEAD_DIM), F32) for _ in range(heads)])
    _, _, accs = lax.while_loop(cond, body, init)
    for g in range(heads):
        o_ref[:, g * HEAD_DIM:(g + 1) * HEAD_DIM] = accs[g].astype(o_ref.dtype)


def _stickbreak(proj, n_heads, col0):
    t = proj.shape[0]
    width = SB_HEADS * HEAD_DIM
    qb0 = col0 // width
    kb0 = qb0 + n_heads // SB_HEADS
    vb0 = kb0 + n_heads // SB_HEADS
    s_later = jnp.arange(2 * SB_KEYS)[:, None] % SB_KEYS
    tri = (s_later > jnp.arange(SB_KEYS)[None, :]).astype(BF16)
    resident = lambda c0: pl.BlockSpec((t, width), lambda h, i: (0, c0 + h), pipeline_mode=pl.Buffered(1))
    return pl.pallas_call(
        functools.partial(_stickbreak_kernel, heads=SB_HEADS),
        grid=(n_heads // SB_HEADS, t // CHUNK),
        in_specs=[pl.BlockSpec((CHUNK, width), lambda h, i: (i, qb0 + h)),
                  resident(kb0), resident(vb0),
                  pl.BlockSpec((2 * SB_KEYS, SB_KEYS), lambda h, i: (0, 0))],
        out_specs=pl.BlockSpec((CHUNK, width), lambda h, i: (i, h)),
        out_shape=jax.ShapeDtypeStruct((t, n_heads * HEAD_DIM), BF16),
        compiler_params=_params("parallel", "arbitrary"),
        name="stickbreak",
    )(proj, proj, proj, tri)


def _router_kernel(x_ref, g_ref, whi_ref, wlo_ref, b_ref, hp_ref, route_ref, counts_ref, carry):
    @pl.when(pl.program_id(0) == 0)
    def _():
        carry[...] = jnp.zeros_like(carry)

    x = x_ref[...]
    tm, d = x.shape
    h = x * lax.rsqrt(jnp.mean(x * x, axis=-1, keepdims=True) + EPS) * g_ref[...]
    h_hi = h.astype(BF16)
    h_hif = h_hi.astype(F32)
    h_lo = (h - h_hif).astype(BF16)
    logits = (jnp.dot(h_hi, whi_ref[...], preferred_element_type=F32)
              + jnp.dot(h_lo, whi_ref[...], preferred_element_type=F32)
              + jnp.dot(h_hi, wlo_ref[...], preferred_element_type=F32)) + b_ref[...]
    lane = lax.broadcasted_iota(jnp.int32, logits.shape, 1)
    neg = -jnp.inf

    def first_max(vals):
        m = jnp.max(vals, axis=-1, keepdims=True)
        idx = jnp.min(jnp.where(vals == m, lane, ROUTE_LANES), axis=-1, keepdims=True)
        return m, idx

    gl = jnp.where(lane < N_GROUPS, logits, neg)
    gmax, g_sel = first_max(gl)
    g_weight = 1.0 / jnp.sum(jnp.exp(gl - gmax), axis=-1, keepdims=True)
    expert = lane - N_GROUPS
    in_group = (expert >= 0) & (expert < N_EXPERTS) & ((expert >> 3) == g_sel)
    el = jnp.where(in_group, logits, neg)
    m1, i1 = first_max(el)
    m2, i2 = first_max(jnp.where(lane == i1, neg, el))
    e21 = jnp.exp(m2 - m1)
    w1 = g_weight / (1.0 + e21)
    w2 = w1 * e21
    sel1 = lane == i1
    sel2 = lane == i2
    chosen = jnp.where(jnp.logical_or(sel1, sel2), 1.0, 0.0)
    row = lax.broadcasted_iota(jnp.int32, (tm, tm), 0)
    col = lax.broadcasted_iota(jnp.int32, (tm, tm), 1)
    earlier = jnp.where(col < row, 1.0, 0.0).astype(BF16)
    before = jnp.dot(earlier, chosen.astype(BF16), preferred_element_type=F32) + carry[...]
    rank1 = jnp.sum(jnp.where(sel1, before, 0.0), axis=-1, keepdims=True)
    rank2 = jnp.sum(jnp.where(sel2, before, 0.0), axis=-1, keepdims=True)
    carry[...] = carry[...] + jnp.sum(chosen, axis=0, keepdims=True)
    counts_ref[...] = carry[...]
    fields = ((i1 - N_GROUPS).astype(F32), (i2 - N_GROUPS).astype(F32), w1, w2, rank1, rank2)
    rec = jnp.zeros_like(logits)
    for k, val in enumerate(fields):
        rec = jnp.where(lane == k, val, rec)
    route_ref[...] = rec
    bits = lax.bitcast_convert_type(h_hif, jnp.uint32)
    hp_ref[...] = (bits[:, :d // 2] >> 16) | (bits[:, d // 2:] & jnp.uint32(0xFFFF0000))


def _router(x, norm_g, wr_g, br_g, wr_e, br_e, tm=512):
    t, d = x.shape
    w = jnp.zeros((d, ROUTE_LANES), F32).at[:, :N_GROUPS].set(wr_g).at[:, N_GROUPS:N_GROUPS + N_EXPERTS].set(wr_e)
    b = jnp.zeros((1, ROUTE_LANES), F32).at[0, :N_GROUPS].set(br_g).at[0, N_GROUPS:N_GROUPS + N_EXPERTS].set(br_e)
    w_hi = w.astype(BF16)
    w_lo = (w - w_hi.astype(F32)).astype(BF16)
    tm = min(tm, t)
    return pl.pallas_call(
        _router_kernel,
        grid=(t // tm,),
        in_specs=[pl.BlockSpec((tm, d), lambda i: (i, 0)),
                  pl.BlockSpec((1, d), lambda i: (0, 0)),
                  pl.BlockSpec((d, ROUTE_LANES), lambda i: (0, 0)),
                  pl.BlockSpec((d, ROUTE_LANES), lambda i: (0, 0)),
                  pl.BlockSpec((1, ROUTE_LANES), lambda i: (0, 0))],
        out_specs=[pl.BlockSpec((tm, d // 2), lambda i: (i, 0)),
                   pl.BlockSpec((tm, ROUTE_LANES), lambda i: (i, 0)),
                   pl.BlockSpec((1, ROUTE_LANES), lambda i: (0, 0))],
        out_shape=[jax.ShapeDtypeStruct((t, d // 2), jnp.uint32),
                   jax.ShapeDtypeStruct((t, ROUTE_LANES), F32),
                   jax.ShapeDtypeStruct((1, ROUTE_LANES), F32)],
        scratch_shapes=[pltpu.VMEM((1, ROUTE_LANES), F32)],
        compiler_params=_params("arbitrary"),
        name="router",
    )(x, norm_g.reshape(1, d), w_hi, w_lo, b)


def _lookup(table, idx):
    hit = idx[..., None] == jnp.arange(table.shape[0], dtype=jnp.int32)
    return jnp.sum(jnp.where(hit, table, 0), axis=-1)


def _dispatch_tables(route, counts, t):
    counts = counts[0, N_GROUPS:N_GROUPS + N_EXPERTS].astype(jnp.int32)
    padded = (counts + EXPERT_BLOCK - 1) // EXPERT_BLOCK * EXPERT_BLOCK
    padded_end = jnp.cumsum(padded)
    padded_start = padded_end - padded
    expert = route[:, 0:2].astype(jnp.int32)
    rank = route[:, 4:6].astype(jnp.int32)
    assign_slot = (_lookup(padded_start, expert) + rank).reshape(-1)

    n_slots = (-(-2 * t // EXPERT_BLOCK) + N_EXPERTS) * EXPERT_BLOCK
    n_items = N_EXPERTS + n_slots // MOE_MAX_ROWS
    chunks = (padded + MOE_MAX_ROWS - 1) // MOE_MAX_ROWS
    chunk_end = jnp.cumsum(chunks)
    n_used = chunk_end[-1]
    ids = jnp.arange(n_items, dtype=jnp.int32)
    item_e = jnp.minimum(jnp.sum(chunk_end[None, :] <= ids[:, None], axis=1), N_EXPERTS - 1).astype(jnp.int32)
    j = ids - _lookup(chunk_end - chunks, item_e)
    item_start = _lookup(padded_start, item_e) + j * MOE_MAX_ROWS
    item_rows = jnp.clip(_lookup(padded, item_e) - j * MOE_MAX_ROWS, 0, MOE_MAX_ROWS)
    used = ids < n_used
    last_e = jnp.sum(jnp.where(ids == n_used - 1, item_e, 0))
    item_e = jnp.where(used, item_e, last_e).astype(jnp.int32)
    item_start = jnp.where(used, item_start, 0).astype(jnp.int32)
    item_rows = jnp.where(used, item_rows, 0).astype(jnp.int32)
    total = padded_end[-1].reshape(1).astype(jnp.int32)
    return (assign_slot.astype(jnp.int32), padded_start.astype(jnp.int32), padded.astype(jnp.int32), total,
            item_e, item_start, item_rows, n_slots, n_items)


def _dispatch_kernel(as_ref, ps_ref, pd_ref, tot_ref, hp_ref, xs_hbm, zbuf, sem, zsem):
    i = pl.program_id(0)
    tm = hp_ref.shape[0]
    n_blocks = xs_hbm.shape[0] // EXPERT_BLOCK

    def zero_block(first_row):
        return pltpu.make_async_copy(zbuf, xs_hbm.at[pl.ds(pl.multiple_of(first_row, EXPERT_BLOCK), EXPERT_BLOCK)],
                                     zsem)

    def zero_fill(act):
        def per_expert(e, carry):
            @pl.when(pd_ref[e] > 0)
            def _():
                act(zero_block(ps_ref[e] + pd_ref[e] - EXPERT_BLOCK))
            return carry
        lax.fori_loop(0, N_EXPERTS, per_expert, 0)

        def per_tail_block(b, carry):
            act(zero_block(b * EXPERT_BLOCK))
            return carry
        lax.fori_loop(tot_ref[0] // EXPERT_BLOCK, n_blocks, per_tail_block, 0)

    @pl.when(i == 0)
    def _():
        zbuf[...] = jnp.zeros_like(zbuf)
        zero_fill(lambda cp: cp.start())
        zero_fill(lambda cp: cp.wait())

    def body(r, carry):
        a = 2 * (i * tm + r)
        for k in range(2):
            pltpu.make_async_copy(hp_ref.at[pl.ds(r, 1)], xs_hbm.at[pl.ds(as_ref[a + k], 1)], sem).start()
        return carry
    lax.fori_loop(0, tm, body, 0, unroll=8)
    for k in range(2):
        pltpu.make_async_copy(hp_ref, xs_hbm.at[pl.ds(0, tm)], sem).wait()


def _dispatch(hp, assign_slot, padded_start, padded, total, n_slots, tm=256):
    t, words = hp.shape
    tm = min(tm, t)
    return pl.pallas_call(
        _dispatch_kernel,
        grid_spec=pltpu.PrefetchScalarGridSpec(
            num_scalar_prefetch=4,
            grid=(t // tm,),
            in_specs=[pl.BlockSpec((tm, words), lambda i, *_: (i, 0))],
            out_specs=pl.BlockSpec(memory_space=pl.ANY),
            scratch_shapes=[pltpu.VMEM((EXPERT_BLOCK, words), hp.dtype),
                            pltpu.SemaphoreType.DMA(()),
                            pltpu.SemaphoreType.DMA(())],
        ),
        out_shape=jax.ShapeDtypeStruct((n_slots, words), hp.dtype),
        compiler_params=_params("arbitrary"),
        name="moe_dispatch",
    )(assign_slot, padded_start, padded, total, hp)


def _unpack_pairs(words):
    lo = lax.bitcast_convert_type(words << 16, F32)
    hi = lax.bitcast_convert_type(words & jnp.uint32(0xFFFF0000), F32)
    return lo, hi


def _pack_pairs(lo, hi):
    lo_bits = lax.bitcast_convert_type(lo.astype(BF16).astype(F32), jnp.uint32)
    hi_bits = lax.bitcast_convert_type(hi.astype(BF16).astype(F32), jnp.uint32)
    return (lo_bits >> 16) | (hi_bits & jnp.uint32(0xFFFF0000))


def _expert_items_kernel(ie_ref, is_ref, ir_ref, tot_ref, in_hbm, *refs, n_weights, cast_weights, compute):
    w_refs = refs[:n_weights]
    out_hbm = refs[n_weights]
    inbuf, outbuf, zbuf, w_bf, sem_in, sem_out, sem_z = refs[n_weights + 1:]
    i = pl.program_id(0)
    n_items = pl.num_programs(0)
    max_blocks = inbuf.shape[1] // EXPERT_BLOCK
    n_out_blocks = out_hbm.shape[0] // EXPERT_BLOCK

    def rows_of(hbm, first_row):
        return hbm.at[pl.ds(pl.multiple_of(first_row, EXPERT_BLOCK), EXPERT_BLOCK)]

    def for_blocks(item, act):
        for c in range(max_blocks):
            @pl.when(c * EXPERT_BLOCK < ir_ref[item])
            def _():
                act(c, is_ref[item] + c * EXPERT_BLOCK)

    def in_copy(buf, act):
        return lambda c, row: act(pltpu.make_async_copy(
            rows_of(in_hbm, row), inbuf.at[buf, pl.ds(c * EXPERT_BLOCK, EXPERT_BLOCK)], sem_in.at[buf]))

    def out_copy(buf, act):
        return lambda c, row: act(pltpu.make_async_copy(
            outbuf.at[buf, pl.ds(c * EXPERT_BLOCK, EXPERT_BLOCK)], rows_of(out_hbm, row), sem_out.at[buf]))

    start = lambda cp: cp.start()
    wait = lambda cp: cp.wait()
    buf = i % 2

    @pl.when(i == 0)
    def _():
        for_blocks(0, in_copy(0, start))
        zbuf[...] = jnp.zeros_like(zbuf)

        def tail(act):
            def body(b, carry):
                act(pltpu.make_async_copy(zbuf, rows_of(out_hbm, b * EXPERT_BLOCK), sem_z))
                return carry
            lax.fori_loop(tot_ref[0] // EXPERT_BLOCK, n_out_blocks, body, 0)
        tail(start)
        tail(wait)

    @pl.when(i + 1 < n_items)
    def _():
        for_blocks(i + 1, in_copy(1 - buf, start))

    for_blocks(i, in_copy(buf, wait))

    @pl.when(jnp.logical_or(i == 0, ie_ref[i] != ie_ref[jnp.maximum(i - 1, 0)]))
    def _():
        cast_weights(w_refs, w_bf)

    @pl.when(i >= 2)
    def _():
        for_blocks(i - 2, out_copy(buf, wait))

    n_blk = ir_ref[i] // EXPERT_BLOCK
    for c in range(1, max_blocks + 1):
        @pl.when(n_blk == c)
        def _():
            rows = c * EXPERT_BLOCK
            outbuf[buf, 0:rows, :] = compute(inbuf[buf, 0:rows, :], w_bf)

    for_blocks(i, out_copy(buf, start))

    @pl.when(i == n_items - 1)
    def _():
        @pl.when(i >= 1)
        def _():
            for_blocks(i - 1, out_copy(1 - buf, wait))
        for_blocks(i, out_copy(buf, wait))


def _expert_items_call(name, items, in_rows, weights, out_width, out_dtype, n_w_cols, cast_weights, compute):
    item_e, item_start, item_rows, total = items
    n_slots, in_width = in_rows.shape
    k_dim = weights[0].shape[1]
    w_spec = lambda w: pl.BlockSpec((None,) + w.shape[1:], lambda i, ie, *_: (ie[i], 0, 0))
    return pl.pallas_call(
        functools.partial(_expert_items_kernel, n_weights=len(weights), cast_weights=cast_weights, compute=compute),
        grid_spec=pltpu.PrefetchScalarGridSpec(
            num_scalar_prefetch=4,
            grid=(item_e.shape[0],),
            in_specs=[pl.BlockSpec(memory_space=pl.ANY)] + [w_spec(w) for w in weights],
            out_specs=pl.BlockSpec(memory_space=pl.ANY),
            scratch_shapes=[pltpu.VMEM((2, MOE_MAX_ROWS, in_width), in_rows.dtype),
                            pltpu.VMEM((2, MOE_MAX_ROWS, out_width), out_dtype),
                            pltpu.VMEM((EXPERT_BLOCK, out_width), out_dtype),
                            pltpu.VMEM((k_dim, n_w_cols), BF16),
                            pltpu.SemaphoreType.DMA((2,)),
                            pltpu.SemaphoreType.DMA((2,)),
                            pltpu.SemaphoreType.DMA(())],
        ),
        out_shape=jax.ShapeDtypeStruct((n_slots, out_width), out_dtype),
        compiler_params=pltpu.CompilerParams(dimension_semantics=("arbitrary",),
                                             vmem_limit_bytes=V7X_VMEM_LIMIT_MOE),
        name=name,
    )(item_e, item_start, item_rows, total, in_rows, *weights)


def _cast_gate_up(w_refs, w_bf):
    d_exp = w_refs[0].shape[1]
    w_bf[:, 0:d_exp] = w_refs[0][...].astype(BF16)
    w_bf[:, d_exp:2 * d_exp] = w_refs[1][...].astype(BF16)


def _gate_up(words, w_bf):
    half = words.shape[1]
    d_exp = w_bf.shape[1] // 2
    lo, hi = _unpack_pairs(words)
    gu = (jnp.dot(lo.astype(BF16), w_bf[0:half, :], preferred_element_type=F32)
          + jnp.dot(hi.astype(BF16), w_bf[half:2 * half, :], preferred_element_type=F32))
    gate, up = gu[:, 0:d_exp], gu[:, d_exp:2 * d_exp]
    return (gate * jax.nn.sigmoid(gate) * up).astype(BF16)


def _cast_down(w_refs, w_bf):
    w_bf[...] = w_refs[0][...].astype(BF16)


def _down(hid, w_bf):
    half = w_bf.shape[1] // 2
    y = jnp.dot(hid, w_bf[...], preferred_element_type=F32)
    return _pack_pairs(y[:, 0:half], y[:, half:2 * half])


def _combine_kernel(as_ref, x_ref, route_ref, g_ref, y_hbm, o_ref, *rest, emit_norm):
    if emit_norm:
        h_ref, ybuf, sem = rest
    else:
        ybuf, sem = rest
    i = pl.program_id(0)
    tm, d = x_ref.shape
    half = d // 2

    def gather_row(tile, r, buf):
        a = 2 * (tile * tm + r)
        for k in range(2):
            pltpu.make_async_copy(y_hbm.at[pl.ds(as_ref[a + k], 1)], ybuf.at[buf, k, pl.ds(r, 1)],
                                  sem.at[buf]).start()

    def wait_gather(buf):
        for k in range(2):
            pltpu.make_async_copy(y_hbm.at[pl.ds(0, tm)], ybuf.at[buf, k], sem.at[buf]).wait()

    @pl.when(i == 0)
    def _():
        def body(r, carry):
            gather_row(0, r, 0)
            return carry
        lax.fori_loop(0, tm, body, 0, unroll=8)

    buf = i % 2
    last = pl.num_programs(0) - 1
    next_tile = jnp.minimum(i + 1, last)
    wait_gather(buf)

    def row_chunk(c, carry):
        for rr in range(COMBINE_ROWS):
            gather_row(next_tile, c * COMBINE_ROWS + rr, 1 - buf)
        rows = pl.ds(pl.multiple_of(c * COMBINE_ROWS, COMBINE_ROWS), COMBINE_ROWS)
        w1 = route_ref[rows, 2:3]
        w2 = route_ref[rows, 3:4]
        sq = jnp.zeros((COMBINE_ROWS, 1), F32)
        for c0 in range(0, half, COMBINE_COLS):
            cols = slice(c0, c0 + COMBINE_COLS)
            up_cols = slice(half + c0, half + c0 + COMBINE_COLS)
            lo1, hi1 = _unpack_pairs(ybuf[buf, 0, rows, cols])
            lo2, hi2 = _unpack_pairs(ybuf[buf, 1, rows, cols])
            out_lo = x_ref[rows, cols] + w1 * lo1 + w2 * lo2
            out_hi = x_ref[rows, up_cols] + w1 * hi1 + w2 * hi2
            o_ref[rows, cols] = out_lo
            o_ref[rows, up_cols] = out_hi
            if emit_norm:
                sq = sq + jnp.sum(out_lo * out_lo + out_hi * out_hi, axis=-1, keepdims=True)
        if emit_norm:
            inv = lax.rsqrt(sq * (1.0 / d) + EPS)
            for c0 in range(0, d, COMBINE_COLS):
                cols = slice(c0, c0 + COMBINE_COLS)
                h_ref[rows, cols] = (o_ref[rows, cols] * inv * g_ref[:, cols]).astype(h_ref.dtype)
        return carry
    lax.fori_loop(0, tm // COMBINE_ROWS, row_chunk, 0, unroll=4)

    @pl.when(i == last)
    def _():
        wait_gather(1 - buf)


def _moe(x, norm_g, wr_g, br_g, wr_e, br_e, w_gate, w_up, w_down, next_norm_g=None, tm_combine=256):
    t, d = x.shape
    d_exp = w_gate.shape[2]
    hp, route, counts = _router(x, norm_g, wr_g, br_g, wr_e, br_e)
    (assign_slot, padded_start, padded, total, item_e, item_start, item_rows,
     n_slots, _) = _dispatch_tables(route, counts, t)
    items = (item_e, item_start, item_rows, total)
    xs = _dispatch(hp, assign_slot, padded_start, padded, total, n_slots)
    hid = _expert_items_call("moe_up", items, xs, (w_gate, w_up), d_exp, BF16, 2 * d_exp,
                             _cast_gate_up, _gate_up)
    y = _expert_items_call("moe_down", items, hid, (w_down,), d // 2, jnp.uint32, d, _cast_down, _down)

    tm = min(tm_combine, t)
    emit_norm = next_norm_g is not None
    row_tile = pl.BlockSpec((tm, d), lambda i, a: (i, 0))
    gain = next_norm_g if emit_norm else jnp.ones((d,), F32)
    out = pl.pallas_call(
        functools.partial(_combine_kernel, emit_norm=emit_norm),
        grid_spec=pltpu.PrefetchScalarGridSpec(
            num_scalar_prefetch=1,
            grid=(t // tm,),
            in_specs=[row_tile,
                      pl.BlockSpec((tm, ROUTE_LANES), lambda i, a: (i, 0)),
                      pl.BlockSpec((1, d), lambda i, a: (0, 0)),
                      pl.BlockSpec(memory_space=pl.ANY)],
            out_specs=[row_tile, row_tile] if emit_norm else row_tile,
            scratch_shapes=[pltpu.VMEM((2, 2, tm, d // 2), jnp.uint32),
                            pltpu.SemaphoreType.DMA((2,))],
        ),
        out_shape=([jax.ShapeDtypeStruct((t, d), F32), jax.ShapeDtypeStruct((t, d), BF16)] if emit_norm
                   else jax.ShapeDtypeStruct((t, d), F32)),
        compiler_params=_params("arbitrary"),
        name="moe_combine",
    )(assign_slot, x, route, gain.reshape(1, d).astype(F32), y)
    return tuple(out) if emit_norm else (out, None)


def _even_layer_mixer(x, mix_norm, w_in, a_ln_g, a_ln_b, a_ws, a_bs, b_q_norm, b_k_norm, b_sinks, w_out):
    n_groups = a_ws.shape[0]
    a_width = n_groups * HEAD_DIM
    n_heads = b_sinks.shape[0]
    n_kv = (w_in.shape[1] - 2 * a_width - n_heads * HEAD_DIM) // (2 * HEAD_DIM)
    h = _rmsnorm(x, mix_norm)
    proj = _matmul(h, w_in.astype(BF16), tm=1024, tn=1024)
    y_a = _gmlp(proj, a_ln_g, a_ln_b, a_ws, a_bs)
    y_b = _swa(proj, b_q_norm, b_k_norm, b_sinks, n_heads, n_kv, col0=2 * a_width)
    return _out_proj(x, y_a, y_b, w_out.astype(BF16))


def _odd_layer_mixer(x, h, w_in, c_conv_w, c_conv_b, c_ln_g, c_ln_b, w_out):
    c_width = c_conv_w.shape[1]
    n_heads = (w_in.shape[1] - 2 * c_width) // (3 * HEAD_DIM)
    proj = _matmul(h, w_in.astype(BF16), tm=1024, tn=1024)
    y_c = _conformer(proj, c_conv_w, c_conv_b, c_ln_g, c_ln_b)
    y_d = _stickbreak(proj, n_heads, col0=2 * c_width)
    return _out_proj(x, y_c, y_d, w_out.astype(BF16))


def kernel(x, l0_mix_norm, l0_w_in, l0_a_ln_g, l0_a_ln_b, l0_a_ws, l0_a_bs, l0_b_q_norm, l0_b_k_norm, l0_b_sinks, l0_w_out, l0_ffn_norm, l0_router_group_w, l0_router_group_b, l0_router_expert_w, l0_router_expert_b, l0_w_gate, l0_w_up, l0_w_down, l1_mix_norm, l1_w_in, l1_c_conv_w, l1_c_conv_b, l1_c_ln_g, l1_c_ln_b, l1_w_out, l1_ffn_norm, l1_router_group_w, l1_router_group_b, l1_router_expert_w, l1_router_expert_b, l1_w_gate, l1_w_up, l1_w_down):
    bsz, seq, d = x.shape
    xt = x.reshape(bsz * seq, d)
    assert bsz == 1, "the mixers index positions by row; one sequence per call"
    xt = _even_layer_mixer(xt, l0_mix_norm, l0_w_in, l0_a_ln_g, l0_a_ln_b, l0_a_ws, l0_a_bs,
                           l0_b_q_norm, l0_b_k_norm, l0_b_sinks, l0_w_out)
    xt, h1 = _moe(xt, l0_ffn_norm, l0_router_group_w, l0_router_group_b, l0_router_expert_w,
                  l0_router_expert_b, l0_w_gate, l0_w_up, l0_w_down, next_norm_g=l1_mix_norm)
    xt = _odd_layer_mixer(xt, h1, l1_w_in, l1_c_conv_w, l1_c_conv_b, l1_c_ln_g, l1_c_ln_b, l1_w_out)
    xt, _ = _moe(xt, l1_ffn_norm, l1_router_group_w, l1_router_group_b, l1_router_expert_w,
                 l1_router_expert_b, l1_w_gate, l1_w_up, l1_w_down)
    return xt.reshape(bsz, seq, d)
```

```python
import functools

import jax
import jax.numpy as jnp
from jax import lax
from jax.experimental import pallas as pl
from jax.experimental.pallas import tpu as pltpu

F32 = jnp.float32
BF16 = jnp.bfloat16

HEAD_DIM = 128
CHUNK = 128
EPS = 1e-6
N_GROUPS = 8
EXPERTS_PER_GROUP = 8
N_EXPERTS = N_GROUPS * EXPERTS_PER_GROUP
EXPERT_BLOCK = 128
CONV_WIDTH = 31
SUBLANES = 8
CONV_ROWS = 128
CONV_HALO = 32
ROUTE_LANES = 128
SB_LOG_ZERO = -104.0
SB_HEADS = 4
SB_KEYS = 384
DISPATCH_BATCH = 256
DISPATCH_LAG = 2
COMBINE_ROWS = 16
COMBINE_COLS = 512
MOE_MAX_ROWS = 768
V7X_VMEM_LIMIT = 56 * 1024 * 1024
V7X_VMEM_LIMIT_MOE = 60 * 1024 * 1024


def _params(*sem):
    return pltpu.CompilerParams(dimension_semantics=sem, vmem_limit_bytes=V7X_VMEM_LIMIT)


def _rmsnorm_kernel(x_ref, g_ref, o_ref):
    x = x_ref[...]
    ms = jnp.mean(x * x, axis=-1, keepdims=True)
    o_ref[...] = (x * lax.rsqrt(ms + EPS) * g_ref[...]).astype(o_ref.dtype)


def _rmsnorm(x, g, tm=256):
    t, d = x.shape
    return pl.pallas_call(
        _rmsnorm_kernel,
        grid=(t // tm,),
        in_specs=[pl.BlockSpec((tm, d), lambda i: (i, 0)),
                  pl.BlockSpec((1, d), lambda i: (0, 0))],
        out_specs=pl.BlockSpec((tm, d), lambda i: (i, 0)),
        out_shape=jax.ShapeDtypeStruct((t, d), BF16),
        compiler_params=_params("parallel"),
        name="rmsnorm",
    )(x, g.reshape(1, d))


def _matmul_kernel(a_ref, w_ref, o_ref):
    o_ref[...] = jnp.dot(a_ref[...], w_ref[...], preferred_element_type=F32).astype(o_ref.dtype)


def _matmul(a, w, tm, tn):
    m, k = a.shape
    n = w.shape[1]
    tm = min(tm, m)
    return pl.pallas_call(
        _matmul_kernel,
        grid=(m // tm, n // tn),
        in_specs=[pl.BlockSpec((tm, k), lambda i, j: (i, 0)),
                  pl.BlockSpec((k, tn), lambda i, j: (0, j))],
        out_specs=pl.BlockSpec((tm, tn), lambda i, j: (i, j)),
        out_shape=jax.ShapeDtypeStruct((m, n), BF16),
        compiler_params=_params("parallel", "arbitrary"),
        name="in_proj",
    )(a, w)


def _out_proj_kernel(x_ref, a_ref, b_ref, wa_ref, wb_ref, o_ref):
    acc = jnp.dot(a_ref[...], wa_ref[...], preferred_element_type=F32)
    acc = acc + jnp.dot(b_ref[...], wb_ref[...], preferred_element_type=F32)
    o_ref[...] = x_ref[...] + acc


def _out_proj(x, ya, yb, w, tm=1024, tn=1024):
    t, d = x.shape
    ka, kb = ya.shape[1], yb.shape[1]
    assert ka == kb and w.shape == (ka + kb, d)
    tm = min(tm, t)
    return pl.pallas_call(
        _out_proj_kernel,
        grid=(t // tm, d // tn),
        in_specs=[pl.BlockSpec((tm, tn), lambda i, j: (i, j)),
                  pl.BlockSpec((tm, ka), lambda i, j: (i, 0)),
                  pl.BlockSpec((tm, kb), lambda i, j: (i, 0)),
                  pl.BlockSpec((ka, tn), lambda i, j: (0, j)),
                  pl.BlockSpec((kb, tn), lambda i, j: (1, j))],
        out_specs=pl.BlockSpec((tm, tn), lambda i, j: (i, j)),
        out_shape=jax.ShapeDtypeStruct((t, d), F32),
        compiler_params=_params("parallel", "arbitrary"),
        name="out_proj",
    )(x, ya, yb, w, w)


def _layer_norm(x, g, b):
    mu = jnp.mean(x, axis=-1, keepdims=True)
    xc = x - mu
    var = jnp.mean(xc * xc, axis=-1, keepdims=True)
    return xc * lax.rsqrt(var + EPS) * g + b


def _gelu(x):
    return 0.5 * x * (1.0 + lax.erf(x * (2.0 ** -0.5)))


def _gmlp_kernel(u_ref, v_ref, lng_ref, lnb_ref, w_ref, bs_ref, o_ref):
    n_groups = w_ref.shape[0]
    v = _gelu(v_ref[...].astype(F32))
    vn = _layer_norm(v, lng_ref[...], lnb_ref[...]).astype(BF16)
    group_cols = [slice(g * HEAD_DIM, (g + 1) * HEAD_DIM) for g in range(n_groups)]
    mixed = [jnp.dot(w_ref[g], vn[:, cols], preferred_element_type=F32) for g, cols in enumerate(group_cols)]
    for g, cols in enumerate(group_cols):
        u = _gelu(u_ref[:, cols].astype(F32))
        o_ref[:, cols] = (u * (mixed[g] + bs_ref[:, cols])).astype(o_ref.dtype)


def _gmlp(proj, ln_g, ln_b, w_s, b_s):
    t = proj.shape[0]
    n_groups = w_s.shape[0]
    width = n_groups * HEAD_DIM
    causal = jnp.tril(jnp.ones((CHUNK, CHUNK), dtype=bool))
    w = jnp.where(causal[None], w_s, 0.0).astype(BF16)
    bs = jnp.repeat(b_s.T.astype(F32), HEAD_DIM, axis=1)
    return pl.pallas_call(
        _gmlp_kernel,
        grid=(t // CHUNK,),
        in_specs=[pl.BlockSpec((CHUNK, width), lambda i: (i, 0)),
                  pl.BlockSpec((CHUNK, width), lambda i: (i, 1)),
                  pl.BlockSpec((1, width), lambda i: (0, 0)),
                  pl.BlockSpec((1, width), lambda i: (0, 0)),
                  pl.BlockSpec((n_groups, CHUNK, CHUNK), lambda i: (0, 0, 0)),
                  pl.BlockSpec((CHUNK, width), lambda i: (0, 0))],
        out_specs=pl.BlockSpec((CHUNK, width), lambda i: (i, 0)),
        out_shape=jax.ShapeDtypeStruct((t, width), BF16),
        compiler_params=_params("parallel"),
        name="gmlp",
    )(proj, proj, ln_g.reshape(1, width), ln_b.reshape(1, width), w, bs)


def _swa_kernel(slopes_ref, sinks_ref, q_ref, kc_ref, kp_ref, vc_ref, vp_ref, qg_ref, kg_ref, o_ref, *, grp):
    n = pl.program_id(0)
    h = pl.program_id(1)
    k = jnp.concatenate([kp_ref[...], kc_ref[...]], axis=0).astype(F32)
    k = k * lax.rsqrt(jnp.mean(k * k, axis=-1, keepdims=True) + EPS) * kg_ref[...]
    kb = k.astype(BF16)
    vb = jnp.concatenate([vp_ref[...], vc_ref[...]], axis=0)
    qi = lax.broadcasted_iota(jnp.int32, (CHUNK, 2 * CHUNK), 0)
    kj = lax.broadcasted_iota(jnp.int32, (CHUNK, 2 * CHUNK), 1)
    dist = qi + CHUNK - kj
    valid = (dist >= 0) & (dist < CHUNK) & (n * CHUNK - CHUNK + kj >= 0)
    distf = dist.astype(F32)
    head_cols = [slice(g * HEAD_DIM, (g + 1) * HEAD_DIM) for g in range(grp)]
    scores = []
    for cols in head_cols:
        q = q_ref[:, cols].astype(F32)
        q = q * lax.rsqrt(jnp.mean(q * q, axis=-1, keepdims=True) + EPS) * qg_ref[...]
        scores.append(lax.dot_general(q.astype(BF16), kb, (((1,), (1,)), ((), ())),
                                      preferred_element_type=F32) * (HEAD_DIM ** -0.5))
    probs, denoms = [], []
    for g in range(grp):
        slope = slopes_ref[h * grp + g]
        sink = sinks_ref[h * grp + g]
        s = jnp.where(valid, scores[g] - slope * distf, -jnp.inf)
        m = jnp.maximum(jnp.max(s, axis=-1, keepdims=True), sink)
        p = jnp.exp(s - m)
        denoms.append(jnp.sum(p, axis=-1, keepdims=True) + jnp.exp(sink - m))
        probs.append(p.astype(BF16))
    for g in range(grp):
        o = jnp.dot(probs[g], vb, preferred_element_type=F32) / denoms[g]
        o_ref[:, head_cols[g]] = o.astype(o_ref.dtype)


def _swa(proj, q_gain, k_gain, sinks, n_heads, n_kv, col0):
    t = proj.shape[0]
    grp = n_heads // n_kv
    qb0 = col0 // (grp * HEAD_DIM)
    kb0 = (col0 + n_heads * HEAD_DIM) // HEAD_DIM
    vb0 = kb0 + n_kv
    slopes = 2.0 ** (-8.0 * jnp.arange(1, n_heads + 1, dtype=F32) / n_heads)
    prev = lambda n: jnp.maximum(n - 1, 0)
    return pl.pallas_call(
        functools.partial(_swa_kernel, grp=grp),
        grid_spec=pltpu.PrefetchScalarGridSpec(
            num_scalar_prefetch=2,
            grid=(t // CHUNK, n_kv),
            in_specs=[pl.BlockSpec((CHUNK, grp * HEAD_DIM), lambda n, h, *_: (n, qb0 + h)),
                      pl.BlockSpec((CHUNK, HEAD_DIM), lambda n, h, *_: (n, kb0 + h)),
                      pl.BlockSpec((CHUNK, HEAD_DIM), lambda n, h, *_: (prev(n), kb0 + h)),
                      pl.BlockSpec((CHUNK, HEAD_DIM), lambda n, h, *_: (n, vb0 + h)),
                      pl.BlockSpec((CHUNK, HEAD_DIM), lambda n, h, *_: (prev(n), vb0 + h)),
                      pl.BlockSpec((1, HEAD_DIM), lambda n, h, *_: (0, 0)),
                      pl.BlockSpec((1, HEAD_DIM), lambda n, h, *_: (0, 0))],
            out_specs=pl.BlockSpec((CHUNK, grp * HEAD_DIM), lambda n, h, *_: (n, h)),
        ),
        out_shape=jax.ShapeDtypeStruct((t, n_heads * HEAD_DIM), BF16),
        compiler_params=_params("parallel", "arbitrary"),
        name="swa",
    )(slopes, sinks.astype(F32), proj, proj, proj, proj, proj,
      q_gain.reshape(1, HEAD_DIM).astype(F32), k_gain.reshape(1, HEAD_DIM).astype(F32))


def _conformer_kernel(a_ref, g_ref, ap_ref, gp_ref, w_ref, cb_ref, lg_ref, lb_ref, o_ref, hbuf, ybuf):
    i = pl.program_id(0)
    tc, width = a_ref.shape
    span = CONV_HALO + tc
    hp = ap_ref[...].astype(F32) * jax.nn.sigmoid(gp_ref[...].astype(F32))
    hbuf[0, 0:CONV_HALO, :] = jnp.where(i > 0, hp, 0.0)
    hbuf[0, CONV_HALO:span, :] = a_ref[...].astype(F32) * jax.nn.sigmoid(g_ref[...].astype(F32))
    hbuf[0, span:span + SUBLANES, :] = jnp.zeros((SUBLANES, width), F32)
    for r in range(1, SUBLANES):
        hbuf[r, 0:span, :] = hbuf[0, r:r + span, :]
    base = CONV_HALO - (CONV_WIDTH - 1)
    for c in range(width // HEAD_DIM):
        cols = slice(c * HEAD_DIM, (c + 1) * HEAD_DIM)
        for t0 in range(0, tc, CONV_ROWS):
            acc = jnp.zeros((CONV_ROWS, HEAD_DIM), F32)
            for r in range(SUBLANES):
                taps = [j for j in range(CONV_WIDTH) if (base + j) % SUBLANES == r]
                first = t0 + base + taps[0] - r
                rows = hbuf[r, first:t0 + base + taps[-1] - r + CONV_ROWS, cols]
                for j in taps:
                    off = t0 + base + j - r - first
                    acc = acc + w_ref[j:j + 1, cols] * rows[off:off + CONV_ROWS, :]
            ybuf[t0:t0 + CONV_ROWS, cols] = acc
    y = _layer_norm(ybuf[...] + cb_ref[...], lg_ref[...], lb_ref[...])
    o_ref[...] = (y * jax.nn.sigmoid(y)).astype(o_ref.dtype)


def _conformer(proj, conv_w, conv_b, ln_g, ln_b, tc=256):
    t = proj.shape[0]
    width = conv_w.shape[1]
    tc = min(tc, t)
    halo_blocks = tc // CONV_HALO
    prev = lambda i: jnp.maximum(i * halo_blocks - 1, 0)
    vec = lambda: pl.BlockSpec((1, width), lambda i: (0, 0))
    return pl.pallas_call(
        _conformer_kernel,
        grid=(t // tc,),
        in_specs=[pl.BlockSpec((tc, width), lambda i: (i, 0)),
                  pl.BlockSpec((tc, width), lambda i: (i, 1)),
                  pl.BlockSpec((CONV_HALO, width), lambda i: (prev(i), 0)),
                  pl.BlockSpec((CONV_HALO, width), lambda i: (prev(i), 1)),
                  pl.BlockSpec((CONV_WIDTH, width), lambda i: (0, 0)),
                  vec(), vec(), vec()],
        out_specs=pl.BlockSpec((tc, width), lambda i: (i, 0)),
        out_shape=jax.ShapeDtypeStruct((t, width), BF16),
        scratch_shapes=[pltpu.VMEM((SUBLANES, CONV_HALO + tc + SUBLANES, width), F32),
                        pltpu.VMEM((tc, width), F32)],
        compiler_params=_params("parallel"),
        name="conformer",
    )(proj, proj, proj, proj, conv_w.astype(F32), conv_b.reshape(1, width).astype(F32),
      ln_g.reshape(1, width).astype(F32), ln_b.reshape(1, width).astype(F32))


def _stickbreak_kernel(q_ref, k_ref, v_ref, tri_ref, o_ref, *, heads):
    i = pl.program_id(1)
    tq = q_ref.shape[0]
    col = lax.broadcasted_iota(jnp.int32, (tq, SB_KEYS), 1)
    q_pos = i * tq + lax.broadcasted_iota(jnp.int32, (tq, 1), 0)

    def walk_tile(upper, cs, accs):
        start = pl.multiple_of(jnp.maximum(upper - SB_KEYS, 0), tq)
        valid = col < jnp.minimum(q_pos, upper) - start
        head_cols = [slice(g * HEAD_DIM, (g + 1) * HEAD_DIM) for g in range(heads)]
        zs = [lax.dot_general(q_ref[:, cols], k_ref[pl.ds(start, SB_KEYS), cols], (((1,), (1,)), ((), ())),
                              preferred_element_type=F32) * (HEAD_DIM ** -0.5) for cols in head_cols]
        log_wins, splits, new_cs = [], [], []
        for g in range(heads):
            z = zs[g]
            softplus = jnp.maximum(z, 0.0) + jnp.log(1.0 + jnp.exp(-jnp.abs(z)))
            log_fail = jnp.where(valid, -softplus, 0.0)
            hi = log_fail.astype(BF16)
            lo = (log_fail - hi.astype(F32)).astype(BF16)
            splits.append(jnp.concatenate([hi, lo], axis=1))
            log_wins.append(z - softplus)
            new_cs.append(cs[g] + jnp.sum(log_fail, axis=-1, keepdims=True))
        later = jnp.dot(jnp.concatenate(splits, axis=0), tri_ref[...], preferred_element_type=F32)
        new_accs = []
        for g in range(heads):
            a = jnp.where(valid, jnp.exp(log_wins[g] + later[g * tq:(g + 1) * tq, :] + cs[g]), 0.0)
            new_accs.append(accs[g] + jnp.dot(a.astype(BF16), v_ref[pl.ds(start, SB_KEYS), head_cols[g]],
                                              preferred_element_type=F32))
        return start, new_cs, new_accs

    def cond(state):
        upper, cs, _ = state
        alive = functools.reduce(jnp.maximum, [jnp.max(c) for c in cs])
        return jnp.logical_and(upper > 0, alive > SB_LOG_ZERO)

    def body(state):
        return walk_tile(*state)

    init = ((i + 1) * tq,
            [jnp.zeros((tq, 1), F32) for _ in range(heads)],
            [jnp.zeros((tq, HEAD_DIM), F32) for _ in range(heads)])
    _, _, accs = lax.while_loop(cond, body, init)
    for g in range(heads):
        o_ref[:, g * HEAD_DIM:(g + 1) * HEAD_DIM] = accs[g].astype(o_ref.dtype)


def _stickbreak(proj, n_heads, col0):
    t = proj.shape[0]
    width = SB_HEADS * HEAD_DIM
    qb0 = col0 // width
    kb0 = qb0 + n_heads // SB_HEADS
    vb0 = kb0 + n_heads // SB_HEADS
    s_later = jnp.arange(2 * SB_KEYS)[:, None] % SB_KEYS
    tri = (s_later > jnp.arange(SB_KEYS)[None, :]).astype(BF16)
    resident = lambda c0: pl.BlockSpec((t, width), lambda h, i: (0, c0 + h), pipeline_mode=pl.Buffered(1))
    return pl.pallas_call(
        functools.partial(_stickbreak_kernel, heads=SB_HEADS),
        grid=(n_heads // SB_HEADS, t // CHUNK),
        in_specs=[pl.BlockSpec((CHUNK, width), lambda h, i: (i, qb0 + h)),
                  resident(kb0), resident(vb0),
                  pl.BlockSpec((2 * SB_KEYS, SB_KEYS), lambda h, i: (0, 0))],
        out_specs=pl.BlockSpec((CHUNK, width), lambda h, i: (i, h)),
        out_shape=jax.ShapeDtypeStruct((t, n_heads * HEAD_DIM), BF16),
        compiler_params=_params("parallel", "arbitrary"),
        name="stickbreak",
    )(proj, proj, proj, tri)


def _router_kernel(x_ref, g_ref, whi_ref, wlo_ref, b_ref, hp_ref, route_ref, counts_ref, carry):
    @pl.when(pl.program_id(0) == 0)
    def _():
        carry[...] = jnp.zeros_like(carry)

    x = x_ref[...]
    tm, d = x.shape
    h = x * lax.rsqrt(jnp.mean(x * x, axis=-1, keepdims=True) + EPS) * g_ref[...]
    h_hi = h.astype(BF16)
    h_hif = h_hi.astype(F32)
    h_lo = (h - h_hif).astype(BF16)
    logits = (jnp.dot(h_hi, whi_ref[...], preferred_element_type=F32)
              + jnp.dot(h_lo, whi_ref[...], preferred_element_type=F32)
              + jnp.dot(h_hi, wlo_ref[...], preferred_element_type=F32)) + b_ref[...]
    lane = lax.broadcasted_iota(jnp.int32, logits.shape, 1)
    neg = -jnp.inf

    def first_max(vals):
        m = jnp.max(vals, axis=-1, keepdims=True)
        idx = jnp.min(jnp.where(vals == m, lane, ROUTE_LANES), axis=-1, keepdims=True)
        return m, idx

    gl = jnp.where(lane < N_GROUPS, logits, neg)
    gmax, g_sel = first_max(gl)
    g_weight = 1.0 / jnp.sum(jnp.exp(gl - gmax), axis=-1, keepdims=True)
    expert = lane - N_GROUPS
    in_group = (expert >= 0) & (expert < N_EXPERTS) & ((expert >> 3) == g_sel)
    el = jnp.where(in_group, logits, neg)
    m1, i1 = first_max(el)
    m2, i2 = first_max(jnp.where(lane == i1, neg, el))
    e21 = jnp.exp(m2 - m1)
    w1 = g_weight / (1.0 + e21)
    w2 = w1 * e21
    sel1 = lane == i1
    sel2 = lane == i2
    chosen = jnp.where(jnp.logical_or(sel1, sel2), 1.0, 0.0)
    row = lax.broadcasted_iota(jnp.int32, (tm, tm), 0)
    col = lax.broadcasted_iota(jnp.int32, (tm, tm), 1)
    earlier = jnp.where(col < row, 1.0, 0.0).astype(BF16)
    before = jnp.dot(earlier, chosen.astype(BF16), preferred_element_type=F32) + carry[...]
    rank1 = jnp.sum(jnp.where(sel1, before, 0.0), axis=-1, keepdims=True)
    rank2 = jnp.sum(jnp.where(sel2, before, 0.0), axis=-1, keepdims=True)
    carry[...] = carry[...] + jnp.sum(chosen, axis=0, keepdims=True)
    counts_ref[...] = carry[...]
    fields = ((i1 - N_GROUPS).astype(F32), (i2 - N_GROUPS).astype(F32), w1, w2, rank1, rank2)
    rec = jnp.zeros_like(logits)
    for k, val in enumerate(fields):
        rec = jnp.where(lane == k, val, rec)
    route_ref[...] = rec
    bits = lax.bitcast_convert_type(h_hif, jnp.uint32)
    hp_ref[...] = (bits[:, :d // 2] >> 16) | (bits[:, d // 2:] & jnp.uint32(0xFFFF0000))


def _router(x, norm_g, wr_g, br_g, wr_e, br_e, tm=512):
    t, d = x.shape
    pad = ROUTE_LANES - N_GROUPS - N_EXPERTS
    w = jnp.concatenate([wr_g.astype(F32), wr_e.astype(F32), jnp.zeros((d, pad), F32)], axis=1)
    b = jnp.concatenate([br_g.astype(F32), br_e.astype(F32), jnp.zeros((pad,), F32)]).reshape(1, ROUTE_LANES)
    w_hi = w.astype(BF16)
    w_lo = (w - w_hi.astype(F32)).astype(BF16)
    tm = min(tm, t)
    return pl.pallas_call(
        _router_kernel,
        grid=(t // tm,),
        in_specs=[pl.BlockSpec((tm, d), lambda i: (i, 0)),
                  pl.BlockSpec((1, d), lambda i: (0, 0)),
                  pl.BlockSpec((d, ROUTE_LANES), lambda i: (0, 0)),
                  pl.BlockSpec((d, ROUTE_LANES), lambda i: (0, 0)),
                  pl.BlockSpec((1, ROUTE_LANES), lambda i: (0, 0))],
        out_specs=[pl.BlockSpec((tm, d // 2), lambda i: (i, 0)),
                   pl.BlockSpec((tm, ROUTE_LANES), lambda i: (i, 0)),
                   pl.BlockSpec((1, ROUTE_LANES), lambda i: (0, 0))],
        out_shape=[jax.ShapeDtypeStruct((t, d // 2), jnp.uint32),
                   jax.ShapeDtypeStruct((t, ROUTE_LANES), F32),
                   jax.ShapeDtypeStruct((1, ROUTE_LANES), F32)],
        scratch_shapes=[pltpu.VMEM((1, ROUTE_LANES), F32)],
        compiler_params=_params("arbitrary"),
        name="router",
    )(x, norm_g.reshape(1, d), w_hi, w_lo, b)


def _lookup(table, idx):
    hit = idx[..., None] == jnp.arange(table.shape[0], dtype=jnp.int32)
    return jnp.sum(jnp.where(hit, table, 0), axis=-1)


def _dispatch_tables(route, counts, t):
    counts = counts[0, N_GROUPS:N_GROUPS + N_EXPERTS].astype(jnp.int32)
    padded = (counts + EXPERT_BLOCK - 1) // EXPERT_BLOCK * EXPERT_BLOCK
    padded_end = jnp.cumsum(padded)
    padded_start = padded_end - padded
    expert = route[:, 0:2].astype(jnp.int32)
    rank = route[:, 4:6].astype(jnp.int32)
    assign_slot = (_lookup(padded_start, expert) + rank).reshape(-1)

    n_slots = (-(-2 * t // EXPERT_BLOCK) + N_EXPERTS) * EXPERT_BLOCK
    n_items = N_EXPERTS + n_slots // MOE_MAX_ROWS
    chunks = (padded + MOE_MAX_ROWS - 1) // MOE_MAX_ROWS
    chunk_end = jnp.cumsum(chunks)
    n_used = chunk_end[-1]
    ids = jnp.arange(n_items, dtype=jnp.int32)
    item_e = jnp.minimum(jnp.sum(chunk_end[None, :] <= ids[:, None], axis=1), N_EXPERTS - 1).astype(jnp.int32)
    j = ids - _lookup(chunk_end - chunks, item_e)
    item_start = _lookup(padded_start, item_e) + j * MOE_MAX_ROWS
    item_rows = jnp.clip(_lookup(padded, item_e) - j * MOE_MAX_ROWS, 0, MOE_MAX_ROWS)
    used = ids < n_used
    last_e = jnp.sum(jnp.where(ids == n_used - 1, item_e, 0))
    item_e = jnp.where(used, item_e, last_e).astype(jnp.int32)
    item_start = jnp.where(used, item_start, 0).astype(jnp.int32)
    item_rows = jnp.where(used, item_rows, 0).astype(jnp.int32)
    total = padded_end[-1].reshape(1).astype(jnp.int32)
    return (assign_slot.astype(jnp.int32), padded_start.astype(jnp.int32), padded.astype(jnp.int32), total,
            item_e, item_start, item_rows, n_slots, n_items)


def _dispatch_kernel(as_ref, ps_ref, pd_ref, tot_ref, hp_hbm, xs_hbm, zbuf, sem, zsem):
    n_blocks = xs_hbm.shape[0] // EXPERT_BLOCK
    batch = min(DISPATCH_BATCH, hp_hbm.shape[0])
    n_batches = hp_hbm.shape[0] // batch

    def zero_block(first_row):
        return pltpu.make_async_copy(zbuf, xs_hbm.at[pl.ds(pl.multiple_of(first_row, EXPERT_BLOCK), EXPERT_BLOCK)],
                                     zsem)

    def zero_fill(act):
        def per_expert(e, carry):
            @pl.when(pd_ref[e] > 0)
            def _():
                act(zero_block(ps_ref[e] + pd_ref[e] - EXPERT_BLOCK))
            return carry
        lax.fori_loop(0, N_EXPERTS, per_expert, 0)

        def per_tail_block(b, carry):
            act(zero_block(b * EXPERT_BLOCK))
            return carry
        lax.fori_loop(tot_ref[0] // EXPERT_BLOCK, n_blocks, per_tail_block, 0)

    zbuf[...] = jnp.zeros_like(zbuf)
    zero_fill(lambda cp: cp.start())
    zero_fill(lambda cp: cp.wait())

    def wait_batch():
        for k in range(2):
            pltpu.make_async_copy(hp_hbm.at[pl.ds(0, batch)], xs_hbm.at[pl.ds(0, batch)], sem).wait()

    def per_batch(b, carry):
        def body(r, carry):
            tok = b * batch + r
            for k in range(2):
                pltpu.make_async_copy(hp_hbm.at[pl.ds(tok, 1)], xs_hbm.at[pl.ds(as_ref[2 * tok + k], 1)],
                                      sem).start()
            return carry
        lax.fori_loop(0, batch, body, 0, unroll=8)

        @pl.when(b >= DISPATCH_LAG)
        def _():
            wait_batch()
        return carry
    lax.fori_loop(0, n_batches, per_batch, 0)
    for _ in range(min(DISPATCH_LAG, n_batches)):
        wait_batch()


def _dispatch(hp, assign_slot, padded_start, padded, total, n_slots):
    t, words = hp.shape
    return pl.pallas_call(
        _dispatch_kernel,
        grid_spec=pltpu.PrefetchScalarGridSpec(
            num_scalar_prefetch=4,
            grid=(1,),
            in_specs=[pl.BlockSpec(memory_space=pl.ANY)],
            out_specs=pl.BlockSpec(memory_space=pl.ANY),
            scratch_shapes=[pltpu.VMEM((EXPERT_BLOCK, words), hp.dtype),
                            pltpu.SemaphoreType.DMA(()),
                            pltpu.SemaphoreType.DMA(())],
        ),
        out_shape=jax.ShapeDtypeStruct((n_slots, words), hp.dtype),
        compiler_params=_params("arbitrary"),
        name="moe_dispatch",
    )(assign_slot, padded_start, padded, total, hp)


def _unpack_pairs(words):
    lo = lax.bitcast_convert_type(words << 16, F32)
    hi = lax.bitcast_convert_type(words & jnp.uint32(0xFFFF0000), F32)
    return lo, hi


def _pack_pairs(lo, hi):
    lo_bits = lax.bitcast_convert_type(lo.astype(BF16).astype(F32), jnp.uint32)
    hi_bits = lax.bitcast_convert_type(hi.astype(BF16).astype(F32), jnp.uint32)
    return (lo_bits >> 16) | (hi_bits & jnp.uint32(0xFFFF0000))


def _expert_items_kernel(ie_ref, is_ref, ir_ref, tot_ref, in_hbm, *refs, n_weights, cast_weights, compute):
    w_refs = refs[:n_weights]
    out_hbm = refs[n_weights]
    inbuf, outbuf, zbuf, w_bf, sem_in, sem_out, sem_z = refs[n_weights + 1:]
    i = pl.program_id(0)
    n_items = pl.num_programs(0)
    max_blocks = inbuf.shape[1] // EXPERT_BLOCK
    n_out_blocks = out_hbm.shape[0] // EXPERT_BLOCK

    def rows_of(hbm, first_row):
        return hbm.at[pl.ds(pl.multiple_of(first_row, EXPERT_BLOCK), EXPERT_BLOCK)]

    def for_blocks(item, act):
        for c in range(max_blocks):
            @pl.when(c * EXPERT_BLOCK < ir_ref[item])
            def _():
                act(c, is_ref[item] + c * EXPERT_BLOCK)

    def in_copy(buf, act):
        return lambda c, row: act(pltpu.make_async_copy(
            rows_of(in_hbm, row), inbuf.at[buf, pl.ds(c * EXPERT_BLOCK, EXPERT_BLOCK)], sem_in.at[buf]))

    def out_copy(buf, act):
        return lambda c, row: act(pltpu.make_async_copy(
            outbuf.at[buf, pl.ds(c * EXPERT_BLOCK, EXPERT_BLOCK)], rows_of(out_hbm, row), sem_out.at[buf]))

    start = lambda cp: cp.start()
    wait = lambda cp: cp.wait()
    buf = i % 2

    @pl.when(i == 0)
    def _():
        for_blocks(0, in_copy(0, start))
        zbuf[...] = jnp.zeros_like(zbuf)

        def tail(act):
            def body(b, carry):
                act(pltpu.make_async_copy(zbuf, rows_of(out_hbm, b * EXPERT_BLOCK), sem_z))
                return carry
            lax.fori_loop(tot_ref[0] // EXPERT_BLOCK, n_out_blocks, body, 0)
        tail(start)
        tail(wait)

    @pl.when(i + 1 < n_items)
    def _():
        for_blocks(i + 1, in_copy(1 - buf, start))

    for_blocks(i, in_copy(buf, wait))

    @pl.when(jnp.logical_or(i == 0, ie_ref[i] != ie_ref[jnp.maximum(i - 1, 0)]))
    def _():
        cast_weights(w_refs, w_bf)

    @pl.when(i >= 2)
    def _():
        for_blocks(i - 2, out_copy(buf, wait))

    n_blk = ir_ref[i] // EXPERT_BLOCK
    for c in range(1, max_blocks + 1):
        @pl.when(n_blk == c)
        def _():
            rows = c * EXPERT_BLOCK
            outbuf[buf, 0:rows, :] = compute(inbuf[buf, 0:rows, :], w_bf)

    for_blocks(i, out_copy(buf, start))

    @pl.when(i == n_items - 1)
    def _():
        @pl.when(i >= 1)
        def _():
            for_blocks(i - 1, out_copy(1 - buf, wait))
        for_blocks(i, out_copy(buf, wait))


def _expert_items_call(name, items, in_rows, weights, out_width, out_dtype, n_w_cols, cast_weights, compute):
    item_e, item_start, item_rows, total = items
    n_slots, in_width = in_rows.shape
    k_dim = weights[0].shape[1]
    w_spec = lambda w: pl.BlockSpec((None,) + w.shape[1:], lambda i, ie, *_: (ie[i], 0, 0))
    return pl.pallas_call(
        functools.partial(_expert_items_kernel, n_weights=len(weights), cast_weights=cast_weights, compute=compute),
        grid_spec=pltpu.PrefetchScalarGridSpec(
            num_scalar_prefetch=4,
            grid=(item_e.shape[0],),
            in_specs=[pl.BlockSpec(memory_space=pl.ANY)] + [w_spec(w) for w in weights],
            out_specs=pl.BlockSpec(memory_space=pl.ANY),
            scratch_shapes=[pltpu.VMEM((2, MOE_MAX_ROWS, in_width), in_rows.dtype),
                            pltpu.VMEM((2, MOE_MAX_ROWS, out_width), out_dtype),
                            pltpu.VMEM((EXPERT_BLOCK, out_width), out_dtype),
                            pltpu.VMEM((k_dim, n_w_cols), BF16),
                            pltpu.SemaphoreType.DMA((2,)),
                            pltpu.SemaphoreType.DMA((2,)),
                            pltpu.SemaphoreType.DMA(())],
        ),
        out_shape=jax.ShapeDtypeStruct((n_slots, out_width), out_dtype),
        compiler_params=pltpu.CompilerParams(dimension_semantics=("arbitrary",),
                                             vmem_limit_bytes=V7X_VMEM_LIMIT_MOE),
        name=name,
    )(item_e, item_start, item_rows, total, in_rows, *weights)


def _cast_gate_up(w_refs, w_bf):
    d_exp = w_refs[0].shape[1]
    w_bf[:, 0:d_exp] = w_refs[0][...].astype(BF16)
    w_bf[:, d_exp:2 * d_exp] = w_refs[1][...].astype(BF16)


def _gate_up(words, w_bf):
    half = words.shape[1]
    d_exp = w_bf.shape[1] // 2
    lo, hi = _unpack_pairs(words)
    gu = (jnp.dot(lo.astype(BF16), w_bf[0:half, :], preferred_element_type=F32)
          + jnp.dot(hi.astype(BF16), w_bf[half:2 * half, :], preferred_element_type=F32))
    gate, up = gu[:, 0:d_exp], gu[:, d_exp:2 * d_exp]
    return (gate * jax.nn.sigmoid(gate) * up).astype(BF16)


def _cast_down(w_refs, w_bf):
    w_bf[...] = w_refs[0][...].astype(BF16)


def _down(hid, w_bf):
    half = w_bf.shape[1] // 2
    y = jnp.dot(hid, w_bf[...], preferred_element_type=F32)
    return _pack_pairs(y[:, 0:half], y[:, half:2 * half])


def _combine_kernel(as_ref, x_ref, route_ref, g_ref, y_hbm, o_ref, *rest, emit_norm):
    if emit_norm:
        h_ref, ybuf, sem = rest
    else:
        ybuf, sem = rest
    i = pl.program_id(0)
    tm, d = x_ref.shape
    half = d // 2

    def gather_row(tile, r, buf):
        a = 2 * (tile * tm + r)
        for k in range(2):
            pltpu.make_async_copy(y_hbm.at[pl.ds(as_ref[a + k], 1)], ybuf.at[buf, k, pl.ds(r, 1)],
                                  sem.at[buf]).start()

    def wait_gather(buf):
        for k in range(2):
            pltpu.make_async_copy(y_hbm.at[pl.ds(0, tm)], ybuf.at[buf, k], sem.at[buf]).wait()

    @pl.when(i == 0)
    def _():
        def body(r, carry):
            gather_row(0, r, 0)
            return carry
        lax.fori_loop(0, tm, body, 0, unroll=8)

    buf = i % 2
    last = pl.num_programs(0) - 1
    next_tile = jnp.minimum(i + 1, last)
    wait_gather(buf)

    def row_chunk(c, carry):
        for rr in range(COMBINE_ROWS):
            gather_row(next_tile, c * COMBINE_ROWS + rr, 1 - buf)
        rows = pl.ds(pl.multiple_of(c * COMBINE_ROWS, COMBINE_ROWS), COMBINE_ROWS)
        w1 = route_ref[rows, 2:3]
        w2 = route_ref[rows, 3:4]
        sq = jnp.zeros((COMBINE_ROWS, 1), F32)
        for c0 in range(0, half, COMBINE_COLS):
            cols = slice(c0, c0 + COMBINE_COLS)
            up_cols = slice(half + c0, half + c0 + COMBINE_COLS)
            lo1, hi1 = _unpack_pairs(ybuf[buf, 0, rows, cols])
            lo2, hi2 = _unpack_pairs(ybuf[buf, 1, rows, cols])
            out_lo = x_ref[rows, cols] + w1 * lo1 + w2 * lo2
            out_hi = x_ref[rows, up_cols] + w1 * hi1 + w2 * hi2
            o_ref[rows, cols] = out_lo
            o_ref[rows, up_cols] = out_hi
            if emit_norm:
                sq = sq + jnp.sum(out_lo * out_lo + out_hi * out_hi, axis=-1, keepdims=True)
        if emit_norm:
            inv = lax.rsqrt(sq * (1.0 / d) + EPS)
            for c0 in range(0, d, COMBINE_COLS):
                cols = slice(c0, c0 + COMBINE_COLS)
                h_ref[rows, cols] = (o_ref[rows, cols] * inv * g_ref[:, cols]).astype(h_ref.dtype)
        return carry
    lax.fori_loop(0, tm // COMBINE_ROWS, row_chunk, 0, unroll=4)

    @pl.when(i == last)
    def _():
        wait_gather(1 - buf)


def _moe(x, norm_g, wr_g, br_g, wr_e, br_e, w_gate, w_up, w_down, next_norm_g=None, tm_combine=256):
    t, d = x.shape
    d_exp = w_gate.shape[2]
    hp, route, counts = _router(x, norm_g, wr_g, br_g, wr_e, br_e)
    (assign_slot, padded_start, padded, total, item_e, item_start, item_rows,
     n_slots, _) = _dispatch_tables(route, counts, t)
    items = (item_e, item_start, item_rows, total)
    xs = _dispatch(hp, assign_slot, padded_start, padded, total, n_slots)
    hid = _expert_items_call("moe_up", items, xs, (w_gate, w_up), d_exp, BF16, 2 * d_exp,
                             _cast_gate_up, _gate_up)
    y = _expert_items_call("moe_down", items, hid, (w_down,), d // 2, jnp.uint32, d, _cast_down, _down)

    tm = min(tm_combine, t)
    emit_norm = next_norm_g is not None
    row_tile = pl.BlockSpec((tm, d), lambda i, a: (i, 0))
    gain = next_norm_g if emit_norm else jnp.ones((d,), F32)
    out = pl.pallas_call(
        functools.partial(_combine_kernel, emit_norm=emit_norm),
        grid_spec=pltpu.PrefetchScalarGridSpec(
            num_scalar_prefetch=1,
            grid=(t // tm,),
            in_specs=[row_tile,
                      pl.BlockSpec((tm, ROUTE_LANES), lambda i, a: (i, 0)),
                      pl.BlockSpec((1, d), lambda i, a: (0, 0)),
                      pl.BlockSpec(memory_space=pl.ANY)],
            out_specs=[row_tile, row_tile] if emit_norm else row_tile,
            scratch_shapes=[pltpu.VMEM((2, 2, tm, d // 2), jnp.uint32),
                            pltpu.SemaphoreType.DMA((2,))],
        ),
        out_shape=([jax.ShapeDtypeStruct((t, d), F32), jax.ShapeDtypeStruct((t, d), BF16)] if emit_norm
                   else jax.ShapeDtypeStruct((t, d), F32)),
        compiler_params=_params("arbitrary"),
        name="moe_combine",
    )(assign_slot, x, route, gain.reshape(1, d).astype(F32), y)
    return tuple(out) if emit_norm else (out, None)


def _even_layer_mixer(x, mix_norm, w_in, a_ln_g, a_ln_b, a_ws, a_bs, b_q_norm, b_k_norm, b_sinks, w_out):
    n_groups = a_ws.shape[0]
    a_width = n_groups * HEAD_DIM
    n_heads = b_sinks.shape[0]
    n_kv = (w_in.shape[1] - 2 * a_width - n_heads * HEAD_DIM) // (2 * HEAD_DIM)
    h = _rmsnorm(x, mix_norm)
    proj = _matmul(h, w_in.astype(BF16), tm=1024, tn=1024)
    y_a = _gmlp(proj, a_ln_g, a_ln_b, a_ws, a_bs)
    y_b = _swa(proj, b_q_norm, b_k_norm, b_sinks, n_heads, n_kv, col0=2 * a_width)
    return _out_proj(x, y_a, y_b, w_out.astype(BF16))


def _odd_layer_mixer(x, h, w_in, c_conv_w, c_conv_b, c_ln_g, c_ln_b, w_out):
    c_width = c_conv_w.shape[1]
    n_heads = (w_in.shape[1] - 2 * c_width) // (3 * HEAD_DIM)
    proj = _matmul(h, w_in.astype(BF16), tm=1024, tn=1024)
    y_c = _conformer(proj, c_conv_w, c_conv_b, c_ln_g, c_ln_b)
    y_d = _stickbreak(proj, n_heads, col0=2 * c_width)
    return _out_proj(x, y_c, y_d, w_out.astype(BF16))


def kernel(x, l0_mix_norm, l0_w_in, l0_a_ln_g, l0_a_ln_b, l0_a_ws, l0_a_bs, l0_b_q_norm, l0_b_k_norm, l0_b_sinks, l0_w_out, l0_ffn_norm, l0_router_group_w, l0_router_group_b, l0_router_expert_w, l0_router_expert_b, l0_w_gate, l0_w_up, l0_w_down, l1_mix_norm, l1_w_in, l1_c_conv_w, l1_c_conv_b, l1_c_ln_g, l1_c_ln_b, l1_w_out, l1_ffn_norm, l1_router_group_w, l1_router_group_b, l1_router_expert_w, l1_router_expert_b, l1_w_gate, l1_w_up, l1_w_down):
    bsz, seq, d = x.shape
    xt = x.reshape(bsz * seq, d)
    assert bsz == 1, "the mixers index positions by row; one sequence per call"
    xt = _even_layer_mixer(xt, l0_mix_norm, l0_w_in, l0_a_ln_g, l0_a_ln_b, l0_a_ws, l0_a_bs,
                           l0_b_q_norm, l0_b_k_norm, l0_b_sinks, l0_w_out)
    xt, h1 = _moe(xt, l0_ffn_norm, l0_router_group_w, l0_router_group_b, l0_router_expert_w,
                  l0_router_expert_b, l0_w_gate, l0_w_up, l0_w_down, next_norm_g=l1_mix_norm)
    xt = _odd_layer_mixer(xt, h1, l1_w_in, l1_c_conv_w, l1_c_conv_b, l1_c_ln_g, l1_c_ln_b, l1_w_out)
    xt, _ = _moe(xt, l1_ffn_norm, l1_router_group_w, l1_router_group_b, l1_router_expert_w,
                 l1_router_expert_b, l1_w_gate, l1_w_up, l1_w_down)
    return xt.reshape(bsz, seq, d)
```

```python
import functools

import jax
import jax.numpy as jnp
from jax import lax
from jax.experimental import pallas as pl
from jax.experimental.pallas import tpu as pltpu

F32 = jnp.float32
BF16 = jnp.bfloat16

HEAD_DIM = 128
CHUNK = 128
EPS = 1e-6
N_GROUPS = 8
EXPERTS_PER_GROUP = 8
N_EXPERTS = N_GROUPS * EXPERTS_PER_GROUP
EXPERT_BLOCK = 128
CONV_WIDTH = 31
SUBLANES = 8
CONV_ROWS = 128
CONV_HALO = 32
ROUTE_LANES = 128
SWA_KV_HEADS = 4
SB_LOG_ZERO = -104.0
SB_HEADS = 4
SB_KEYS = 384
COMBINE_ROWS = 16
COMBINE_COLS = 512
MOE_MAX_ROWS = 768
V7X_VMEM_LIMIT = 56 * 1024 * 1024
V7X_VMEM_LIMIT_MOE = 60 * 1024 * 1024


def _params(*sem):
    return pltpu.CompilerParams(dimension_semantics=sem, vmem_limit_bytes=V7X_VMEM_LIMIT)


def _rmsnorm_kernel(x_ref, g_ref, o_ref):
    x = x_ref[...]
    ms = jnp.mean(x * x, axis=-1, keepdims=True)
    o_ref[...] = (x * lax.rsqrt(ms + EPS) * g_ref[...]).astype(o_ref.dtype)


def _rmsnorm(x, g, tm=256):
    t, d = x.shape
    return pl.pallas_call(
        _rmsnorm_kernel,
        grid=(t // tm,),
        in_specs=[pl.BlockSpec((tm, d), lambda i: (i, 0)),
                  pl.BlockSpec((1, d), lambda i: (0, 0))],
        out_specs=pl.BlockSpec((tm, d), lambda i: (i, 0)),
        out_shape=jax.ShapeDtypeStruct((t, d), BF16),
        compiler_params=_params("parallel"),
        name="rmsnorm",
    )(x, g.reshape(1, d))


def _matmul_kernel(a_ref, w_ref, o_ref):
    o_ref[...] = jnp.dot(a_ref[...], w_ref[...], preferred_element_type=F32).astype(o_ref.dtype)


def _matmul(a, w, tm, tn):
    m, k = a.shape
    n = w.shape[1]
    tm = min(tm, m)
    return pl.pallas_call(
        _matmul_kernel,
        grid=(m // tm, n // tn),
        in_specs=[pl.BlockSpec((tm, k), lambda i, j: (i, 0)),
                  pl.BlockSpec((k, tn), lambda i, j: (0, j))],
        out_specs=pl.BlockSpec((tm, tn), lambda i, j: (i, j)),
        out_shape=jax.ShapeDtypeStruct((m, n), BF16),
        compiler_params=_params("parallel", "arbitrary"),
        name="in_proj",
    )(a, w)


def _out_proj_kernel(x_ref, a_ref, b_ref, wa_ref, wb_ref, o_ref):
    acc = jnp.dot(a_ref[...], wa_ref[...], preferred_element_type=F32)
    acc = acc + jnp.dot(b_ref[...], wb_ref[...], preferred_element_type=F32)
    o_ref[...] = x_ref[...] + acc


def _out_proj(x, ya, yb, w, tm=1024, tn=1024):
    t, d = x.shape
    ka, kb = ya.shape[1], yb.shape[1]
    assert ka == kb and w.shape == (ka + kb, d)
    tm = min(tm, t)
    return pl.pallas_call(
        _out_proj_kernel,
        grid=(t // tm, d // tn),
        in_specs=[pl.BlockSpec((tm, tn), lambda i, j: (i, j)),
                  pl.BlockSpec((tm, ka), lambda i, j: (i, 0)),
                  pl.BlockSpec((tm, kb), lambda i, j: (i, 0)),
                  pl.BlockSpec((ka, tn), lambda i, j: (0, j)),
                  pl.BlockSpec((kb, tn), lambda i, j: (1, j))],
        out_specs=pl.BlockSpec((tm, tn), lambda i, j: (i, j)),
        out_shape=jax.ShapeDtypeStruct((t, d), F32),
        compiler_params=_params("parallel", "arbitrary"),
        name="out_proj",
    )(x, ya, yb, w, w)


def _layer_norm(x, g, b):
    mu = jnp.mean(x, axis=-1, keepdims=True)
    xc = x - mu
    var = jnp.mean(xc * xc, axis=-1, keepdims=True)
    return xc * lax.rsqrt(var + EPS) * g + b


def _gelu(x):
    return 0.5 * x * (1.0 + lax.erf(x * (2.0 ** -0.5)))


def _gmlp_kernel(u_ref, v_ref, lng_ref, lnb_ref, w_ref, bs_ref, o_ref):
    n_groups = w_ref.shape[0]
    v = _gelu(v_ref[...].astype(F32))
    vn = _layer_norm(v, lng_ref[...], lnb_ref[...]).astype(BF16)
    group_cols = [slice(g * HEAD_DIM, (g + 1) * HEAD_DIM) for g in range(n_groups)]
    mixed = [jnp.dot(w_ref[g], vn[:, cols], preferred_element_type=F32) for g, cols in enumerate(group_cols)]
    for g, cols in enumerate(group_cols):
        u = _gelu(u_ref[:, cols].astype(F32))
        o_ref[:, cols] = (u * (mixed[g] + bs_ref[:, cols])).astype(o_ref.dtype)


def _gmlp(proj, ln_g, ln_b, w_s, b_s):
    t = proj.shape[0]
    n_groups = w_s.shape[0]
    width = n_groups * HEAD_DIM
    causal = jnp.tril(jnp.ones((CHUNK, CHUNK), dtype=bool))
    w = jnp.where(causal[None], w_s, 0.0).astype(BF16)
    bs = jnp.repeat(b_s.T.astype(F32), HEAD_DIM, axis=1)
    return pl.pallas_call(
        _gmlp_kernel,
        grid=(t // CHUNK,),
        in_specs=[pl.BlockSpec((CHUNK, width), lambda i: (i, 0)),
                  pl.BlockSpec((CHUNK, width), lambda i: (i, 1)),
                  pl.BlockSpec((1, width), lambda i: (0, 0)),
                  pl.BlockSpec((1, width), lambda i: (0, 0)),
                  pl.BlockSpec((n_groups, CHUNK, CHUNK), lambda i: (0, 0, 0)),
                  pl.BlockSpec((CHUNK, width), lambda i: (0, 0))],
        out_specs=pl.BlockSpec((CHUNK, width), lambda i: (i, 0)),
        out_shape=jax.ShapeDtypeStruct((t, width), BF16),
        compiler_params=_params("parallel"),
        name="gmlp",
    )(proj, proj, ln_g.reshape(1, width), ln_b.reshape(1, width), w, bs)


def _swa_kernel(slopes_ref, sinks_ref, q_ref, kc_ref, kp_ref, vc_ref, vp_ref, qg_ref, kg_ref, o_ref, *, kvs, grp):
    n = pl.program_id(0)
    first_head = pl.program_id(1) * kvs * grp
    qi = lax.broadcasted_iota(jnp.int32, (CHUNK, 2 * CHUNK), 0)
    kj = lax.broadcasted_iota(jnp.int32, (CHUNK, 2 * CHUNK), 1)
    dist = qi + CHUNK - kj
    valid = (dist >= 0) & (dist < CHUNK) & (n * CHUNK - CHUNK + kj >= 0)
    distf = dist.astype(F32)
    head_cols = lambda j: slice(j * HEAD_DIM, (j + 1) * HEAD_DIM)
    scores, values = [], []
    for kv in range(kvs):
        k = jnp.concatenate([kp_ref[:, head_cols(kv)], kc_ref[:, head_cols(kv)]], axis=0).astype(F32)
        k = k * lax.rsqrt(jnp.mean(k * k, axis=-1, keepdims=True) + EPS) * kg_ref[...]
        kb = k.astype(BF16)
        values.append(jnp.concatenate([vp_ref[:, head_cols(kv)], vc_ref[:, head_cols(kv)]], axis=0))
        for g in range(grp):
            q = q_ref[:, head_cols(kv * grp + g)].astype(F32)
            q = q * lax.rsqrt(jnp.mean(q * q, axis=-1, keepdims=True) + EPS) * qg_ref[...]
            scores.append(lax.dot_general(q.astype(BF16), kb, (((1,), (1,)), ((), ())),
                                          preferred_element_type=F32) * (HEAD_DIM ** -0.5))
    probs, denoms = [], []
    for j in range(kvs * grp):
        slope = slopes_ref[first_head + j]
        sink = sinks_ref[first_head + j]
        s = jnp.where(valid, scores[j] - slope * distf, -jnp.inf)
        m = jnp.maximum(jnp.max(s, axis=-1, keepdims=True), sink)
        p = jnp.exp(s - m)
        denoms.append(jnp.sum(p, axis=-1, keepdims=True) + jnp.exp(sink - m))
        probs.append(p.astype(BF16))
    for j in range(kvs * grp):
        o = jnp.dot(probs[j], values[j // grp], preferred_element_type=F32) / denoms[j]
        o_ref[:, head_cols(j)] = o.astype(o_ref.dtype)


def _swa(proj, q_gain, k_gain, sinks, n_heads, n_kv, col0):
    t = proj.shape[0]
    grp = n_heads // n_kv
    kvs = min(SWA_KV_HEADS, n_kv)
    q_width, kv_width = kvs * grp * HEAD_DIM, kvs * HEAD_DIM
    assert col0 % q_width == 0 and (col0 + n_heads * HEAD_DIM) % kv_width == 0 and n_kv % kvs == 0
    qb0 = col0 // q_width
    kb0 = (col0 + n_heads * HEAD_DIM) // kv_width
    vb0 = kb0 + n_kv // kvs
    slopes = 2.0 ** (-8.0 * jnp.arange(1, n_heads + 1, dtype=F32) / n_heads)
    prev = lambda n: jnp.maximum(n - 1, 0)
    return pl.pallas_call(
        functools.partial(_swa_kernel, kvs=kvs, grp=grp),
        grid_spec=pltpu.PrefetchScalarGridSpec(
            num_scalar_prefetch=2,
            grid=(t // CHUNK, n_kv // kvs),
            in_specs=[pl.BlockSpec((CHUNK, q_width), lambda n, h, *_: (n, qb0 + h)),
                      pl.BlockSpec((CHUNK, kv_width), lambda n, h, *_: (n, kb0 + h)),
                      pl.BlockSpec((CHUNK, kv_width), lambda n, h, *_: (prev(n), kb0 + h)),
                      pl.BlockSpec((CHUNK, kv_width), lambda n, h, *_: (n, vb0 + h)),
                      pl.BlockSpec((CHUNK, kv_width), lambda n, h, *_: (prev(n), vb0 + h)),
                      pl.BlockSpec((1, HEAD_DIM), lambda n, h, *_: (0, 0)),
                      pl.BlockSpec((1, HEAD_DIM), lambda n, h, *_: (0, 0))],
            out_specs=pl.BlockSpec((CHUNK, q_width), lambda n, h, *_: (n, h)),
        ),
        out_shape=jax.ShapeDtypeStruct((t, n_heads * HEAD_DIM), BF16),
        compiler_params=_params("parallel", "arbitrary"),
        name="swa",
    )(slopes, sinks.astype(F32), proj, proj, proj, proj, proj,
      q_gain.reshape(1, HEAD_DIM).astype(F32), k_gain.reshape(1, HEAD_DIM).astype(F32))


def _conformer_kernel(a_ref, g_ref, ap_ref, gp_ref, w_ref, cb_ref, lg_ref, lb_ref, o_ref, hbuf, ybuf):
    i = pl.program_id(0)
    tc, width = a_ref.shape
    span = CONV_HALO + tc
    hp = ap_ref[...].astype(F32) * jax.nn.sigmoid(gp_ref[...].astype(F32))
    hbuf[0, 0:CONV_HALO, :] = jnp.where(i > 0, hp, 0.0)
    hbuf[0, CONV_HALO:span, :] = a_ref[...].astype(F32) * jax.nn.sigmoid(g_ref[...].astype(F32))
    hbuf[0, span:span + SUBLANES, :] = jnp.zeros((SUBLANES, width), F32)
    for r in range(1, SUBLANES):
        hbuf[r, 0:span, :] = hbuf[0, r:r + span, :]
    base = CONV_HALO - (CONV_WIDTH - 1)
    for c in range(width // HEAD_DIM):
        cols = slice(c * HEAD_DIM, (c + 1) * HEAD_DIM)
        for t0 in range(0, tc, CONV_ROWS):
            acc = jnp.zeros((CONV_ROWS, HEAD_DIM), F32)
            for r in range(SUBLANES):
                taps = [j for j in range(CONV_WIDTH) if (base + j) % SUBLANES == r]
                first = t0 + base + taps[0] - r
                rows = hbuf[r, first:t0 + base + taps[-1] - r + CONV_ROWS, cols]
                for j in taps:
                    off = t0 + base + j - r - first
                    acc = acc + w_ref[j:j + 1, cols] * rows[off:off + CONV_ROWS, :]
            ybuf[t0:t0 + CONV_ROWS, cols] = acc
    y = _layer_norm(ybuf[...] + cb_ref[...], lg_ref[...], lb_ref[...])
    o_ref[...] = (y * jax.nn.sigmoid(y)).astype(o_ref.dtype)


def _conformer(proj, conv_w, conv_b, ln_g, ln_b, tc=256):
    t = proj.shape[0]
    width = conv_w.shape[1]
    tc = min(tc, t)
    halo_blocks = tc // CONV_HALO
    prev = lambda i: jnp.maximum(i * halo_blocks - 1, 0)
    vec = lambda: pl.BlockSpec((1, width), lambda i: (0, 0))
    return pl.pallas_call(
        _conformer_kernel,
        grid=(t // tc,),
        in_specs=[pl.BlockSpec((tc, width), lambda i: (i, 0)),
                  pl.BlockSpec((tc, width), lambda i: (i, 1)),
                  pl.BlockSpec((CONV_HALO, width), lambda i: (prev(i), 0)),
                  pl.BlockSpec((CONV_HALO, width), lambda i: (prev(i), 1)),
                  pl.BlockSpec((CONV_WIDTH, width), lambda i: (0, 0)),
                  vec(), vec(), vec()],
        out_specs=pl.BlockSpec((tc, width), lambda i: (i, 0)),
        out_shape=jax.ShapeDtypeStruct((t, width), BF16),
        scratch_shapes=[pltpu.VMEM((SUBLANES, CONV_HALO + tc + SUBLANES, width), F32),
                        pltpu.VMEM((tc, width), F32)],
        compiler_params=_params("parallel"),
        name="conformer",
    )(proj, proj, proj, proj, conv_w.astype(F32), conv_b.reshape(1, width).astype(F32),
      ln_g.reshape(1, width).astype(F32), ln_b.reshape(1, width).astype(F32))


def _stickbreak_kernel(q_ref, k_ref, v_ref, tri_ref, o_ref, *, heads):
    i = pl.program_id(1)
    tq = q_ref.shape[0]
    col = lax.broadcasted_iota(jnp.int32, (tq, SB_KEYS), 1)
    q_pos = i * tq + lax.broadcasted_iota(jnp.int32, (tq, 1), 0)

    def walk_tile(upper, cs, accs):
        start = pl.multiple_of(jnp.maximum(upper - SB_KEYS, 0), tq)
        valid = col < jnp.minimum(q_pos, upper) - start
        head_cols = [slice(g * HEAD_DIM, (g + 1) * HEAD_DIM) for g in range(heads)]
        zs = [lax.dot_general(q_ref[:, cols], k_ref[pl.ds(start, SB_KEYS), cols], (((1,), (1,)), ((), ())),
                              preferred_element_type=F32) * (HEAD_DIM ** -0.5) for cols in head_cols]
        log_wins, splits, new_cs = [], [], []
        for g in range(heads):
            z = zs[g]
            softplus = jnp.maximum(z, 0.0) + jnp.log(1.0 + jnp.exp(-jnp.abs(z)))
            log_fail = jnp.where(valid, -softplus, 0.0)
            hi = log_fail.astype(BF16)
            lo = (log_fail - hi.astype(F32)).astype(BF16)
            splits.append(jnp.concatenate([hi, lo], axis=1))
            log_wins.append(z - softplus)
            new_cs.append(cs[g] + jnp.sum(log_fail, axis=-1, keepdims=True))
        later = jnp.dot(jnp.concatenate(splits, axis=0), tri_ref[...], preferred_element_type=F32)
        new_accs = []
        for g in range(heads):
            a = jnp.where(valid, jnp.exp(log_wins[g] + later[g * tq:(g + 1) * tq, :] + cs[g]), 0.0)
            new_accs.append(accs[g] + jnp.dot(a.astype(BF16), v_ref[pl.ds(start, SB_KEYS), head_cols[g]],
                                              preferred_element_type=F32))
        return start, new_cs, new_accs

    def cond(state):
        upper, cs, _ = state
        alive = functools.reduce(jnp.maximum, [jnp.max(c) for c in cs])
        return jnp.logical_and(upper > 0, alive > SB_LOG_ZERO)

    def body(state):
        return walk_tile(*state)

    init = ((i + 1) * tq,
            [jnp.zeros((tq, 1), F32) for _ in range(heads)],
            [jnp.zeros((tq, HEAD_DIM), F32) for _ in range(heads)])
    _, _, accs = lax.while_loop(cond, body, init)
    for g in range(heads):
        o_ref[:, g * HEAD_DIM:(g + 1) * HEAD_DIM] = accs[g].astype(o_ref.dtype)


def _stickbreak(proj, n_heads, col0):
    t = proj.shape[0]
    width = SB_HEADS * HEAD_DIM
    qb0 = col0 // width
    kb0 = qb0 + n_heads // SB_HEADS
    vb0 = kb0 + n_heads // SB_HEADS
    s_later = jnp.arange(2 * SB_KEYS)[:, None] % SB_KEYS
    tri = (s_later > jnp.arange(SB_KEYS)[None, :]).astype(BF16)
    resident = lambda c0: pl.BlockSpec((t, width), lambda h, i: (0, c0 + h), pipeline_mode=pl.Buffered(1))
    return pl.pallas_call(
        functools.partial(_stickbreak_kernel, heads=SB_HEADS),
        grid=(n_heads // SB_HEADS, t // CHUNK),
        in_specs=[pl.BlockSpec((CHUNK, width), lambda h, i: (i, qb0 + h)),
                  resident(kb0), resident(vb0),
                  pl.BlockSpec((2 * SB_KEYS, SB_KEYS), lambda h, i: (0, 0))],
        out_specs=pl.BlockSpec((CHUNK, width), lambda h, i: (i, h)),
        out_shape=jax.ShapeDtypeStruct((t, n_heads * HEAD_DIM), BF16),
        compiler_params=_params("parallel", "arbitrary"),
        name="stickbreak",
    )(proj, proj, proj, tri)


def _router_kernel(x_ref, g_ref, whi_ref, wlo_ref, b_ref, hp_ref, route_ref, counts_ref, carry):
    @pl.when(pl.program_id(0) == 0)
    def _():
        carry[...] = jnp.zeros_like(carry)

    x = x_ref[...]
    tm, d = x.shape
    h = x * lax.rsqrt(jnp.mean(x * x, axis=-1, keepdims=True) + EPS) * g_ref[...]
    h_hi = h.astype(BF16)
    h_hif = h_hi.astype(F32)
    h_lo = (h - h_hif).astype(BF16)
    logits = (jnp.dot(h_hi, whi_ref[...], preferred_element_type=F32)
              + jnp.dot(h_lo, whi_ref[...], preferred_element_type=F32)
              + jnp.dot(h_hi, wlo_ref[...], preferred_element_type=F32)) + b_ref[...]
    lane = lax.broadcasted_iota(jnp.int32, logits.shape, 1)
    neg = -jnp.inf

    def first_max(vals):
        m = jnp.max(vals, axis=-1, keepdims=True)
        idx = jnp.min(jnp.where(vals == m, lane, ROUTE_LANES), axis=-1, keepdims=True)
        return m, idx

    gl = jnp.where(lane < N_GROUPS, logits, neg)
    gmax, g_sel = first_max(gl)
    g_weight = 1.0 / jnp.sum(jnp.exp(gl - gmax), axis=-1, keepdims=True)
    expert = lane - N_GROUPS
    in_group = (expert >= 0) & (expert < N_EXPERTS) & ((expert >> 3) == g_sel)
    el = jnp.where(in_group, logits, neg)
    m1, i1 = first_max(el)
    m2, i2 = first_max(jnp.where(lane == i1, neg, el))
    e21 = jnp.exp(m2 - m1)
    w1 = g_weight / (1.0 + e21)
    w2 = w1 * e21
    sel1 = lane == i1
    sel2 = lane == i2
    chosen = jnp.where(jnp.logical_or(sel1, sel2), 1.0, 0.0)
    row = lax.broadcasted_iota(jnp.int32, (tm, tm), 0)
    col = lax.broadcasted_iota(jnp.int32, (tm, tm), 1)
    earlier = jnp.where(col < row, 1.0, 0.0).astype(BF16)
    before = jnp.dot(earlier, chosen.astype(BF16), preferred_element_type=F32) + carry[...]
    rank1 = jnp.sum(jnp.where(sel1, before, 0.0), axis=-1, keepdims=True)
    rank2 = jnp.sum(jnp.where(sel2, before, 0.0), axis=-1, keepdims=True)
    carry[...] = carry[...] + jnp.sum(chosen, axis=0, keepdims=True)
    counts_ref[...] = carry[...]
    fields = ((i1 - N_GROUPS).astype(F32), (i2 - N_GROUPS).astype(F32), w1, w2, rank1, rank2)
    rec = jnp.zeros_like(logits)
    for k, val in enumerate(fields):
        rec = jnp.where(lane == k, val, rec)
    route_ref[...] = rec
    bits = lax.bitcast_convert_type(h_hif, jnp.uint32)
    hp_ref[...] = (bits[:, :d // 2] >> 16) | (bits[:, d // 2:] & jnp.uint32(0xFFFF0000))


def _router(x, norm_g, wr_g, br_g, wr_e, br_e, tm=512):
    t, d = x.shape
    pad = ROUTE_LANES - N_GROUPS - N_EXPERTS
    w = jnp.concatenate([wr_g.astype(F32), wr_e.astype(F32), jnp.zeros((d, pad), F32)], axis=1)
    b = jnp.concatenate([br_g.astype(F32), br_e.astype(F32), jnp.zeros((pad,), F32)]).reshape(1, ROUTE_LANES)
    w_hi = w.astype(BF16)
    w_lo = (w - w_hi.astype(F32)).astype(BF16)
    tm = min(tm, t)
    return pl.pallas_call(
        _router_kernel,
        grid=(t // tm,),
        in_specs=[pl.BlockSpec((tm, d), lambda i: (i, 0)),
                  pl.BlockSpec((1, d), lambda i: (0, 0)),
                  pl.BlockSpec((d, ROUTE_LANES), lambda i: (0, 0)),
                  pl.BlockSpec((d, ROUTE_LANES), lambda i: (0, 0)),
                  pl.BlockSpec((1, ROUTE_LANES), lambda i: (0, 0))],
        out_specs=[pl.BlockSpec((tm, d // 2), lambda i: (i, 0)),
                   pl.BlockSpec((tm, ROUTE_LANES), lambda i: (i, 0)),
                   pl.BlockSpec((1, ROUTE_LANES), lambda i: (0, 0))],
        out_shape=[jax.ShapeDtypeStruct((t, d // 2), jnp.uint32),
                   jax.ShapeDtypeStruct((t, ROUTE_LANES), F32),
                   jax.ShapeDtypeStruct((1, ROUTE_LANES), F32)],
        scratch_shapes=[pltpu.VMEM((1, ROUTE_LANES), F32)],
        compiler_params=_params("arbitrary"),
        name="router",
    )(x, norm_g.reshape(1, d), w_hi, w_lo, b)


def _lookup(table, idx):
    hit = idx[..., None] == jnp.arange(table.shape[0], dtype=jnp.int32)
    return jnp.sum(jnp.where(hit, table, 0), axis=-1)


def _dispatch_tables(route, counts, t):
    counts = counts[0, N_GROUPS:N_GROUPS + N_EXPERTS].astype(jnp.int32)
    padded = (counts + EXPERT_BLOCK - 1) // EXPERT_BLOCK * EXPERT_BLOCK
    padded_end = jnp.cumsum(padded)
    padded_start = padded_end - padded
    expert = route[:, 0:2].astype(jnp.int32)
    rank = route[:, 4:6].astype(jnp.int32)
    assign_slot = (_lookup(padded_start, expert) + rank).reshape(-1)

    n_slots = (-(-2 * t // EXPERT_BLOCK) + N_EXPERTS) * EXPERT_BLOCK
    n_items = N_EXPERTS + n_slots // MOE_MAX_ROWS
    chunks = (padded + MOE_MAX_ROWS - 1) // MOE_MAX_ROWS
    chunk_end = jnp.cumsum(chunks)
    n_used = chunk_end[-1]
    ids = jnp.arange(n_items, dtype=jnp.int32)
    item_e = jnp.minimum(jnp.sum(chunk_end[None, :] <= ids[:, None], axis=1), N_EXPERTS - 1).astype(jnp.int32)
    j = ids - _lookup(chunk_end - chunks, item_e)
    item_start = _lookup(padded_start, item_e) + j * MOE_MAX_ROWS
    item_rows = jnp.clip(_lookup(padded, item_e) - j * MOE_MAX_ROWS, 0, MOE_MAX_ROWS)
    used = ids < n_used
    last_e = jnp.sum(jnp.where(ids == n_used - 1, item_e, 0))
    item_e = jnp.where(used, item_e, last_e).astype(jnp.int32)
    item_start = jnp.where(used, item_start, 0).astype(jnp.int32)
    item_rows = jnp.where(used, item_rows, 0).astype(jnp.int32)
    total = padded_end[-1].reshape(1).astype(jnp.int32)
    return (assign_slot.astype(jnp.int32), padded_start.astype(jnp.int32), padded.astype(jnp.int32), total,
            item_e, item_start, item_rows, n_slots, n_items)


def _dispatch_kernel(as_ref, ps_ref, pd_ref, tot_ref, hp_ref, xs_hbm, zbuf, sem, zsem):
    i = pl.program_id(0)
    tm = hp_ref.shape[0]
    n_blocks = xs_hbm.shape[0] // EXPERT_BLOCK

    def zero_block(first_row):
        return pltpu.make_async_copy(zbuf, xs_hbm.at[pl.ds(pl.multiple_of(first_row, EXPERT_BLOCK), EXPERT_BLOCK)],
                                     zsem)

    def zero_fill(act):
        def per_expert(e, carry):
            @pl.when(pd_ref[e] > 0)
            def _():
                act(zero_block(ps_ref[e] + pd_ref[e] - EXPERT_BLOCK))
            return carry
        lax.fori_loop(0, N_EXPERTS, per_expert, 0)

        def per_tail_block(b, carry):
            act(zero_block(b * EXPERT_BLOCK))
            return carry
        lax.fori_loop(tot_ref[0] // EXPERT_BLOCK, n_blocks, per_tail_block, 0)

    @pl.when(i == 0)
    def _():
        zbuf[...] = jnp.zeros_like(zbuf)
        zero_fill(lambda cp: cp.start())
        zero_fill(lambda cp: cp.wait())

    def body(r, carry):
        a = 2 * (i * tm + r)
        for k in range(2):
            pltpu.make_async_copy(hp_ref.at[pl.ds(r, 1)], xs_hbm.at[pl.ds(as_ref[a + k], 1)], sem).start()
        return carry
    lax.fori_loop(0, tm, body, 0, unroll=8)
    for k in range(2):
        pltpu.make_async_copy(hp_ref, xs_hbm.at[pl.ds(0, tm)], sem).wait()


def _dispatch(hp, assign_slot, padded_start, padded, total, n_slots, tm=256):
    t, words = hp.shape
    tm = min(tm, t)
    return pl.pallas_call(
        _dispatch_kernel,
        grid_spec=pltpu.PrefetchScalarGridSpec(
            num_scalar_prefetch=4,
            grid=(t // tm,),
            in_specs=[pl.BlockSpec((tm, words), lambda i, *_: (i, 0))],
            out_specs=pl.BlockSpec(memory_space=pl.ANY),
            scratch_shapes=[pltpu.VMEM((EXPERT_BLOCK, words), hp.dtype),
                            pltpu.SemaphoreType.DMA(()),
                            pltpu.SemaphoreType.DMA(())],
        ),
        out_shape=jax.ShapeDtypeStruct((n_slots, words), hp.dtype),
        compiler_params=_params("arbitrary"),
        name="moe_dispatch",
    )(assign_slot, padded_start, padded, total, hp)


def _unpack_pairs(words):
    lo = lax.bitcast_convert_type(words << 16, F32)
    hi = lax.bitcast_convert_type(words & jnp.uint32(0xFFFF0000), F32)
    return lo, hi


def _pack_pairs(lo, hi):
    lo_bits = lax.bitcast_convert_type(lo.astype(BF16).astype(F32), jnp.uint32)
    hi_bits = lax.bitcast_convert_type(hi.astype(BF16).astype(F32), jnp.uint32)
    return (lo_bits >> 16) | (hi_bits & jnp.uint32(0xFFFF0000))


def _expert_items_kernel(ie_ref, is_ref, ir_ref, tot_ref, in_hbm, *refs, n_weights, cast_weights, compute):
    w_refs = refs[:n_weights]
    out_hbm = refs[n_weights]
    inbuf, outbuf, zbuf, w_bf, sem_in, sem_out, sem_z = refs[n_weights + 1:]
    i = pl.program_id(0)
    n_items = pl.num_programs(0)
    max_blocks = inbuf.shape[1] // EXPERT_BLOCK
    n_out_blocks = out_hbm.shape[0] // EXPERT_BLOCK

    def rows_of(hbm, first_row):
        return hbm.at[pl.ds(pl.multiple_of(first_row, EXPERT_BLOCK), EXPERT_BLOCK)]

    def for_blocks(item, act):
        for c in range(max_blocks):
            @pl.when(c * EXPERT_BLOCK < ir_ref[item])
            def _():
                act(c, is_ref[item] + c * EXPERT_BLOCK)

    def in_copy(buf, act):
        return lambda c, row: act(pltpu.make_async_copy(
            rows_of(in_hbm, row), inbuf.at[buf, pl.ds(c * EXPERT_BLOCK, EXPERT_BLOCK)], sem_in.at[buf]))

    def out_copy(buf, act):
        return lambda c, row: act(pltpu.make_async_copy(
            outbuf.at[buf, pl.ds(c * EXPERT_BLOCK, EXPERT_BLOCK)], rows_of(out_hbm, row), sem_out.at[buf]))

    start = lambda cp: cp.start()
    wait = lambda cp: cp.wait()
    buf = i % 2

    @pl.when(i == 0)
    def _():
        for_blocks(0, in_copy(0, start))
        zbuf[...] = jnp.zeros_like(zbuf)

        def tail(act):
            def body(b, carry):
                act(pltpu.make_async_copy(zbuf, rows_of(out_hbm, b * EXPERT_BLOCK), sem_z))
                return carry
            lax.fori_loop(tot_ref[0] // EXPERT_BLOCK, n_out_blocks, body, 0)
        tail(start)
        tail(wait)

    @pl.when(i + 1 < n_items)
    def _():
        for_blocks(i + 1, in_copy(1 - buf, start))

    for_blocks(i, in_copy(buf, wait))

    @pl.when(jnp.logical_or(i == 0, ie_ref[i] != ie_ref[jnp.maximum(i - 1, 0)]))
    def _():
        cast_weights(w_refs, w_bf)

    @pl.when(i >= 2)
    def _():
        for_blocks(i - 2, out_copy(buf, wait))

    n_blk = ir_ref[i] // EXPERT_BLOCK
    for c in range(1, max_blocks + 1):
        @pl.when(n_blk == c)
        def _():
            rows = c * EXPERT_BLOCK
            outbuf[buf, 0:rows, :] = compute(inbuf[buf, 0:rows, :], w_bf)

    for_blocks(i, out_copy(buf, start))

    @pl.when(i == n_items - 1)
    def _():
        @pl.when(i >= 1)
        def _():
            for_blocks(i - 1, out_copy(1 - buf, wait))
        for_blocks(i, out_copy(buf, wait))


def _expert_items_call(name, items, in_rows, weights, out_width, out_dtype, n_w_cols, cast_weights, compute):
    item_e, item_start, item_rows, total = items
    n_slots, in_width = in_rows.shape
    k_dim = weights[0].shape[1]
    w_spec = lambda w: pl.BlockSpec((None,) + w.shape[1:], lambda i, ie, *_: (ie[i], 0, 0))
    return pl.pallas_call(
        functools.partial(_expert_items_kernel, n_weights=len(weights), cast_weights=cast_weights, compute=compute),
        grid_spec=pltpu.PrefetchScalarGridSpec(
            num_scalar_prefetch=4,
            grid=(item_e.shape[0],),
            in_specs=[pl.BlockSpec(memory_space=pl.ANY)] + [w_spec(w) for w in weights],
            out_specs=pl.BlockSpec(memory_space=pl.ANY),
            scratch_shapes=[pltpu.VMEM((2, MOE_MAX_ROWS, in_width), in_rows.dtype),
                            pltpu.VMEM((2, MOE_MAX_ROWS, out_width), out_dtype),
                            pltpu.VMEM((EXPERT_BLOCK, out_width), out_dtype),
                            pltpu.VMEM((k_dim, n_w_cols), BF16),
                            pltpu.SemaphoreType.DMA((2,)),
                            pltpu.SemaphoreType.DMA((2,)),
                            pltpu.SemaphoreType.DMA(())],
        ),
        out_shape=jax.ShapeDtypeStruct((n_slots, out_width), out_dtype),
        compiler_params=pltpu.CompilerParams(dimension_semantics=("arbitrary",),
                                             vmem_limit_bytes=V7X_VMEM_LIMIT_MOE),
        name=name,
    )(item_e, item_start, item_rows, total, in_rows, *weights)


def _cast_gate_up(w_refs, w_bf):
    d_exp = w_refs[0].shape[1]
    w_bf[:, 0:d_exp] = w_refs[0][...].astype(BF16)
    w_bf[:, d_exp:2 * d_exp] = w_refs[1][...].astype(BF16)


def _gate_up(words, w_bf):
    half = words.shape[1]
    d_exp = w_bf.shape[1] // 2
    lo, hi = _unpack_pairs(words)
    gu = (jnp.dot(lo.astype(BF16), w_bf[0:half, :], preferred_element_type=F32)
          + jnp.dot(hi.astype(BF16), w_bf[half:2 * half, :], preferred_element_type=F32))
    gate, up = gu[:, 0:d_exp], gu[:, d_exp:2 * d_exp]
    return (gate * jax.nn.sigmoid(gate) * up).astype(BF16)


def _cast_down(w_refs, w_bf):
    w_bf[...] = w_refs[0][...].astype(BF16)


def _down(hid, w_bf):
    half = w_bf.shape[1] // 2
    y = jnp.dot(hid, w_bf[...], preferred_element_type=F32)
    return _pack_pairs(y[:, 0:half], y[:, half:2 * half])


def _combine_kernel(as_ref, x_ref, route_ref, g_ref, y_hbm, o_ref, *rest, emit_norm):
    if emit_norm:
        h_ref, ybuf, sem = rest
    else:
        ybuf, sem = rest
    i = pl.program_id(0)
    tm, d = x_ref.shape
    half = d // 2

    def gather_row(tile, r, buf):
        a = 2 * (tile * tm + r)
        for k in range(2):
            pltpu.make_async_copy(y_hbm.at[pl.ds(as_ref[a + k], 1)], ybuf.at[buf, k, pl.ds(r, 1)],
                                  sem.at[buf]).start()

    def wait_gather(buf):
        for k in range(2):
            pltpu.make_async_copy(y_hbm.at[pl.ds(0, tm)], ybuf.at[buf, k], sem.at[buf]).wait()

    @pl.when(i == 0)
    def _():
        def body(r, carry):
            gather_row(0, r, 0)
            return carry
        lax.fori_loop(0, tm, body, 0, unroll=8)

    buf = i % 2
    last = pl.num_programs(0) - 1
    next_tile = jnp.minimum(i + 1, last)
    wait_gather(buf)

    def row_chunk(c, carry):
        for rr in range(COMBINE_ROWS):
            gather_row(next_tile, c * COMBINE_ROWS + rr, 1 - buf)
        rows = pl.ds(pl.multiple_of(c * COMBINE_ROWS, COMBINE_ROWS), COMBINE_ROWS)
        w1 = route_ref[rows, 2:3]
        w2 = route_ref[rows, 3:4]
        sq = jnp.zeros((COMBINE_ROWS, 1), F32)
        for c0 in range(0, half, COMBINE_COLS):
            cols = slice(c0, c0 + COMBINE_COLS)
            up_cols = slice(half + c0, half + c0 + COMBINE_COLS)
            lo1, hi1 = _unpack_pairs(ybuf[buf, 0, rows, cols])
            lo2, hi2 = _unpack_pairs(ybuf[buf, 1, rows, cols])
            out_lo = x_ref[rows, cols] + w1 * lo1 + w2 * lo2
            out_hi = x_ref[rows, up_cols] + w1 * hi1 + w2 * hi2
            o_ref[rows, cols] = out_lo
            o_ref[rows, up_cols] = out_hi
            if emit_norm:
                sq = sq + jnp.sum(out_lo * out_lo + out_hi * out_hi, axis=-1, keepdims=True)
        if emit_norm:
            inv = lax.rsqrt(sq * (1.0 / d) + EPS)
            for c0 in range(0, d, COMBINE_COLS):
                cols = slice(c0, c0 + COMBINE_COLS)
                h_ref[rows, cols] = (o_ref[rows, cols] * inv * g_ref[:, cols]).astype(h_ref.dtype)
        return carry
    lax.fori_loop(0, tm // COMBINE_ROWS, row_chunk, 0, unroll=4)

    @pl.when(i == last)
    def _():
        wait_gather(1 - buf)


def _moe(x, norm_g, wr_g, br_g, wr_e, br_e, w_gate, w_up, w_down, next_norm_g=None, tm_combine=256):
    t, d = x.shape
    d_exp = w_gate.shape[2]
    hp, route, counts = _router(x, norm_g, wr_g, br_g, wr_e, br_e)
    (assign_slot, padded_start, padded, total, item_e, item_start, item_rows,
     n_slots, _) = _dispatch_tables(route, counts, t)
    items = (item_e, item_start, item_rows, total)
    xs = _dispatch(hp, assign_slot, padded_start, padded, total, n_slots)
    hid = _expert_items_call("moe_up", items, xs, (w_gate, w_up), d_exp, BF16, 2 * d_exp,
                             _cast_gate_up, _gate_up)
    y = _expert_items_call("moe_down", items, hid, (w_down,), d // 2, jnp.uint32, d, _cast_down, _down)

    tm = min(tm_combine, t)
    emit_norm = next_norm_g is not None
    row_tile = pl.BlockSpec((tm, d), lambda i, a: (i, 0))
    gain = next_norm_g if emit_norm else jnp.ones((d,), F32)
    out = pl.pallas_call(
        functools.partial(_combine_kernel, emit_norm=emit_norm),
        grid_spec=pltpu.PrefetchScalarGridSpec(
            num_scalar_prefetch=1,
            grid=(t // tm,),
            in_specs=[row_tile,
                      pl.BlockSpec((tm, ROUTE_LANES), lambda i, a: (i, 0)),
                      pl.BlockSpec((1, d), lambda i, a: (0, 0)),
                      pl.BlockSpec(memory_space=pl.ANY)],
            out_specs=[row_tile, row_tile] if emit_norm else row_tile,
            scratch_shapes=[pltpu.VMEM((2, 2, tm, d // 2), jnp.uint32),
                            pltpu.SemaphoreType.DMA((2,))],
        ),
        out_shape=([jax.ShapeDtypeStruct((t, d), F32), jax.ShapeDtypeStruct((t, d), BF16)] if emit_norm
                   else jax.ShapeDtypeStruct((t, d), F32)),
        compiler_params=_params("arbitrary"),
        name="moe_combine",
    )(assign_slot, x, route, gain.reshape(1, d).astype(F32), y)
    return tuple(out) if emit_norm else (out, None)


def _even_layer_mixer(x, mix_norm, w_in, a_ln_g, a_ln_b, a_ws, a_bs, b_q_norm, b_k_norm, b_sinks, w_out):
    n_groups = a_ws.shape[0]
    a_width = n_groups * HEAD_DIM
    n_heads = b_sinks.shape[0]
    n_kv = (w_in.shape[1] - 2 * a_width - n_heads * HEAD_DIM) // (2 * HEAD_DIM)
    h = _rmsnorm(x, mix_norm)
    proj = _matmul(h, w_in.astype(BF16), tm=1024, tn=1024)
    y_a = _gmlp(proj, a_ln_g, a_ln_b, a_ws, a_bs)
    y_b = _swa(proj, b_q_norm, b_k_norm, b_sinks, n_heads, n_kv, col0=2 * a_width)
    return _out_proj(x, y_a, y_b, w_out.astype(BF16))


def _odd_layer_mixer(x, h, w_in, c_conv_w, c_conv_b, c_ln_g, c_ln_b, w_out):
    c_width = c_conv_w.shape[1]
    n_heads = (w_in.shape[1] - 2 * c_width) // (3 * HEAD_DIM)
    proj = _matmul(h, w_in.astype(BF16), tm=1024, tn=1024)
    y_c = _conformer(proj, c_conv_w, c_conv_b, c_ln_g, c_ln_b)
    y_d = _stickbreak(proj, n_heads, col0=2 * c_width)
    return _out_proj(x, y_c, y_d, w_out.astype(BF16))


def kernel(x, l0_mix_norm, l0_w_in, l0_a_ln_g, l0_a_ln_b, l0_a_ws, l0_a_bs, l0_b_q_norm, l0_b_k_norm, l0_b_sinks, l0_w_out, l0_ffn_norm, l0_router_group_w, l0_router_group_b, l0_router_expert_w, l0_router_expert_b, l0_w_gate, l0_w_up, l0_w_down, l1_mix_norm, l1_w_in, l1_c_conv_w, l1_c_conv_b, l1_c_ln_g, l1_c_ln_b, l1_w_out, l1_ffn_norm, l1_router_group_w, l1_router_group_b, l1_router_expert_w, l1_router_expert_b, l1_w_gate, l1_w_up, l1_w_down):
    bsz, seq, d = x.shape
    xt = x.reshape(bsz * seq, d)
    assert bsz == 1, "the mixers index positions by row; one sequence per call"
    xt = _even_layer_mixer(xt, l0_mix_norm, l0_w_in, l0_a_ln_g, l0_a_ln_b, l0_a_ws, l0_a_bs,
                           l0_b_q_norm, l0_b_k_norm, l0_b_sinks, l0_w_out)
    xt, h1 = _moe(xt, l0_ffn_norm, l0_router_group_w, l0_router_group_b, l0_router_expert_w,
                  l0_router_expert_b, l0_w_gate, l0_w_up, l0_w_down, next_norm_g=l1_mix_norm)
    xt = _odd_layer_mixer(xt, h1, l1_w_in, l1_c_conv_w, l1_c_conv_b, l1_c_ln_g, l1_c_ln_b, l1_w_out)
    xt, _ = _moe(xt, l1_ffn_norm, l1_router_group_w, l1_router_group_b, l1_router_expert_w,
                 l1_router_expert_b, l1_w_gate, l1_w_up, l1_w_down)
    return xt.reshape(bsz, seq, d)
```

```python
import functools

import jax
import jax.numpy as jnp
from jax import lax
from jax.experimental import pallas as pl
from jax.experimental.pallas import tpu as pltpu

F32 = jnp.float32
BF16 = jnp.bfloat16

HEAD_DIM = 128
CHUNK = 128
EPS = 1e-6
N_GROUPS = 8
EXPERTS_PER_GROUP = 8
N_EXPERTS = N_GROUPS * EXPERTS_PER_GROUP
EXPERT_BLOCK = 128
CONV_WIDTH = 31
SUBLANES = 8
CONV_ROWS = 128
CONV_HALO = 32
ROUTE_LANES = 128
GMLP_ROWS = 512
SWA_KV_HEADS = 4
SB_LOG_ZERO = -104.0
SB_HEADS = 4
SB_QUERIES = 256
SB_KEYS = 384
COMBINE_ROWS = 16
COMBINE_COLS = 512
MOE_MAX_ROWS = 768
V7X_VMEM_LIMIT = 56 * 1024 * 1024
V7X_VMEM_LIMIT_MOE = 60 * 1024 * 1024


def _params(*sem):
    return pltpu.CompilerParams(dimension_semantics=sem, vmem_limit_bytes=V7X_VMEM_LIMIT)


def _rmsnorm_kernel(x_ref, g_ref, o_ref):
    x = x_ref[...]
    ms = jnp.mean(x * x, axis=-1, keepdims=True)
    o_ref[...] = (x * lax.rsqrt(ms + EPS) * g_ref[...]).astype(o_ref.dtype)


def _rmsnorm(x, g, tm=256):
    t, d = x.shape
    return pl.pallas_call(
        _rmsnorm_kernel,
        grid=(t // tm,),
        in_specs=[pl.BlockSpec((tm, d), lambda i: (i, 0)),
                  pl.BlockSpec((1, d), lambda i: (0, 0))],
        out_specs=pl.BlockSpec((tm, d), lambda i: (i, 0)),
        out_shape=jax.ShapeDtypeStruct((t, d), BF16),
        compiler_params=_params("parallel"),
        name="rmsnorm",
    )(x, g.reshape(1, d))


def _matmul_kernel(a_ref, w_ref, o_ref):
    o_ref[...] = jnp.dot(a_ref[...], w_ref[...], preferred_element_type=F32).astype(o_ref.dtype)


def _matmul(a, w, tm, tn):
    m, k = a.shape
    n = w.shape[1]
    tm = min(tm, m)
    return pl.pallas_call(
        _matmul_kernel,
        grid=(m // tm, n // tn),
        in_specs=[pl.BlockSpec((tm, k), lambda i, j: (i, 0)),
                  pl.BlockSpec((k, tn), lambda i, j: (0, j))],
        out_specs=pl.BlockSpec((tm, tn), lambda i, j: (i, j)),
        out_shape=jax.ShapeDtypeStruct((m, n), BF16),
        compiler_params=_params("parallel", "arbitrary"),
        name="in_proj",
    )(a, w)


def _out_proj_kernel(x_ref, a_ref, b_ref, wa_ref, wb_ref, o_ref):
    acc = jnp.dot(a_ref[...], wa_ref[...], preferred_element_type=F32)
    acc = acc + jnp.dot(b_ref[...], wb_ref[...], preferred_element_type=F32)
    o_ref[...] = x_ref[...] + acc


def _out_proj(x, ya, yb, w, tm=1024, tn=1024):
    t, d = x.shape
    ka, kb = ya.shape[1], yb.shape[1]
    assert ka == kb and w.shape == (ka + kb, d)
    tm = min(tm, t)
    return pl.pallas_call(
        _out_proj_kernel,
        grid=(t // tm, d // tn),
        in_specs=[pl.BlockSpec((tm, tn), lambda i, j: (i, j)),
                  pl.BlockSpec((tm, ka), lambda i, j: (i, 0)),
                  pl.BlockSpec((tm, kb), lambda i, j: (i, 0)),
                  pl.BlockSpec((ka, tn), lambda i, j: (0, j)),
                  pl.BlockSpec((kb, tn), lambda i, j: (1, j))],
        out_specs=pl.BlockSpec((tm, tn), lambda i, j: (i, j)),
        out_shape=jax.ShapeDtypeStruct((t, d), F32),
        compiler_params=_params("parallel", "arbitrary"),
        name="out_proj",
    )(x, ya, yb, w, w)


def _layer_norm(x, g, b):
    mu = jnp.mean(x, axis=-1, keepdims=True)
    xc = x - mu
    var = jnp.mean(xc * xc, axis=-1, keepdims=True)
    return xc * lax.rsqrt(var + EPS) * g + b


def _gelu(x):
    return 0.5 * x * (1.0 + lax.erf(x * (2.0 ** -0.5)))


def _gmlp_kernel(u_ref, v_ref, lng_ref, lnb_ref, w_ref, bs_ref, o_ref):
    n_groups = w_ref.shape[0]
    group_cols = [slice(g * HEAD_DIM, (g + 1) * HEAD_DIM) for g in range(n_groups)]
    for c in range(u_ref.shape[0] // CHUNK):
        rows = slice(c * CHUNK, (c + 1) * CHUNK)
        v = _gelu(v_ref[rows, :].astype(F32))
        vn = _layer_norm(v, lng_ref[...], lnb_ref[...]).astype(BF16)
        mixed = [jnp.dot(w_ref[g], vn[:, cols], preferred_element_type=F32) for g, cols in enumerate(group_cols)]
        for g, cols in enumerate(group_cols):
            u = _gelu(u_ref[rows, cols].astype(F32))
            o_ref[rows, cols] = (u * (mixed[g] + bs_ref[:, cols])).astype(o_ref.dtype)


def _gmlp(proj, ln_g, ln_b, w_s, b_s):
    t = proj.shape[0]
    n_groups = w_s.shape[0]
    width = n_groups * HEAD_DIM
    causal = jnp.tril(jnp.ones((CHUNK, CHUNK), dtype=bool))
    w = jnp.where(causal[None], w_s, 0.0).astype(BF16)
    bs = jnp.repeat(b_s.T.astype(F32), HEAD_DIM, axis=1)
    tm = min(GMLP_ROWS, t)
    return pl.pallas_call(
        _gmlp_kernel,
        grid=(t // tm,),
        in_specs=[pl.BlockSpec((tm, width), lambda i: (i, 0)),
                  pl.BlockSpec((tm, width), lambda i: (i, 1)),
                  pl.BlockSpec((1, width), lambda i: (0, 0)),
                  pl.BlockSpec((1, width), lambda i: (0, 0)),
                  pl.BlockSpec((n_groups, CHUNK, CHUNK), lambda i: (0, 0, 0)),
                  pl.BlockSpec((CHUNK, width), lambda i: (0, 0))],
        out_specs=pl.BlockSpec((tm, width), lambda i: (i, 0)),
        out_shape=jax.ShapeDtypeStruct((t, width), BF16),
        compiler_params=_params("parallel"),
        name="gmlp",
    )(proj, proj, ln_g.reshape(1, width), ln_b.reshape(1, width), w, bs)


def _swa_kernel(slopes_ref, sinks_ref, q_ref, kc_ref, kp_ref, vc_ref, vp_ref, qg_ref, kg_ref, o_ref, *, kvs, grp):
    n = pl.program_id(0)
    first_head = pl.program_id(1) * kvs * grp
    qi = lax.broadcasted_iota(jnp.int32, (CHUNK, 2 * CHUNK), 0)
    kj = lax.broadcasted_iota(jnp.int32, (CHUNK, 2 * CHUNK), 1)
    dist = qi + CHUNK - kj
    valid = (dist >= 0) & (dist < CHUNK) & (n * CHUNK - CHUNK + kj >= 0)
    distf = dist.astype(F32)
    head_cols = lambda j: slice(j * HEAD_DIM, (j + 1) * HEAD_DIM)
    scores, values = [], []
    for kv in range(kvs):
        k = jnp.concatenate([kp_ref[:, head_cols(kv)], kc_ref[:, head_cols(kv)]], axis=0).astype(F32)
        k = k * lax.rsqrt(jnp.mean(k * k, axis=-1, keepdims=True) + EPS) * kg_ref[...]
        kb = k.astype(BF16)
        values.append(jnp.concatenate([vp_ref[:, head_cols(kv)], vc_ref[:, head_cols(kv)]], axis=0))
        for g in range(grp):
            q = q_ref[:, head_cols(kv * grp + g)].astype(F32)
            q = q * lax.rsqrt(jnp.mean(q * q, axis=-1, keepdims=True) + EPS) * qg_ref[...]
            scores.append(lax.dot_general(q.astype(BF16), kb, (((1,), (1,)), ((), ())),
                                          preferred_element_type=F32) * (HEAD_DIM ** -0.5))
    probs, denoms = [], []
    for j in range(kvs * grp):
        slope = slopes_ref[first_head + j]
        sink = sinks_ref[first_head + j]
        s = jnp.where(valid, scores[j] - slope * distf, -jnp.inf)
        m = jnp.maximum(jnp.max(s, axis=-1, keepdims=True), sink)
        p = jnp.exp(s - m)
        denoms.append(jnp.sum(p, axis=-1, keepdims=True) + jnp.exp(sink - m))
        probs.append(p.astype(BF16))
    for j in range(kvs * grp):
        o = jnp.dot(probs[j], values[j // grp], preferred_element_type=F32) / denoms[j]
        o_ref[:, head_cols(j)] = o.astype(o_ref.dtype)


def _swa(proj, q_gain, k_gain, sinks, n_heads, n_kv, col0):
    t = proj.shape[0]
    grp = n_heads // n_kv
    kvs = min(SWA_KV_HEADS, n_kv)
    q_width, kv_width = kvs * grp * HEAD_DIM, kvs * HEAD_DIM
    assert col0 % q_width == 0 and (col0 + n_heads * HEAD_DIM) % kv_width == 0 and n_kv % kvs == 0
    qb0 = col0 // q_width
    kb0 = (col0 + n_heads * HEAD_DIM) // kv_width
    vb0 = kb0 + n_kv // kvs
    slopes = 2.0 ** (-8.0 * jnp.arange(1, n_heads + 1, dtype=F32) / n_heads)
    prev = lambda n: jnp.maximum(n - 1, 0)
    return pl.pallas_call(
        functools.partial(_swa_kernel, kvs=kvs, grp=grp),
        grid_spec=pltpu.PrefetchScalarGridSpec(
            num_scalar_prefetch=2,
            grid=(t // CHUNK, n_kv // kvs),
            in_specs=[pl.BlockSpec((CHUNK, q_width), lambda n, h, *_: (n, qb0 + h)),
                      pl.BlockSpec((CHUNK, kv_width), lambda n, h, *_: (n, kb0 + h)),
                      pl.BlockSpec((CHUNK, kv_width), lambda n, h, *_: (prev(n), kb0 + h)),
                      pl.BlockSpec((CHUNK, kv_width), lambda n, h, *_: (n, vb0 + h)),
                      pl.BlockSpec((CHUNK, kv_width), lambda n, h, *_: (prev(n), vb0 + h)),
                      pl.BlockSpec((1, HEAD_DIM), lambda n, h, *_: (0, 0)),
                      pl.BlockSpec((1, HEAD_DIM), lambda n, h, *_: (0, 0))],
            out_specs=pl.BlockSpec((CHUNK, q_width), lambda n, h, *_: (n, h)),
        ),
        out_shape=jax.ShapeDtypeStruct((t, n_heads * HEAD_DIM), BF16),
        compiler_params=_params("parallel", "arbitrary"),
        name="swa",
    )(slopes, sinks.astype(F32), proj, proj, proj, proj, proj,
      q_gain.reshape(1, HEAD_DIM).astype(F32), k_gain.reshape(1, HEAD_DIM).astype(F32))


def _conformer_kernel(a_ref, g_ref, ap_ref, gp_ref, w_ref, cb_ref, lg_ref, lb_ref, o_ref, hbuf, ybuf):
    i = pl.program_id(0)
    tc, width = a_ref.shape
    span = CONV_HALO + tc
    hp = ap_ref[...].astype(F32) * jax.nn.sigmoid(gp_ref[...].astype(F32))
    hbuf[0, 0:CONV_HALO, :] = jnp.where(i > 0, hp, 0.0)
    hbuf[0, CONV_HALO:span, :] = a_ref[...].astype(F32) * jax.nn.sigmoid(g_ref[...].astype(F32))
    hbuf[0, span:span + SUBLANES, :] = jnp.zeros((SUBLANES, width), F32)
    for r in range(1, SUBLANES):
        hbuf[r, 0:span, :] = hbuf[0, r:r + span, :]
    base = CONV_HALO - (CONV_WIDTH - 1)
    for c in range(width // HEAD_DIM):
        cols = slice(c * HEAD_DIM, (c + 1) * HEAD_DIM)
        for t0 in range(0, tc, CONV_ROWS):
            acc = jnp.zeros((CONV_ROWS, HEAD_DIM), F32)
            for r in range(SUBLANES):
                taps = [j for j in range(CONV_WIDTH) if (base + j) % SUBLANES == r]
                first = t0 + base + taps[0] - r
                rows = hbuf[r, first:t0 + base + taps[-1] - r + CONV_ROWS, cols]
                for j in taps:
                    off = t0 + base + j - r - first
                    acc = acc + w_ref[j:j + 1, cols] * rows[off:off + CONV_ROWS, :]
            ybuf[t0:t0 + CONV_ROWS, cols] = acc
    y = _layer_norm(ybuf[...] + cb_ref[...], lg_ref[...], lb_ref[...])
    o_ref[...] = (y * jax.nn.sigmoid(y)).astype(o_ref.dtype)


def _conformer(proj, conv_w, conv_b, ln_g, ln_b, tc=256):
    t = proj.shape[0]
    width = conv_w.shape[1]
    tc = min(tc, t)
    halo_blocks = tc // CONV_HALO
    prev = lambda i: jnp.maximum(i * halo_blocks - 1, 0)
    vec = lambda: pl.BlockSpec((1, width), lambda i: (0, 0))
    return pl.pallas_call(
        _conformer_kernel,
        grid=(t // tc,),
        in_specs=[pl.BlockSpec((tc, width), lambda i: (i, 0)),
                  pl.BlockSpec((tc, width), lambda i: (i, 1)),
                  pl.BlockSpec((CONV_HALO, width), lambda i: (prev(i), 0)),
                  pl.BlockSpec((CONV_HALO, width), lambda i: (prev(i), 1)),
                  pl.BlockSpec((CONV_WIDTH, width), lambda i: (0, 0)),
                  vec(), vec(), vec()],
        out_specs=pl.BlockSpec((tc, width), lambda i: (i, 0)),
        out_shape=jax.ShapeDtypeStruct((t, width), BF16),
        scratch_shapes=[pltpu.VMEM((SUBLANES, CONV_HALO + tc + SUBLANES, width), F32),
                        pltpu.VMEM((tc, width), F32)],
        compiler_params=_params("parallel"),
        name="conformer",
    )(proj, proj, proj, proj, conv_w.astype(F32), conv_b.reshape(1, width).astype(F32),
      ln_g.reshape(1, width).astype(F32), ln_b.reshape(1, width).astype(F32))


def _stickbreak_kernel(q_ref, k_ref, v_ref, tri_ref, o_ref, *, heads):
    i = pl.program_id(1)
    tq = CHUNK
    subs = q_ref.shape[0] // tq
    col = lax.broadcasted_iota(jnp.int32, (tq, SB_KEYS), 1)
    row = lax.broadcasted_iota(jnp.int32, (tq, 1), 0)
    head_cols = [slice(g * HEAD_DIM, (g + 1) * HEAD_DIM) for g in range(heads)]
    chains = [(s, g) for s in range(subs) for g in range(heads)]

    def walk_tile(walked, cs, accs):
        starts, valids = [], []
        for s in range(subs):
            block = i * subs + s
            upper = (block + 1) * tq - walked * SB_KEYS
            start = pl.multiple_of(jnp.maximum(upper - SB_KEYS, 0), tq)
            starts.append(start)
            valids.append(col < jnp.minimum(block * tq + row, upper) - start)
        zs = [lax.dot_general(q_ref[s * tq:(s + 1) * tq, head_cols[g]], k_ref[pl.ds(starts[s], SB_KEYS), head_cols[g]],
                              (((1,), (1,)), ((), ())), preferred_element_type=F32) * (HEAD_DIM ** -0.5)
              for s, g in chains]
        log_wins, splits, new_cs = [], [], []
        for n, (s, g) in enumerate(chains):
            z = zs[n]
            softplus = jnp.maximum(z, 0.0) + jnp.log(1.0 + jnp.exp(-jnp.abs(z)))
            log_fail = jnp.where(valids[s], -softplus, 0.0)
            hi = log_fail.astype(BF16)
            lo = (log_fail - hi.astype(F32)).astype(BF16)
            splits.append(jnp.concatenate([hi, lo], axis=1))
            log_wins.append(z - softplus)
            new_cs.append(cs[n] + jnp.sum(log_fail, axis=-1, keepdims=True))
        later = jnp.dot(jnp.concatenate(splits, axis=0), tri_ref[...], preferred_element_type=F32)
        new_accs = []
        for n, (s, g) in enumerate(chains):
            a = jnp.where(valids[s], jnp.exp(log_wins[n] + later[n * tq:(n + 1) * tq, :] + cs[n]), 0.0)
            new_accs.append(accs[n] + jnp.dot(a.astype(BF16), v_ref[pl.ds(starts[s], SB_KEYS), head_cols[g]],
                                              preferred_element_type=F32))
        return walked + 1, new_cs, new_accs

    def cond(state):
        walked, cs, _ = state
        alive = functools.reduce(jnp.maximum, [jnp.max(c) for c in cs])
        keys_left = (i + 1) * subs * tq - walked * SB_KEYS > 0
        return jnp.logical_and(keys_left, alive > SB_LOG_ZERO)

    def body(state):
        return walk_tile(*state)

    init = (0,
            [jnp.zeros((tq, 1), F32) for _ in chains],
            [jnp.zeros((tq, HEAD_DIM), F32) for _ in chains])
    _, _, accs = lax.while_loop(cond, body, init)
    for n, (s, g) in enumerate(chains):
        o_ref[s * tq:(s + 1) * tq, head_cols[g]] = accs[n].astype(o_ref.dtype)


def _stickbreak(proj, n_heads, col0):
    t = proj.shape[0]
    width = SB_HEADS * HEAD_DIM
    qb0 = col0 // width
    kb0 = qb0 + n_heads // SB_HEADS
    vb0 = kb0 + n_heads // SB_HEADS
    s_later = jnp.arange(2 * SB_KEYS)[:, None] % SB_KEYS
    tri = (s_later > jnp.arange(SB_KEYS)[None, :]).astype(BF16)
    resident = lambda c0: pl.BlockSpec((t, width), lambda h, i: (0, c0 + h), pipeline_mode=pl.Buffered(1))
    return pl.pallas_call(
        functools.partial(_stickbreak_kernel, heads=SB_HEADS),
        grid=(n_heads // SB_HEADS, t // SB_QUERIES),
        in_specs=[pl.BlockSpec((SB_QUERIES, width), lambda h, i: (i, qb0 + h)),
                  resident(kb0), resident(vb0),
                  pl.BlockSpec((2 * SB_KEYS, SB_KEYS), lambda h, i: (0, 0))],
        out_specs=pl.BlockSpec((SB_QUERIES, width), lambda h, i: (i, h)),
        out_shape=jax.ShapeDtypeStruct((t, n_heads * HEAD_DIM), BF16),
        compiler_params=_params("parallel", "arbitrary"),
        name="stickbreak",
    )(proj, proj, proj, tri)


def _router_kernel(x_ref, g_ref, whi_ref, wlo_ref, b_ref, hp_ref, route_ref, counts_ref, carry):
    @pl.when(pl.program_id(0) == 0)
    def _():
        carry[...] = jnp.zeros_like(carry)

    x = x_ref[...]
    tm, d = x.shape
    h = x * lax.rsqrt(jnp.mean(x * x, axis=-1, keepdims=True) + EPS) * g_ref[...]
    h_hi = h.astype(BF16)
    h_hif = h_hi.astype(F32)
    h_lo = (h - h_hif).astype(BF16)
    logits = (jnp.dot(h_hi, whi_ref[...], preferred_element_type=F32)
              + jnp.dot(h_lo, whi_ref[...], preferred_element_type=F32)
              + jnp.dot(h_hi, wlo_ref[...], preferred_element_type=F32)) + b_ref[...]
    lane = lax.broadcasted_iota(jnp.int32, logits.shape, 1)
    neg = -jnp.inf

    def first_max(vals):
        m = jnp.max(vals, axis=-1, keepdims=True)
        idx = jnp.min(jnp.where(vals == m, lane, ROUTE_LANES), axis=-1, keepdims=True)
        return m, idx

    gl = jnp.where(lane < N_GROUPS, logits, neg)
    gmax, g_sel = first_max(gl)
    g_weight = 1.0 / jnp.sum(jnp.exp(gl - gmax), axis=-1, keepdims=True)
    expert = lane - N_GROUPS
    in_group = (expert >= 0) & (expert < N_EXPERTS) & ((expert >> 3) == g_sel)
    el = jnp.where(in_group, logits, neg)
    m1, i1 = first_max(el)
    m2, i2 = first_max(jnp.where(lane == i1, neg, el))
    e21 = jnp.exp(m2 - m1)
    w1 = g_weight / (1.0 + e21)
    w2 = w1 * e21
    sel1 = lane == i1
    sel2 = lane == i2
    chosen = jnp.where(jnp.logical_or(sel1, sel2), 1.0, 0.0)
    row = lax.broadcasted_iota(jnp.int32, (tm, tm), 0)
    col = lax.broadcasted_iota(jnp.int32, (tm, tm), 1)
    earlier = jnp.where(col < row, 1.0, 0.0).astype(BF16)
    before = jnp.dot(earlier, chosen.astype(BF16), preferred_element_type=F32) + carry[...]
    rank1 = jnp.sum(jnp.where(sel1, before, 0.0), axis=-1, keepdims=True)
    rank2 = jnp.sum(jnp.where(sel2, before, 0.0), axis=-1, keepdims=True)
    carry[...] = carry[...] + jnp.sum(chosen, axis=0, keepdims=True)
    counts_ref[...] = carry[...]
    fields = ((i1 - N_GROUPS).astype(F32), (i2 - N_GROUPS).astype(F32), w1, w2, rank1, rank2)
    rec = jnp.zeros_like(logits)
    for k, val in enumerate(fields):
        rec = jnp.where(lane == k, val, rec)
    route_ref[...] = rec
    bits = lax.bitcast_convert_type(h_hif, jnp.uint32)
    hp_ref[...] = (bits[:, :d // 2] >> 16) | (bits[:, d // 2:] & jnp.uint32(0xFFFF0000))


def _router(x, norm_g, wr_g, br_g, wr_e, br_e, tm=512):
    t, d = x.shape
    pad = ROUTE_LANES - N_GROUPS - N_EXPERTS
    w = jnp.concatenate([wr_g.astype(F32), wr_e.astype(F32), jnp.zeros((d, pad), F32)], axis=1)
    b = jnp.concatenate([br_g.astype(F32), br_e.astype(F32), jnp.zeros((pad,), F32)]).reshape(1, ROUTE_LANES)
    w_hi = w.astype(BF16)
    w_lo = (w - w_hi.astype(F32)).astype(BF16)
    tm = min(tm, t)
    return pl.pallas_call(
        _router_kernel,
        grid=(t // tm,),
        in_specs=[pl.BlockSpec((tm, d), lambda i: (i, 0)),
                  pl.BlockSpec((1, d), lambda i: (0, 0)),
                  pl.BlockSpec((d, ROUTE_LANES), lambda i: (0, 0)),
                  pl.BlockSpec((d, ROUTE_LANES), lambda i: (0, 0)),
                  pl.BlockSpec((1, ROUTE_LANES), lambda i: (0, 0))],
        out_specs=[pl.BlockSpec((tm, d // 2), lambda i: (i, 0)),
                   pl.BlockSpec((tm, ROUTE_LANES), lambda i: (i, 0)),
                   pl.BlockSpec((1, ROUTE_LANES), lambda i: (0, 0))],
        out_shape=[jax.ShapeDtypeStruct((t, d // 2), jnp.uint32),
                   jax.ShapeDtypeStruct((t, ROUTE_LANES), F32),
                   jax.ShapeDtypeStruct((1, ROUTE_LANES), F32)],
        scratch_shapes=[pltpu.VMEM((1, ROUTE_LANES), F32)],
        compiler_params=_params("arbitrary"),
        name="router",
    )(x, norm_g.reshape(1, d), w_hi, w_lo, b)


def _lookup(table, idx):
    hit = idx[..., None] == jnp.arange(table.shape[0], dtype=jnp.int32)
    return jnp.sum(jnp.where(hit, table, 0), axis=-1)


def _dispatch_tables(route, counts, t):
    counts = counts[0, N_GROUPS:N_GROUPS + N_EXPERTS].astype(jnp.int32)
    padded = (counts + EXPERT_BLOCK - 1) // EXPERT_BLOCK * EXPERT_BLOCK
    padded_end = jnp.cumsum(padded)
    padded_start = padded_end - padded
    expert = route[:, 0:2].astype(jnp.int32)
    rank = route[:, 4:6].astype(jnp.int32)
    assign_slot = (_lookup(padded_start, expert) + rank).reshape(-1)

    n_slots = (-(-2 * t // EXPERT_BLOCK) + N_EXPERTS) * EXPERT_BLOCK
    n_items = N_EXPERTS + n_slots // MOE_MAX_ROWS
    chunks = (padded + MOE_MAX_ROWS - 1) // MOE_MAX_ROWS
    chunk_end = jnp.cumsum(chunks)
    n_used = chunk_end[-1]
    ids = jnp.arange(n_items, dtype=jnp.int32)
    item_e = jnp.minimum(jnp.sum(chunk_end[None, :] <= ids[:, None], axis=1), N_EXPERTS - 1).astype(jnp.int32)
    j = ids - _lookup(chunk_end - chunks, item_e)
    item_start = _lookup(padded_start, item_e) + j * MOE_MAX_ROWS
    item_rows = jnp.clip(_lookup(padded, item_e) - j * MOE_MAX_ROWS, 0, MOE_MAX_ROWS)
    used = ids < n_used
    last_e = jnp.sum(jnp.where(ids == n_used - 1, item_e, 0))
    item_e = jnp.where(used, item_e, last_e).astype(jnp.int32)
    item_start = jnp.where(used, item_start, 0).astype(jnp.int32)
    item_rows = jnp.where(used, item_rows, 0).astype(jnp.int32)
    total = padded_end[-1].reshape(1).astype(jnp.int32)
    return (assign_slot.astype(jnp.int32), padded_start.astype(jnp.int32), padded.astype(jnp.int32), total,
            item_e, item_start, item_rows, n_slots, n_items)


def _dispatch_kernel(as_ref, ps_ref, pd_ref, tot_ref, hp_ref, xs_hbm, zbuf, sem, zsem):
    i = pl.program_id(0)
    tm = hp_ref.shape[0]
    n_blocks = xs_hbm.shape[0] // EXPERT_BLOCK

    def zero_block(first_row):
        return pltpu.make_async_copy(zbuf, xs_hbm.at[pl.ds(pl.multiple_of(first_row, EXPERT_BLOCK), EXPERT_BLOCK)],
                                     zsem)

    def zero_fill(act):
        def per_expert(e, carry):
            @pl.when(pd_ref[e] > 0)
            def _():
                act(zero_block(ps_ref[e] + pd_ref[e] - EXPERT_BLOCK))
            return carry
        lax.fori_loop(0, N_EXPERTS, per_expert, 0)

        def per_tail_block(b, carry):
            act(zero_block(b * EXPERT_BLOCK))
            return carry
        lax.fori_loop(tot_ref[0] // EXPERT_BLOCK, n_blocks, per_tail_block, 0)

    @pl.when(i == 0)
    def _():
        zbuf[...] = jnp.zeros_like(zbuf)
        zero_fill(lambda cp: cp.start())
        zero_fill(lambda cp: cp.wait())

    def body(r, carry):
        a = 2 * (i * tm + r)
        for k in range(2):
            pltpu.make_async_copy(hp_ref.at[pl.ds(r, 1)], xs_hbm.at[pl.ds(as_ref[a + k], 1)], sem).start()
        return carry
    lax.fori_loop(0, tm, body, 0, unroll=8)
    for k in range(2):
        pltpu.make_async_copy(hp_ref, xs_hbm.at[pl.ds(0, tm)], sem).wait()


def _dispatch(hp, assign_slot, padded_start, padded, total, n_slots, tm=512):
    t, words = hp.shape
    tm = min(tm, t)
    return pl.pallas_call(
        _dispatch_kernel,
        grid_spec=pltpu.PrefetchScalarGridSpec(
            num_scalar_prefetch=4,
            grid=(t // tm,),
            in_specs=[pl.BlockSpec((tm, words), lambda i, *_: (i, 0))],
            out_specs=pl.BlockSpec(memory_space=pl.ANY),
            scratch_shapes=[pltpu.VMEM((EXPERT_BLOCK, words), hp.dtype),
                            pltpu.SemaphoreType.DMA(()),
                            pltpu.SemaphoreType.DMA(())],
        ),
        out_shape=jax.ShapeDtypeStruct((n_slots, words), hp.dtype),
        compiler_params=_params("arbitrary"),
        name="moe_dispatch",
    )(assign_slot, padded_start, padded, total, hp)


def _unpack_pairs(words):
    lo = lax.bitcast_convert_type(words << 16, F32)
    hi = lax.bitcast_convert_type(words & jnp.uint32(0xFFFF0000), F32)
    return lo, hi


def _pack_pairs(lo, hi):
    lo_bits = lax.bitcast_convert_type(lo.astype(BF16).astype(F32), jnp.uint32)
    hi_bits = lax.bitcast_convert_type(hi.astype(BF16).astype(F32), jnp.uint32)
    return (lo_bits >> 16) | (hi_bits & jnp.uint32(0xFFFF0000))


def _expert_items_kernel(ie_ref, is_ref, ir_ref, tot_ref, in_hbm, *refs, n_weights, cast_weights, compute):
    w_refs = refs[:n_weights]
    out_hbm = refs[n_weights]
    inbuf, outbuf, zbuf, w_bf, sem_in, sem_out, sem_z = refs[n_weights + 1:]
    i = pl.program_id(0)
    n_items = pl.num_programs(0)
    max_blocks = inbuf.shape[1] // EXPERT_BLOCK
    n_out_blocks = out_hbm.shape[0] // EXPERT_BLOCK

    def rows_of(hbm, first_row):
        return hbm.at[pl.ds(pl.multiple_of(first_row, EXPERT_BLOCK), EXPERT_BLOCK)]

    def for_blocks(item, act):
        for c in range(max_blocks):
            @pl.when(c * EXPERT_BLOCK < ir_ref[item])
            def _():
                act(c, is_ref[item] + c * EXPERT_BLOCK)

    def in_copy(buf, act):
        return lambda c, row: act(pltpu.make_async_copy(
            rows_of(in_hbm, row), inbuf.at[buf, pl.ds(c * EXPERT_BLOCK, EXPERT_BLOCK)], sem_in.at[buf]))

    def out_copy(buf, act):
        return lambda c, row: act(pltpu.make_async_copy(
            outbuf.at[buf, pl.ds(c * EXPERT_BLOCK, EXPERT_BLOCK)], rows_of(out_hbm, row), sem_out.at[buf]))

    start = lambda cp: cp.start()
    wait = lambda cp: cp.wait()
    buf = i % 2

    @pl.when(i == 0)
    def _():
        for_blocks(0, in_copy(0, start))
        zbuf[...] = jnp.zeros_like(zbuf)

        def tail(act):
            def body(b, carry):
                act(pltpu.make_async_copy(zbuf, rows_of(out_hbm, b * EXPERT_BLOCK), sem_z))
                return carry
            lax.fori_loop(tot_ref[0] // EXPERT_BLOCK, n_out_blocks, body, 0)
        tail(start)
        tail(wait)

    @pl.when(i + 1 < n_items)
    def _():
        for_blocks(i + 1, in_copy(1 - buf, start))

    for_blocks(i, in_copy(buf, wait))

    @pl.when(jnp.logical_or(i == 0, ie_ref[i] != ie_ref[jnp.maximum(i - 1, 0)]))
    def _():
        cast_weights(w_refs, w_bf)

    @pl.when(i >= 2)
    def _():
        for_blocks(i - 2, out_copy(buf, wait))

    n_blk = ir_ref[i] // EXPERT_BLOCK
    for c in range(1, max_blocks + 1):
        @pl.when(n_blk == c)
        def _():
            rows = c * EXPERT_BLOCK
            outbuf[buf, 0:rows, :] = compute(inbuf[buf, 0:rows, :], w_bf)

    for_blocks(i, out_copy(buf, start))

    @pl.when(i == n_items - 1)
    def _():
        @pl.when(i >= 1)
        def _():
            for_blocks(i - 1, out_copy(1 - buf, wait))
        for_blocks(i, out_copy(buf, wait))


def _expert_items_call(name, items, in_rows, weights, out_width, out_dtype, n_w_cols, cast_weights, compute):
    item_e, item_start, item_rows, total = items
    n_slots, in_width = in_rows.shape
    k_dim = weights[0].shape[1]
    w_spec = lambda w: pl.BlockSpec((None,) + w.shape[1:], lambda i, ie, *_: (ie[i], 0, 0))
    return pl.pallas_call(
        functools.partial(_expert_items_kernel, n_weights=len(weights), cast_weights=cast_weights, compute=compute),
        grid_spec=pltpu.PrefetchScalarGridSpec(
            num_scalar_prefetch=4,
            grid=(item_e.shape[0],),
            in_specs=[pl.BlockSpec(memory_space=pl.ANY)] + [w_spec(w) for w in weights],
            out_specs=pl.BlockSpec(memory_space=pl.ANY),
            scratch_shapes=[pltpu.VMEM((2, MOE_MAX_ROWS, in_width), in_rows.dtype),
                            pltpu.VMEM((2, MOE_MAX_ROWS, out_width), out_dtype),
                            pltpu.VMEM((EXPERT_BLOCK, out_width), out_dtype),
                            pltpu.VMEM((k_dim, n_w_cols), BF16),
                            pltpu.SemaphoreType.DMA((2,)),
                            pltpu.SemaphoreType.DMA((2,)),
                            pltpu.SemaphoreType.DMA(())],
        ),
        out_shape=jax.ShapeDtypeStruct((n_slots, out_width), out_dtype),
        compiler_params=pltpu.CompilerParams(dimension_semantics=("arbitrary",),
                                             vmem_limit_bytes=V7X_VMEM_LIMIT_MOE),
        name=name,
    )(item_e, item_start, item_rows, total, in_rows, *weights)


def _cast_gate_up(w_refs, w_bf):
    d_exp = w_refs[0].shape[1]
    w_bf[:, 0:d_exp] = w_refs[0][...].astype(BF16)
    w_bf[:, d_exp:2 * d_exp] = w_refs[1][...].astype(BF16)


def _gate_up(words, w_bf):
    half = words.shape[1]
    d_exp = w_bf.shape[1] // 2
    lo, hi = _unpack_pairs(words)
    gu = (jnp.dot(lo.astype(BF16), w_bf[0:half, :], preferred_element_type=F32)
          + jnp.dot(hi.astype(BF16), w_bf[half:2 * half, :], preferred_element_type=F32))
    gate, up = gu[:, 0:d_exp], gu[:, d_exp:2 * d_exp]
    return (gate * jax.nn.sigmoid(gate) * up).astype(BF16)


def _cast_down(w_refs, w_bf):
    w_bf[...] = w_refs[0][...].astype(BF16)


def _down(hid, w_bf):
    half = w_bf.shape[1] // 2
    y = jnp.dot(hid, w_bf[...], preferred_element_type=F32)
    return _pack_pairs(y[:, 0:half], y[:, half:2 * half])


def _combine_kernel(as_ref, x_ref, route_ref, g_ref, y_hbm, o_ref, *rest, emit_norm):
    if emit_norm:
        h_ref, ybuf, sem = rest
    else:
        ybuf, sem = rest
    i = pl.program_id(0)
    tm, d = x_ref.shape
    half = d // 2

    def gather_row(tile, r, buf):
        a = 2 * (tile * tm + r)
        for k in range(2):
            pltpu.make_async_copy(y_hbm.at[pl.ds(as_ref[a + k], 1)], ybuf.at[buf, k, pl.ds(r, 1)],
                                  sem.at[buf]).start()

    def wait_gather(buf):
        for k in range(2):
            pltpu.make_async_copy(y_hbm.at[pl.ds(0, tm)], ybuf.at[buf, k], sem.at[buf]).wait()

    @pl.when(i == 0)
    def _():
        def body(r, carry):
            gather_row(0, r, 0)
            return carry
        lax.fori_loop(0, tm, body, 0, unroll=8)

    buf = i % 2
    last = pl.num_programs(0) - 1
    next_tile = jnp.minimum(i + 1, last)
    wait_gather(buf)

    def row_chunk(c, carry):
        for rr in range(COMBINE_ROWS):
            gather_row(next_tile, c * COMBINE_ROWS + rr, 1 - buf)
        rows = pl.ds(pl.multiple_of(c * COMBINE_ROWS, COMBINE_ROWS), COMBINE_ROWS)
        w1 = route_ref[rows, 2:3]
        w2 = route_ref[rows, 3:4]
        sq = jnp.zeros((COMBINE_ROWS, 1), F32)
        for c0 in range(0, half, COMBINE_COLS):
            cols = slice(c0, c0 + COMBINE_COLS)
            up_cols = slice(half + c0, half + c0 + COMBINE_COLS)
            lo1, hi1 = _unpack_pairs(ybuf[buf, 0, rows, cols])
            lo2, hi2 = _unpack_pairs(ybuf[buf, 1, rows, cols])
            out_lo = x_ref[rows, cols] + w1 * lo1 + w2 * lo2
            out_hi = x_ref[rows, up_cols] + w1 * hi1 + w2 * hi2
            o_ref[rows, cols] = out_lo
            o_ref[rows, up_cols] = out_hi
            if emit_norm:
                sq = sq + jnp.sum(out_lo * out_lo + out_hi * out_hi, axis=-1, keepdims=True)
        if emit_norm:
            inv = lax.rsqrt(sq * (1.0 / d) + EPS)
            for c0 in range(0, d, COMBINE_COLS):
                cols = slice(c0, c0 + COMBINE_COLS)
                h_ref[rows, cols] = (o_ref[rows, cols] * inv * g_ref[:, cols]).astype(h_ref.dtype)
        return carry
    lax.fori_loop(0, tm // COMBINE_ROWS, row_chunk, 0, unroll=4)

    @pl.when(i == last)
    def _():
        wait_gather(1 - buf)


def _moe(x, norm_g, wr_g, br_g, wr_e, br_e, w_gate, w_up, w_down, next_norm_g=None, tm_combine=256):
    t, d = x.shape
    d_exp = w_gate.shape[2]
    hp, route, counts = _router(x, norm_g, wr_g, br_g, wr_e, br_e)
    (assign_slot, padded_start, padded, total, item_e, item_start, item_rows,
     n_slots, _) = _dispatch_tables(route, counts, t)
    items = (item_e, item_start, item_rows, total)
    xs = _dispatch(hp, assign_slot, padded_start, padded, total, n_slots)
    hid = _expert_items_call("moe_up", items, xs, (w_gate, w_up), d_exp, BF16, 2 * d_exp,
                             _cast_gate_up, _gate_up)
    y = _expert_items_call("moe_down", items, hid, (w_down,), d // 2, jnp.uint32, d, _cast_down, _down)

    tm = min(tm_combine, t)
    emit_norm = next_norm_g is not None
    row_tile = pl.BlockSpec((tm, d), lambda i, a: (i, 0))
    gain = next_norm_g if emit_norm else jnp.ones((d,), F32)
    out = pl.pallas_call(
        functools.partial(_combine_kernel, emit_norm=emit_norm),
        grid_spec=pltpu.PrefetchScalarGridSpec(
            num_scalar_prefetch=1,
            grid=(t // tm,),
            in_specs=[row_tile,
                      pl.BlockSpec((tm, ROUTE_LANES), lambda i, a: (i, 0)),
                      pl.BlockSpec((1, d), lambda i, a: (0, 0)),
                      pl.BlockSpec(memory_space=pl.ANY)],
            out_specs=[row_tile, row_tile] if emit_norm else row_tile,
            scratch_shapes=[pltpu.VMEM((2, 2, tm, d // 2), jnp.uint32),
                            pltpu.SemaphoreType.DMA((2,))],
        ),
        out_shape=([jax.ShapeDtypeStruct((t, d), F32), jax.ShapeDtypeStruct((t, d), BF16)] if emit_norm
                   else jax.ShapeDtypeStruct((t, d), F32)),
        compiler_params=_params("arbitrary"),
        name="moe_combine",
    )(assign_slot, x, route, gain.reshape(1, d).astype(F32), y)
    return tuple(out) if emit_norm else (out, None)


def _even_layer_mixer(x, mix_norm, w_in, a_ln_g, a_ln_b, a_ws, a_bs, b_q_norm, b_k_norm, b_sinks, w_out):
    n_groups = a_ws.shape[0]
    a_width = n_groups * HEAD_DIM
    n_heads = b_sinks.shape[0]
    n_kv = (w_in.shape[1] - 2 * a_width - n_heads * HEAD_DIM) // (2 * HEAD_DIM)
    h = _rmsnorm(x, mix_norm)
    proj = _matmul(h, w_in.astype(BF16), tm=1024, tn=1024)
    y_a = _gmlp(proj, a_ln_g, a_ln_b, a_ws, a_bs)
    y_b = _swa(proj, b_q_norm, b_k_norm, b_sinks, n_heads, n_kv, col0=2 * a_width)
    return _out_proj(x, y_a, y_b, w_out.astype(BF16))


def _odd_layer_mixer(x, h, w_in, c_conv_w, c_conv_b, c_ln_g, c_ln_b, w_out):
    c_width = c_conv_w.shape[1]
    n_heads = (w_in.shape[1] - 2 * c_width) // (3 * HEAD_DIM)
    proj = _matmul(h, w_in.astype(BF16), tm=1024, tn=1024)
    y_c = _conformer(proj, c_conv_w, c_conv_b, c_ln_g, c_ln_b)
    y_d = _stickbreak(proj, n_heads, col0=2 * c_width)
    return _out_proj(x, y_c, y_d, w_out.astype(BF16))


def kernel(x, l0_mix_norm, l0_w_in, l0_a_ln_g, l0_a_ln_b, l0_a_ws, l0_a_bs, l0_b_q_norm, l0_b_k_norm, l0_b_sinks, l0_w_out, l0_ffn_norm, l0_router_group_w, l0_router_group_b, l0_router_expert_w, l0_router_expert_b, l0_w_gate, l0_w_up, l0_w_down, l1_mix_norm, l1_w_in, l1_c_conv_w, l1_c_conv_b, l1_c_ln_g, l1_c_ln_b, l1_w_out, l1_ffn_norm, l1_router_group_w, l1_router_group_b, l1_router_expert_w, l1_router_expert_b, l1_w_gate, l1_w_up, l1_w_down):
    bsz, seq, d = x.shape
    xt = x.reshape(bsz * seq, d)
    assert bsz == 1, "the mixers index positions by row; one sequence per call"
    xt = _even_layer_mixer(xt, l0_mix_norm, l0_w_in, l0_a_ln_g, l0_a_ln_b, l0_a_ws, l0_a_bs,
                           l0_b_q_norm, l0_b_k_norm, l0_b_sinks, l0_w_out)
    xt, h1 = _moe(xt, l0_ffn_norm, l0_router_group_w, l0_router_group_b, l0_router_expert_w,
                  l0_router_expert_b, l0_w_gate, l0_w_up, l0_w_down, next_norm_g=l1_mix_norm)
    xt = _odd_layer_mixer(xt, h1, l1_w_in, l1_c_conv_w, l1_c_conv_b, l1_c_ln_g, l1_c_ln_b, l1_w_out)
    xt, _ = _moe(xt, l1_ffn_norm, l1_router_group_w, l1_router_group_b, l1_router_expert_w,
                 l1_router_expert_b, l1_w_gate, l1_w_up, l1_w_down)
    return xt.reshape(bsz, seq, d)
```

```python
import functools

import jax
import jax.numpy as jnp
from jax import lax
from jax.experimental import pallas as pl
from jax.experimental.pallas import tpu as pltpu

F32 = jnp.float32
BF16 = jnp.bfloat16

HEAD_DIM = 128
CHUNK = 128
EPS = 1e-6
N_GROUPS = 8
EXPERTS_PER_GROUP = 8
N_EXPERTS = N_GROUPS * EXPERTS_PER_GROUP
EXPERT_BLOCK = 128
CONV_WIDTH = 31
SUBLANES = 8
CONV_ROWS = 128
CONV_HALO = 32
ROUTE_LANES = 128
GMLP_ROWS = 512
SWA_KV_HEADS = 4
SB_LOG_ZERO = -104.0
SB_HEADS = 4
SB_QUERIES = 512
SB_KEYS = 384
COMBINE_ROWS = 16
COMBINE_COLS = 512
MOE_MAX_ROWS = 768
V7X_VMEM_LIMIT = 56 * 1024 * 1024
V7X_VMEM_LIMIT_MOE = 60 * 1024 * 1024


def _params(*sem):
    return pltpu.CompilerParams(dimension_semantics=sem, vmem_limit_bytes=V7X_VMEM_LIMIT)


def _rmsnorm_kernel(x_ref, g_ref, o_ref):
    x = x_ref[...]
    ms = jnp.mean(x * x, axis=-1, keepdims=True)
    o_ref[...] = (x * lax.rsqrt(ms + EPS) * g_ref[...]).astype(o_ref.dtype)


def _rmsnorm(x, g, tm=256):
    t, d = x.shape
    return pl.pallas_call(
        _rmsnorm_kernel,
        grid=(t // tm,),
        in_specs=[pl.BlockSpec((tm, d), lambda i: (i, 0)),
                  pl.BlockSpec((1, d), lambda i: (0, 0))],
        out_specs=pl.BlockSpec((tm, d), lambda i: (i, 0)),
        out_shape=jax.ShapeDtypeStruct((t, d), BF16),
        compiler_params=_params("parallel"),
        name="rmsnorm",
    )(x, g.reshape(1, d))


def _matmul_kernel(a_ref, w_ref, o_ref):
    o_ref[...] = jnp.dot(a_ref[...], w_ref[...], preferred_element_type=F32).astype(o_ref.dtype)


def _matmul(a, w, tm, tn):
    m, k = a.shape
    n = w.shape[1]
    tm = min(tm, m)
    return pl.pallas_call(
        _matmul_kernel,
        grid=(m // tm, n // tn),
        in_specs=[pl.BlockSpec((tm, k), lambda i, j: (i, 0)),
                  pl.BlockSpec((k, tn), lambda i, j: (0, j))],
        out_specs=pl.BlockSpec((tm, tn), lambda i, j: (i, j)),
        out_shape=jax.ShapeDtypeStruct((m, n), BF16),
        compiler_params=_params("parallel", "arbitrary"),
        name="in_proj",
    )(a, w)


def _out_proj_kernel(x_ref, a_ref, b_ref, wa_ref, wb_ref, o_ref):
    acc = jnp.dot(a_ref[...], wa_ref[...], preferred_element_type=F32)
    acc = acc + jnp.dot(b_ref[...], wb_ref[...], preferred_element_type=F32)
    o_ref[...] = x_ref[...] + acc


def _out_proj(x, ya, yb, w, tm=1024, tn=1024):
    t, d = x.shape
    ka, kb = ya.shape[1], yb.shape[1]
    assert ka == kb and w.shape == (ka + kb, d)
    tm = min(tm, t)
    return pl.pallas_call(
        _out_proj_kernel,
        grid=(t // tm, d // tn),
        in_specs=[pl.BlockSpec((tm, tn), lambda i, j: (i, j)),
                  pl.BlockSpec((tm, ka), lambda i, j: (i, 0)),
                  pl.BlockSpec((tm, kb), lambda i, j: (i, 0)),
                  pl.BlockSpec((ka, tn), lambda i, j: (0, j)),
                  pl.BlockSpec((kb, tn), lambda i, j: (1, j))],
        out_specs=pl.BlockSpec((tm, tn), lambda i, j: (i, j)),
        out_shape=jax.ShapeDtypeStruct((t, d), F32),
        compiler_params=_params("parallel", "arbitrary"),
        name="out_proj",
    )(x, ya, yb, w, w)


def _layer_norm(x, g, b):
    mu = jnp.mean(x, axis=-1, keepdims=True)
    xc = x - mu
    var = jnp.mean(xc * xc, axis=-1, keepdims=True)
    return xc * lax.rsqrt(var + EPS) * g + b


def _gelu(x):
    return 0.5 * x * (1.0 + lax.erf(x * (2.0 ** -0.5)))


def _gmlp_kernel(u_ref, v_ref, lng_ref, lnb_ref, w_ref, bs_ref, o_ref):
    n_groups = w_ref.shape[0]
    group_cols = [slice(g * HEAD_DIM, (g + 1) * HEAD_DIM) for g in range(n_groups)]
    for c in range(u_ref.shape[0] // CHUNK):
        rows = slice(c * CHUNK, (c + 1) * CHUNK)
        v = _gelu(v_ref[rows, :].astype(F32))
        vn = _layer_norm(v, lng_ref[...], lnb_ref[...]).astype(BF16)
        mixed = [jnp.dot(w_ref[g], vn[:, cols], preferred_element_type=F32) for g, cols in enumerate(group_cols)]
        for g, cols in enumerate(group_cols):
            u = _gelu(u_ref[rows, cols].astype(F32))
            o_ref[rows, cols] = (u * (mixed[g] + bs_ref[:, cols])).astype(o_ref.dtype)


def _gmlp(proj, ln_g, ln_b, w_s, b_s):
    t = proj.shape[0]
    n_groups = w_s.shape[0]
    width = n_groups * HEAD_DIM
    causal = jnp.tril(jnp.ones((CHUNK, CHUNK), dtype=bool))
    w = jnp.where(causal[None], w_s, 0.0).astype(BF16)
    bs = jnp.repeat(b_s.T.astype(F32), HEAD_DIM, axis=1)
    tm = min(GMLP_ROWS, t)
    return pl.pallas_call(
        _gmlp_kernel,
        grid=(t // tm,),
        in_specs=[pl.BlockSpec((tm, width), lambda i: (i, 0)),
                  pl.BlockSpec((tm, width), lambda i: (i, 1)),
                  pl.BlockSpec((1, width), lambda i: (0, 0)),
                  pl.BlockSpec((1, width), lambda i: (0, 0)),
                  pl.BlockSpec((n_groups, CHUNK, CHUNK), lambda i: (0, 0, 0)),
                  pl.BlockSpec((CHUNK, width), lambda i: (0, 0))],
        out_specs=pl.BlockSpec((tm, width), lambda i: (i, 0)),
        out_shape=jax.ShapeDtypeStruct((t, width), BF16),
        compiler_params=_params("parallel"),
        name="gmlp",
    )(proj, proj, ln_g.reshape(1, width), ln_b.reshape(1, width), w, bs)


def _swa_kernel(slopes_ref, sinks_ref, q_ref, kc_ref, kp_ref, vc_ref, vp_ref, qg_ref, kg_ref, o_ref, *, kvs, grp):
    n = pl.program_id(0)
    first_head = pl.program_id(1) * kvs * grp
    qi = lax.broadcasted_iota(jnp.int32, (CHUNK, 2 * CHUNK), 0)
    kj = lax.broadcasted_iota(jnp.int32, (CHUNK, 2 * CHUNK), 1)
    dist = qi + CHUNK - kj
    valid = (dist >= 0) & (dist < CHUNK) & (n * CHUNK - CHUNK + kj >= 0)
    distf = dist.astype(F32)
    head_cols = lambda j: slice(j * HEAD_DIM, (j + 1) * HEAD_DIM)
    scores, values = [], []
    for kv in range(kvs):
        k = jnp.concatenate([kp_ref[:, head_cols(kv)], kc_ref[:, head_cols(kv)]], axis=0).astype(F32)
        k = k * lax.rsqrt(jnp.mean(k * k, axis=-1, keepdims=True) + EPS) * kg_ref[...]
        kb = k.astype(BF16)
        values.append(jnp.concatenate([vp_ref[:, head_cols(kv)], vc_ref[:, head_cols(kv)]], axis=0))
        for g in range(grp):
            q = q_ref[:, head_cols(kv * grp + g)].astype(F32)
            q = q * lax.rsqrt(jnp.mean(q * q, axis=-1, keepdims=True) + EPS) * qg_ref[...]
            scores.append(lax.dot_general(q.astype(BF16), kb, (((1,), (1,)), ((), ())),
                                          preferred_element_type=F32) * (HEAD_DIM ** -0.5))
    probs, denoms = [], []
    for j in range(kvs * grp):
        slope = slopes_ref[first_head + j]
        sink = sinks_ref[first_head + j]
        s = jnp.where(valid, scores[j] - slope * distf, -jnp.inf)
        m = jnp.maximum(jnp.max(s, axis=-1, keepdims=True), sink)
        p = jnp.exp(s - m)
        denoms.append(jnp.sum(p, axis=-1, keepdims=True) + jnp.exp(sink - m))
        probs.append(p.astype(BF16))
    for j in range(kvs * grp):
        o = jnp.dot(probs[j], values[j // grp], preferred_element_type=F32) / denoms[j]
        o_ref[:, head_cols(j)] = o.astype(o_ref.dtype)


def _swa(proj, q_gain, k_gain, sinks, n_heads, n_kv, col0):
    t = proj.shape[0]
    grp = n_heads // n_kv
    kvs = min(SWA_KV_HEADS, n_kv)
    q_width, kv_width = kvs * grp * HEAD_DIM, kvs * HEAD_DIM
    assert col0 % q_width == 0 and (col0 + n_heads * HEAD_DIM) % kv_width == 0 and n_kv % kvs == 0
    qb0 = col0 // q_width
    kb0 = (col0 + n_heads * HEAD_DIM) // kv_width
    vb0 = kb0 + n_kv // kvs
    slopes = 2.0 ** (-8.0 * jnp.arange(1, n_heads + 1, dtype=F32) / n_heads)
    prev = lambda n: jnp.maximum(n - 1, 0)
    return pl.pallas_call(
        functools.partial(_swa_kernel, kvs=kvs, grp=grp),
        grid_spec=pltpu.PrefetchScalarGridSpec(
            num_scalar_prefetch=2,
            grid=(t // CHUNK, n_kv // kvs),
            in_specs=[pl.BlockSpec((CHUNK, q_width), lambda n, h, *_: (n, qb0 + h)),
                      pl.BlockSpec((CHUNK, kv_width), lambda n, h, *_: (n, kb0 + h)),
                      pl.BlockSpec((CHUNK, kv_width), lambda n, h, *_: (prev(n), kb0 + h)),
                      pl.BlockSpec((CHUNK, kv_width), lambda n, h, *_: (n, vb0 + h)),
                      pl.BlockSpec((CHUNK, kv_width), lambda n, h, *_: (prev(n), vb0 + h)),
                      pl.BlockSpec((1, HEAD_DIM), lambda n, h, *_: (0, 0)),
                      pl.BlockSpec((1, HEAD_DIM), lambda n, h, *_: (0, 0))],
            out_specs=pl.BlockSpec((CHUNK, q_width), lambda n, h, *_: (n, h)),
        ),
        out_shape=jax.ShapeDtypeStruct((t, n_heads * HEAD_DIM), BF16),
        compiler_params=_params("parallel", "arbitrary"),
        name="swa",
    )(slopes, sinks.astype(F32), proj, proj, proj, proj, proj,
      q_gain.reshape(1, HEAD_DIM).astype(F32), k_gain.reshape(1, HEAD_DIM).astype(F32))


def _conformer_kernel(a_ref, g_ref, ap_ref, gp_ref, w_ref, cb_ref, lg_ref, lb_ref, o_ref, hbuf, ybuf):
    i = pl.program_id(0)
    tc, width = a_ref.shape
    span = CONV_HALO + tc
    hp = ap_ref[...].astype(F32) * jax.nn.sigmoid(gp_ref[...].astype(F32))
    hbuf[0, 0:CONV_HALO, :] = jnp.where(i > 0, hp, 0.0)
    hbuf[0, CONV_HALO:span, :] = a_ref[...].astype(F32) * jax.nn.sigmoid(g_ref[...].astype(F32))
    hbuf[0, span:span + SUBLANES, :] = jnp.zeros((SUBLANES, width), F32)
    for r in range(1, SUBLANES):
        hbuf[r, 0:span, :] = hbuf[0, r:r + span, :]
    base = CONV_HALO - (CONV_WIDTH - 1)
    for c in range(width // HEAD_DIM):
        cols = slice(c * HEAD_DIM, (c + 1) * HEAD_DIM)
        for t0 in range(0, tc, CONV_ROWS):
            acc = jnp.zeros((CONV_ROWS, HEAD_DIM), F32)
            for r in range(SUBLANES):
                taps = [j for j in range(CONV_WIDTH) if (base + j) % SUBLANES == r]
                first = t0 + base + taps[0] - r
                rows = hbuf[r, first:t0 + base + taps[-1] - r + CONV_ROWS, cols]
                for j in taps:
                    off = t0 + base + j - r - first
                    acc = acc + w_ref[j:j + 1, cols] * rows[off:off + CONV_ROWS, :]
            ybuf[t0:t0 + CONV_ROWS, cols] = acc
    y = _layer_norm(ybuf[...] + cb_ref[...], lg_ref[...], lb_ref[...])
    o_ref[...] = (y * jax.nn.sigmoid(y)).astype(o_ref.dtype)


def _conformer(proj, conv_w, conv_b, ln_g, ln_b, tc=256):
    t = proj.shape[0]
    width = conv_w.shape[1]
    tc = min(tc, t)
    halo_blocks = tc // CONV_HALO
    prev = lambda i: jnp.maximum(i * halo_blocks - 1, 0)
    vec = lambda: pl.BlockSpec((1, width), lambda i: (0, 0))
    return pl.pallas_call(
        _conformer_kernel,
        grid=(t // tc,),
        in_specs=[pl.BlockSpec((tc, width), lambda i: (i, 0)),
                  pl.BlockSpec((tc, width), lambda i: (i, 1)),
                  pl.BlockSpec((CONV_HALO, width), lambda i: (prev(i), 0)),
                  pl.BlockSpec((CONV_HALO, width), lambda i: (prev(i), 1)),
                  pl.BlockSpec((CONV_WIDTH, width), lambda i: (0, 0)),
                  vec(), vec(), vec()],
        out_specs=pl.BlockSpec((tc, width), lambda i: (i, 0)),
        out_shape=jax.ShapeDtypeStruct((t, width), BF16),
        scratch_shapes=[pltpu.VMEM((SUBLANES, CONV_HALO + tc + SUBLANES, width), F32),
                        pltpu.VMEM((tc, width), F32)],
        compiler_params=_params("parallel"),
        name="conformer",
    )(proj, proj, proj, proj, conv_w.astype(F32), conv_b.reshape(1, width).astype(F32),
      ln_g.reshape(1, width).astype(F32), ln_b.reshape(1, width).astype(F32))


def _stickbreak_kernel(q_ref, k_ref, v_ref, tri_ref, o_ref, *, heads):
    i = pl.program_id(1)
    tq = CHUNK
    subs = q_ref.shape[0] // tq
    col = lax.broadcasted_iota(jnp.int32, (tq, SB_KEYS), 1)
    row = lax.broadcasted_iota(jnp.int32, (tq, 1), 0)
    head_cols = [slice(g * HEAD_DIM, (g + 1) * HEAD_DIM) for g in range(heads)]
    chains = [(s, g) for s in range(subs) for g in range(heads)]

    def walk_tile(walked, cs, accs):
        starts, valids = [], []
        for s in range(subs):
            block = i * subs + s
            upper = (block + 1) * tq - walked * SB_KEYS
            start = pl.multiple_of(jnp.maximum(upper - SB_KEYS, 0), tq)
            starts.append(start)
            valids.append(col < jnp.minimum(block * tq + row, upper) - start)
        zs = [lax.dot_general(q_ref[s * tq:(s + 1) * tq, head_cols[g]], k_ref[pl.ds(starts[s], SB_KEYS), head_cols[g]],
                              (((1,), (1,)), ((), ())), preferred_element_type=F32) * (HEAD_DIM ** -0.5)
              for s, g in chains]
        log_wins, splits, new_cs = [], [], []
        for n, (s, g) in enumerate(chains):
            z = zs[n]
            softplus = jnp.maximum(z, 0.0) + jnp.log(1.0 + jnp.exp(-jnp.abs(z)))
            log_fail = jnp.where(valids[s], -softplus, 0.0)
            hi = log_fail.astype(BF16)
            lo = (log_fail - hi.astype(F32)).astype(BF16)
            splits.append(jnp.concatenate([hi, lo], axis=1))
            log_wins.append(z - softplus)
            new_cs.append(cs[n] + jnp.sum(log_fail, axis=-1, keepdims=True))
        later = jnp.dot(jnp.concatenate(splits, axis=0), tri_ref[...], preferred_element_type=F32)
        new_accs = []
        for n, (s, g) in enumerate(chains):
            a = jnp.where(valids[s], jnp.exp(log_wins[n] + later[n * tq:(n + 1) * tq, :] + cs[n]), 0.0)
            new_accs.append(accs[n] + jnp.dot(a.astype(BF16), v_ref[pl.ds(starts[s], SB_KEYS), head_cols[g]],
                                              preferred_element_type=F32))
        return walked + 1, new_cs, new_accs

    def cond(state):
        walked, cs, _ = state
        alive = functools.reduce(jnp.maximum, [jnp.max(c) for c in cs])
        keys_left = (i + 1) * subs * tq - walked * SB_KEYS > 0
        return jnp.logical_and(keys_left, alive > SB_LOG_ZERO)

    def body(state):
        return walk_tile(*state)

    init = (0,
            [jnp.zeros((tq, 1), F32) for _ in chains],
            [jnp.zeros((tq, HEAD_DIM), F32) for _ in chains])
    _, _, accs = lax.while_loop(cond, body, init)
    for n, (s, g) in enumerate(chains):
        o_ref[s * tq:(s + 1) * tq, head_cols[g]] = accs[n].astype(o_ref.dtype)


def _stickbreak(proj, n_heads, col0):
    t = proj.shape[0]
    width = SB_HEADS * HEAD_DIM
    qb0 = col0 // width
    kb0 = qb0 + n_heads // SB_HEADS
    vb0 = kb0 + n_heads // SB_HEADS
    s_later = jnp.arange(2 * SB_KEYS)[:, None] % SB_KEYS
    tri = (s_later > jnp.arange(SB_KEYS)[None, :]).astype(BF16)
    resident = lambda c0: pl.BlockSpec((t, width), lambda h, i: (0, c0 + h), pipeline_mode=pl.Buffered(1))
    return pl.pallas_call(
        functools.partial(_stickbreak_kernel, heads=SB_HEADS),
        grid=(n_heads // SB_HEADS, t // SB_QUERIES),
        in_specs=[pl.BlockSpec((SB_QUERIES, width), lambda h, i: (i, qb0 + h)),
                  resident(kb0), resident(vb0),
                  pl.BlockSpec((2 * SB_KEYS, SB_KEYS), lambda h, i: (0, 0))],
        out_specs=pl.BlockSpec((SB_QUERIES, width), lambda h, i: (i, h)),
        out_shape=jax.ShapeDtypeStruct((t, n_heads * HEAD_DIM), BF16),
        compiler_params=_params("parallel", "arbitrary"),
        name="stickbreak",
    )(proj, proj, proj, tri)


def _router_kernel(x_ref, g_ref, w_ref, b_ref, hp_ref, route_ref, counts_ref, carry):
    @pl.when(pl.program_id(0) == 0)
    def _():
        carry[...] = jnp.zeros_like(carry)

    x = x_ref[...]
    tm, d = x.shape
    h = x * lax.rsqrt(jnp.mean(x * x, axis=-1, keepdims=True) + EPS) * g_ref[...]
    h_hi = h.astype(BF16)
    h_hif = h_hi.astype(F32)
    h_lo = (h - h_hif).astype(BF16)
    hi_both = jnp.dot(h_hi, w_ref[...], preferred_element_type=F32)
    logits = (hi_both[:, :ROUTE_LANES] + hi_both[:, ROUTE_LANES:]
              + jnp.dot(h_lo, w_ref[:, :ROUTE_LANES], preferred_element_type=F32)) + b_ref[...]
    lane = lax.broadcasted_iota(jnp.int32, logits.shape, 1)
    neg = -jnp.inf

    def first_max(vals):
        m = jnp.max(vals, axis=-1, keepdims=True)
        idx = jnp.min(jnp.where(vals == m, lane, ROUTE_LANES), axis=-1, keepdims=True)
        return m, idx

    gl = jnp.where(lane < N_GROUPS, logits, neg)
    gmax, g_sel = first_max(gl)
    g_weight = 1.0 / jnp.sum(jnp.exp(gl - gmax), axis=-1, keepdims=True)
    expert = lane - N_GROUPS
    in_group = (expert >= 0) & (expert < N_EXPERTS) & ((expert >> 3) == g_sel)
    el = jnp.where(in_group, logits, neg)
    m1, i1 = first_max(el)
    m2, i2 = first_max(jnp.where(lane == i1, neg, el))
    e21 = jnp.exp(m2 - m1)
    w1 = g_weight / (1.0 + e21)
    w2 = w1 * e21
    sel1 = lane == i1
    sel2 = lane == i2
    chosen = jnp.where(jnp.logical_or(sel1, sel2), 1.0, 0.0)
    row = lax.broadcasted_iota(jnp.int32, (tm, tm), 0)
    col = lax.broadcasted_iota(jnp.int32, (tm, tm), 1)
    earlier = jnp.where(col < row, 1.0, 0.0).astype(BF16)
    before = jnp.dot(earlier, chosen.astype(BF16), preferred_element_type=F32) + carry[...]
    rank1 = jnp.sum(jnp.where(sel1, before, 0.0), axis=-1, keepdims=True)
    rank2 = jnp.sum(jnp.where(sel2, before, 0.0), axis=-1, keepdims=True)
    carry[...] = carry[...] + jnp.sum(chosen, axis=0, keepdims=True)
    counts_ref[...] = carry[...]
    fields = ((i1 - N_GROUPS).astype(F32), (i2 - N_GROUPS).astype(F32), w1, w2, rank1, rank2)
    rec = jnp.zeros_like(logits)
    for k, val in enumerate(fields):
        rec = jnp.where(lane == k, val, rec)
    route_ref[...] = rec
    bits = lax.bitcast_convert_type(h_hif, jnp.uint32)
    hp_ref[...] = (bits[:, :d // 2] >> 16) | (bits[:, d // 2:] & jnp.uint32(0xFFFF0000))


def _router(x, norm_g, wr_g, br_g, wr_e, br_e, tm=512):
    t, d = x.shape
    pad = ROUTE_LANES - N_GROUPS - N_EXPERTS
    w = jnp.concatenate([wr_g.astype(F32), wr_e.astype(F32), jnp.zeros((d, pad), F32)], axis=1)
    b = jnp.concatenate([br_g.astype(F32), br_e.astype(F32), jnp.zeros((pad,), F32)]).reshape(1, ROUTE_LANES)
    w_hi = w.astype(BF16)
    w_lo = (w - w_hi.astype(F32)).astype(BF16)
    tm = min(tm, t)
    return pl.pallas_call(
        _router_kernel,
        grid=(t // tm,),
        in_specs=[pl.BlockSpec((tm, d), lambda i: (i, 0)),
                  pl.BlockSpec((1, d), lambda i: (0, 0)),
                  pl.BlockSpec((d, 2 * ROUTE_LANES), lambda i: (0, 0)),
                  pl.BlockSpec((1, ROUTE_LANES), lambda i: (0, 0))],
        out_specs=[pl.BlockSpec((tm, d // 2), lambda i: (i, 0)),
                   pl.BlockSpec((tm, ROUTE_LANES), lambda i: (i, 0)),
                   pl.BlockSpec((1, ROUTE_LANES), lambda i: (0, 0))],
        out_shape=[jax.ShapeDtypeStruct((t, d // 2), jnp.uint32),
                   jax.ShapeDtypeStruct((t, ROUTE_LANES), F32),
                   jax.ShapeDtypeStruct((1, ROUTE_LANES), F32)],
        scratch_shapes=[pltpu.VMEM((1, ROUTE_LANES), F32)],
        compiler_params=_params("arbitrary"),
        name="router",
    )(x, norm_g.reshape(1, d), jnp.concatenate([w_hi, w_lo], axis=1), b)


def _lookup(table, idx):
    hit = idx[..., None] == jnp.arange(table.shape[0], dtype=jnp.int32)
    return jnp.sum(jnp.where(hit, table, 0), axis=-1)


def _dispatch_tables(route, counts, t):
    counts = counts[0, N_GROUPS:N_GROUPS + N_EXPERTS].astype(jnp.int32)
    padded = (counts + EXPERT_BLOCK - 1) // EXPERT_BLOCK * EXPERT_BLOCK
    padded_end = jnp.cumsum(padded)
    padded_start = padded_end - padded
    expert = route[:, 0:2].astype(jnp.int32)
    rank = route[:, 4:6].astype(jnp.int32)
    assign_slot = (_lookup(padded_start, expert) + rank).reshape(-1)

    n_slots = (-(-2 * t // EXPERT_BLOCK) + N_EXPERTS) * EXPERT_BLOCK
    n_items = N_EXPERTS + n_slots // MOE_MAX_ROWS
    chunks = (padded + MOE_MAX_ROWS - 1) // MOE_MAX_ROWS
    chunk_end = jnp.cumsum(chunks)
    n_used = chunk_end[-1]
    ids = jnp.arange(n_items, dtype=jnp.int32)
    item_e = jnp.minimum(jnp.sum(chunk_end[None, :] <= ids[:, None], axis=1), N_EXPERTS - 1).astype(jnp.int32)
    j = ids - _lookup(chunk_end - chunks, item_e)
    item_start = _lookup(padded_start, item_e) + j * MOE_MAX_ROWS
    item_rows = jnp.clip(_lookup(padded, item_e) - j * MOE_MAX_ROWS, 0, MOE_MAX_ROWS)
    used = ids < n_used
    last_e = jnp.sum(jnp.where(ids == n_used - 1, item_e, 0))
    item_e = jnp.where(used, item_e, last_e).astype(jnp.int32)
    item_start = jnp.where(used, item_start, 0).astype(jnp.int32)
    item_rows = jnp.where(used, item_rows, 0).astype(jnp.int32)
    total = padded_end[-1].reshape(1).astype(jnp.int32)
    return (assign_slot.astype(jnp.int32), padded_start.astype(jnp.int32), padded.astype(jnp.int32), total,
            item_e, item_start, item_rows, n_slots, n_items)


def _dispatch_kernel(as_ref, ps_ref, pd_ref, tot_ref, hp_ref, xs_hbm, zbuf, sem, zsem):
    i = pl.program_id(0)
    tm = hp_ref.shape[0]
    n_blocks = xs_hbm.shape[0] // EXPERT_BLOCK

    def zero_block(first_row):
        return pltpu.make_async_copy(zbuf, xs_hbm.at[pl.ds(pl.multiple_of(first_row, EXPERT_BLOCK), EXPERT_BLOCK)],
                                     zsem)

    def zero_fill(act):
        def per_expert(e, carry):
            @pl.when(pd_ref[e] > 0)
            def _():
                act(zero_block(ps_ref[e] + pd_ref[e] - EXPERT_BLOCK))
            return carry
        lax.fori_loop(0, N_EXPERTS, per_expert, 0)

        def per_tail_block(b, carry):
            act(zero_block(b * EXPERT_BLOCK))
            return carry
        lax.fori_loop(tot_ref[0] // EXPERT_BLOCK, n_blocks, per_tail_block, 0)

    @pl.when(i == 0)
    def _():
        zbuf[...] = jnp.zeros_like(zbuf)
        zero_fill(lambda cp: cp.start())
        zero_fill(lambda cp: cp.wait())

    def body(r, carry):
        a = 2 * (i * tm + r)
        for k in range(2):
            pltpu.make_async_copy(hp_ref.at[pl.ds(r, 1)], xs_hbm.at[pl.ds(as_ref[a + k], 1)], sem).start()
        return carry
    lax.fori_loop(0, tm, body, 0, unroll=8)
    for k in range(2):
        pltpu.make_async_copy(hp_ref, xs_hbm.at[pl.ds(0, tm)], sem).wait()


def _dispatch(hp, assign_slot, padded_start, padded, total, n_slots, tm=512):
    t, words = hp.shape
    tm = min(tm, t)
    return pl.pallas_call(
        _dispatch_kernel,
        grid_spec=pltpu.PrefetchScalarGridSpec(
            num_scalar_prefetch=4,
            grid=(t // tm,),
            in_specs=[pl.BlockSpec((tm, words), lambda i, *_: (i, 0))],
            out_specs=pl.BlockSpec(memory_space=pl.ANY),
            scratch_shapes=[pltpu.VMEM((EXPERT_BLOCK, words), hp.dtype),
                            pltpu.SemaphoreType.DMA(()),
                            pltpu.SemaphoreType.DMA(())],
        ),
        out_shape=jax.ShapeDtypeStruct((n_slots, words), hp.dtype),
        compiler_params=_params("arbitrary"),
        name="moe_dispatch",
    )(assign_slot, padded_start, padded, total, hp)


def _unpack_pairs(words):
    lo = lax.bitcast_convert_type(words << 16, F32)
    hi = lax.bitcast_convert_type(words & jnp.uint32(0xFFFF0000), F32)
    return lo, hi


def _pack_pairs(lo, hi):
    lo_bits = lax.bitcast_convert_type(lo.astype(BF16).astype(F32), jnp.uint32)
    hi_bits = lax.bitcast_convert_type(hi.astype(BF16).astype(F32), jnp.uint32)
    return (lo_bits >> 16) | (hi_bits & jnp.uint32(0xFFFF0000))


def _expert_items_kernel(ie_ref, is_ref, ir_ref, tot_ref, in_hbm, *refs, n_weights, cast_weights, compute):
    w_refs = refs[:n_weights]
    out_hbm = refs[n_weights]
    inbuf, outbuf, zbuf, w_bf, sem_in, sem_out, sem_z = refs[n_weights + 1:]
    i = pl.program_id(0)
    n_items = pl.num_programs(0)
    max_blocks = inbuf.shape[1] // EXPERT_BLOCK
    n_out_blocks = out_hbm.shape[0] // EXPERT_BLOCK

    def rows_of(hbm, first_row):
        return hbm.at[pl.ds(pl.multiple_of(first_row, EXPERT_BLOCK), EXPERT_BLOCK)]

    def for_blocks(item, act):
        for c in range(max_blocks):
            @pl.when(c * EXPERT_BLOCK < ir_ref[item])
            def _():
                act(c, is_ref[item] + c * EXPERT_BLOCK)

    def in_copy(buf, act):
        return lambda c, row: act(pltpu.make_async_copy(
            rows_of(in_hbm, row), inbuf.at[buf, pl.ds(c * EXPERT_BLOCK, EXPERT_BLOCK)], sem_in.at[buf]))

    def out_copy(buf, act):
        return lambda c, row: act(pltpu.make_async_copy(
            outbuf.at[buf, pl.ds(c * EXPERT_BLOCK, EXPERT_BLOCK)], rows_of(out_hbm, row), sem_out.at[buf]))

    start = lambda cp: cp.start()
    wait = lambda cp: cp.wait()
    buf = i % 2

    @pl.when(i == 0)
    def _():
        for_blocks(0, in_copy(0, start))
        zbuf[...] = jnp.zeros_like(zbuf)

        def tail(act):
            def body(b, carry):
                act(pltpu.make_async_copy(zbuf, rows_of(out_hbm, b * EXPERT_BLOCK), sem_z))
                return carry
            lax.fori_loop(tot_ref[0] // EXPERT_BLOCK, n_out_blocks, body, 0)
        tail(start)
        tail(wait)

    @pl.when(i + 1 < n_items)
    def _():
        for_blocks(i + 1, in_copy(1 - buf, start))

    for_blocks(i, in_copy(buf, wait))

    @pl.when(jnp.logical_or(i == 0, ie_ref[i] != ie_ref[jnp.maximum(i - 1, 0)]))
    def _():
        cast_weights(w_refs, w_bf)

    @pl.when(i >= 2)
    def _():
        for_blocks(i - 2, out_copy(buf, wait))

    n_blk = ir_ref[i] // EXPERT_BLOCK
    for c in range(1, max_blocks + 1):
        @pl.when(n_blk == c)
        def _():
            rows = c * EXPERT_BLOCK
            outbuf[buf, 0:rows, :] = compute(inbuf[buf, 0:rows, :], w_bf)

    for_blocks(i, out_copy(buf, start))

    @pl.when(i == n_items - 1)
    def _():
        @pl.when(i >= 1)
        def _():
            for_blocks(i - 1, out_copy(1 - buf, wait))
        for_blocks(i, out_copy(buf, wait))


def _expert_items_call(name, items, in_rows, weights, out_width, out_dtype, n_w_cols, cast_weights, compute):
    item_e, item_start, item_rows, total = items
    n_slots, in_width = in_rows.shape
    k_dim = weights[0].shape[1]
    w_spec = lambda w: pl.BlockSpec((None,) + w.shape[1:], lambda i, ie, *_: (ie[i], 0, 0))
    return pl.pallas_call(
        functools.partial(_expert_items_kernel, n_weights=len(weights), cast_weights=cast_weights, compute=compute),
        grid_spec=pltpu.PrefetchScalarGridSpec(
            num_scalar_prefetch=4,
            grid=(item_e.shape[0],),
            in_specs=[pl.BlockSpec(memory_space=pl.ANY)] + [w_spec(w) for w in weights],
            out_specs=pl.BlockSpec(memory_space=pl.ANY),
            scratch_shapes=[pltpu.VMEM((2, MOE_MAX_ROWS, in_width), in_rows.dtype),
                            pltpu.VMEM((2, MOE_MAX_ROWS, out_width), out_dtype),
                            pltpu.VMEM((EXPERT_BLOCK, out_width), out_dtype),
                            pltpu.VMEM((k_dim, n_w_cols), BF16),
                            pltpu.SemaphoreType.DMA((2,)),
                            pltpu.SemaphoreType.DMA((2,)),
                            pltpu.SemaphoreType.DMA(())],
        ),
        out_shape=jax.ShapeDtypeStruct((n_slots, out_width), out_dtype),
        compiler_params=pltpu.CompilerParams(dimension_semantics=("arbitrary",),
                                             vmem_limit_bytes=V7X_VMEM_LIMIT_MOE),
        name=name,
    )(item_e, item_start, item_rows, total, in_rows, *weights)


def _cast_gate_up(w_refs, w_bf):
    d_exp = w_refs[0].shape[1]
    w_bf[:, 0:d_exp] = w_refs[0][...].astype(BF16)
    w_bf[:, d_exp:2 * d_exp] = w_refs[1][...].astype(BF16)


def _gate_up(words, w_bf):
    half = words.shape[1]
    d_exp = w_bf.shape[1] // 2
    lo, hi = _unpack_pairs(words)
    gu = (jnp.dot(lo.astype(BF16), w_bf[0:half, :], preferred_element_type=F32)
          + jnp.dot(hi.astype(BF16), w_bf[half:2 * half, :], preferred_element_type=F32))
    gate, up = gu[:, 0:d_exp], gu[:, d_exp:2 * d_exp]
    return (gate * jax.nn.sigmoid(gate) * up).astype(BF16)


def _cast_down(w_refs, w_bf):
    w_bf[...] = w_refs[0][...].astype(BF16)


def _down(hid, w_bf):
    half = w_bf.shape[1] // 2
    y = jnp.dot(hid, w_bf[...], preferred_element_type=F32)
    return _pack_pairs(y[:, 0:half], y[:, half:2 * half])


def _combine_kernel(as_ref, x_ref, route_ref, g_ref, y_hbm, o_ref, *rest, emit_norm):
    if emit_norm:
        h_ref, ybuf, sem = rest
    else:
        ybuf, sem = rest
    i = pl.program_id(0)
    tm, d = x_ref.shape
    half = d // 2

    def gather_row(tile, r, buf):
        a = 2 * (tile * tm + r)
        for k in range(2):
            pltpu.make_async_copy(y_hbm.at[pl.ds(as_ref[a + k], 1)], ybuf.at[buf, k, pl.ds(r, 1)],
                                  sem.at[buf]).start()

    def wait_gather(buf):
        for k in range(2):
            pltpu.make_async_copy(y_hbm.at[pl.ds(0, tm)], ybuf.at[buf, k], sem.at[buf]).wait()

    @pl.when(i == 0)
    def _():
        def body(r, carry):
            gather_row(0, r, 0)
            return carry
        lax.fori_loop(0, tm, body, 0, unroll=8)

    buf = i % 2
    last = pl.num_programs(0) - 1
    next_tile = jnp.minimum(i + 1, last)
    wait_gather(buf)

    def row_chunk(c, carry):
        for rr in range(COMBINE_ROWS):
            gather_row(next_tile, c * COMBINE_ROWS + rr, 1 - buf)
        rows = pl.ds(pl.multiple_of(c * COMBINE_ROWS, COMBINE_ROWS), COMBINE_ROWS)
        w1 = route_ref[rows, 2:3]
        w2 = route_ref[rows, 3:4]
        sq = jnp.zeros((COMBINE_ROWS, 1), F32)
        for c0 in range(0, half, COMBINE_COLS):
            cols = slice(c0, c0 + COMBINE_COLS)
            up_cols = slice(half + c0, half + c0 + COMBINE_COLS)
            lo1, hi1 = _unpack_pairs(ybuf[buf, 0, rows, cols])
            lo2, hi2 = _unpack_pairs(ybuf[buf, 1, rows, cols])
            out_lo = x_ref[rows, cols] + w1 * lo1 + w2 * lo2
            out_hi = x_ref[rows, up_cols] + w1 * hi1 + w2 * hi2
            o_ref[rows, cols] = out_lo
            o_ref[rows, up_cols] = out_hi
            if emit_norm:
                sq = sq + jnp.sum(out_lo * out_lo + out_hi * out_hi, axis=-1, keepdims=True)
        if emit_norm:
            inv = lax.rsqrt(sq * (1.0 / d) + EPS)
            for c0 in range(0, d, COMBINE_COLS):
                cols = slice(c0, c0 + COMBINE_COLS)
                h_ref[rows, cols] = (o_ref[rows, cols] * inv * g_ref[:, cols]).astype(h_ref.dtype)
        return carry
    lax.fori_loop(0, tm // COMBINE_ROWS, row_chunk, 0, unroll=4)

    @pl.when(i == last)
    def _():
        wait_gather(1 - buf)


def _moe(x, norm_g, wr_g, br_g, wr_e, br_e, w_gate, w_up, w_down, next_norm_g=None, tm_combine=256):
    t, d = x.shape
    d_exp = w_gate.shape[2]
    hp, route, counts = _router(x, norm_g, wr_g, br_g, wr_e, br_e)
    (assign_slot, padded_start, padded, total, item_e, item_start, item_rows,
     n_slots, _) = _dispatch_tables(route, counts, t)
    items = (item_e, item_start, item_rows, total)
    xs = _dispatch(hp, assign_slot, padded_start, padded, total, n_slots)
    hid = _expert_items_call("moe_up", items, xs, (w_gate, w_up), d_exp, BF16, 2 * d_exp,
                             _cast_gate_up, _gate_up)
    y = _expert_items_call("moe_down", items, hid, (w_down,), d // 2, jnp.uint32, d, _cast_down, _down)

    tm = min(tm_combine, t)
    emit_norm = next_norm_g is not None
    row_tile = pl.BlockSpec((tm, d), lambda i, a: (i, 0))
    gain = next_norm_g if emit_norm else jnp.ones((d,), F32)
    out = pl.pallas_call(
        functools.partial(_combine_kernel, emit_norm=emit_norm),
        grid_spec=pltpu.PrefetchScalarGridSpec(
            num_scalar_prefetch=1,
            grid=(t // tm,),
            in_specs=[row_tile,
                      pl.BlockSpec((tm, ROUTE_LANES), lambda i, a: (i, 0)),
                      pl.BlockSpec((1, d), lambda i, a: (0, 0)),
                      pl.BlockSpec(memory_space=pl.ANY)],
            out_specs=[row_tile, row_tile] if emit_norm else row_tile,
            scratch_shapes=[pltpu.VMEM((2, 2, tm, d // 2), jnp.uint32),
                            pltpu.SemaphoreType.DMA((2,))],
        ),
        out_shape=([jax.ShapeDtypeStruct((t, d), F32), jax.ShapeDtypeStruct((t, d), BF16)] if emit_norm
                   else jax.ShapeDtypeStruct((t, d), F32)),
        compiler_params=_params("arbitrary"),
        name="moe_combine",
    )(assign_slot, x, route, gain.reshape(1, d).astype(F32), y)
    return tuple(out) if emit_norm else (out, None)


def _even_layer_mixer(x, mix_norm, w_in, a_ln_g, a_ln_b, a_ws, a_bs, b_q_norm, b_k_norm, b_sinks, w_out):
    n_groups = a_ws.shape[0]
    a_width = n_groups * HEAD_DIM
    n_heads = b_sinks.shape[0]
    n_kv = (w_in.shape[1] - 2 * a_width - n_heads * HEAD_DIM) // (2 * HEAD_DIM)
    h = _rmsnorm(x, mix_norm)
    proj = _matmul(h, w_in.astype(BF16), tm=1024, tn=1024)
    y_a = _gmlp(proj, a_ln_g, a_ln_b, a_ws, a_bs)
    y_b = _swa(proj, b_q_norm, b_k_norm, b_sinks, n_heads, n_kv, col0=2 * a_width)
    return _out_proj(x, y_a, y_b, w_out.astype(BF16))


def _odd_layer_mixer(x, h, w_in, c_conv_w, c_conv_b, c_ln_g, c_ln_b, w_out):
    c_width = c_conv_w.shape[1]
    n_heads = (w_in.shape[1] - 2 * c_width) // (3 * HEAD_DIM)
    proj = _matmul(h, w_in.astype(BF16), tm=1024, tn=1024)
    y_c = _conformer(proj, c_conv_w, c_conv_b, c_ln_g, c_ln_b)
    y_d = _stickbreak(proj, n_heads, col0=2 * c_width)
    return _out_proj(x, y_c, y_d, w_out.astype(BF16))


def kernel(x, l0_mix_norm, l0_w_in, l0_a_ln_g, l0_a_ln_b, l0_a_ws, l0_a_bs, l0_b_q_norm, l0_b_k_norm, l0_b_sinks, l0_w_out, l0_ffn_norm, l0_router_group_w, l0_router_group_b, l0_router_expert_w, l0_router_expert_b, l0_w_gate, l0_w_up, l0_w_down, l1_mix_norm, l1_w_in, l1_c_conv_w, l1_c_conv_b, l1_c_ln_g, l1_c_ln_b, l1_w_out, l1_ffn_norm, l1_router_group_w, l1_router_group_b, l1_router_expert_w, l1_router_expert_b, l1_w_gate, l1_w_up, l1_w_down):
    bsz, seq, d = x.shape
    xt = x.reshape(bsz * seq, d)
    assert bsz == 1, "the mixers index positions by row; one sequence per call"
    xt = _even_layer_mixer(xt, l0_mix_norm, l0_w_in, l0_a_ln_g, l0_a_ln_b, l0_a_ws, l0_a_bs,
                           l0_b_q_norm, l0_b_k_norm, l0_b_sinks, l0_w_out)
    xt, h1 = _moe(xt, l0_ffn_norm, l0_router_group_w, l0_router_group_b, l0_router_expert_w,
                  l0_router_expert_b, l0_w_gate, l0_w_up, l0_w_down, next_norm_g=l1_mix_norm)
    xt = _odd_layer_mixer(xt, h1, l1_w_in, l1_c_conv_w, l1_c_conv_b, l1_c_ln_g, l1_c_ln_b, l1_w_out)
    xt, _ = _moe(xt, l1_ffn_norm, l1_router_group_w, l1_router_group_b, l1_router_expert_w,
                 l1_router_expert_b, l1_w_gate, l1_w_up, l1_w_down)
    return xt.reshape(bsz, seq, d)
```

```python
import functools

import jax
import jax.numpy as jnp
from jax import lax
from jax.experimental import pallas as pl
from jax.experimental.pallas import tpu as pltpu

F32 = jnp.float32
BF16 = jnp.bfloat16

HEAD_DIM = 128
CHUNK = 128
EPS = 1e-6
N_GROUPS = 8
EXPERTS_PER_GROUP = 8
N_EXPERTS = N_GROUPS * EXPERTS_PER_GROUP
EXPERT_BLOCK = 128
CONV_WIDTH = 31
SUBLANES = 8
CONV_ROWS = 128
CONV_HALO = 32
ROUTE_LANES = 128
GMLP_ROWS = 512
SWA_KV_HEADS = 4
SB_LOG_ZERO = -104.0
SB_HEADS = 4
SB_QUERIES = 256
SB_KEYS = 384
COMBINE_ROWS = 16
COMBINE_COLS = 512
MOE_MAX_ROWS = 768
V7X_VMEM_LIMIT = 56 * 1024 * 1024
V7X_VMEM_LIMIT_MOE = 60 * 1024 * 1024


def _params(*sem):
    return pltpu.CompilerParams(dimension_semantics=sem, vmem_limit_bytes=V7X_VMEM_LIMIT)


def _rmsnorm_kernel(x_ref, g_ref, o_ref):
    x = x_ref[...]
    ms = jnp.mean(x * x, axis=-1, keepdims=True)
    o_ref[...] = (x * lax.rsqrt(ms + EPS) * g_ref[...]).astype(o_ref.dtype)


def _rmsnorm(x, g, tm=512):
    t, d = x.shape
    return pl.pallas_call(
        _rmsnorm_kernel,
        grid=(t // tm,),
        in_specs=[pl.BlockSpec((tm, d), lambda i: (i, 0)),
                  pl.BlockSpec((1, d), lambda i: (0, 0))],
        out_specs=pl.BlockSpec((tm, d), lambda i: (i, 0)),
        out_shape=jax.ShapeDtypeStruct((t, d), BF16),
        compiler_params=_params("parallel"),
        name="rmsnorm",
    )(x, g.reshape(1, d))


def _matmul_kernel(a_ref, w_ref, o_ref):
    o_ref[...] = jnp.dot(a_ref[...], w_ref[...], preferred_element_type=F32).astype(o_ref.dtype)


def _matmul(a, w, tm, tn):
    m, k = a.shape
    n = w.shape[1]
    tm = min(tm, m)
    return pl.pallas_call(
        _matmul_kernel,
        grid=(m // tm, n // tn),
        in_specs=[pl.BlockSpec((tm, k), lambda i, j: (i, 0)),
                  pl.BlockSpec((k, tn), lambda i, j: (0, j))],
        out_specs=pl.BlockSpec((tm, tn), lambda i, j: (i, j)),
        out_shape=jax.ShapeDtypeStruct((m, n), BF16),
        compiler_params=_params("parallel", "arbitrary"),
        name="in_proj",
    )(a, w)


def _out_proj_kernel(x_ref, a_ref, b_ref, wa_ref, wb_ref, o_ref):
    acc = jnp.dot(a_ref[...], wa_ref[...], preferred_element_type=F32)
    acc = acc + jnp.dot(b_ref[...], wb_ref[...], preferred_element_type=F32)
    o_ref[...] = x_ref[...] + acc


def _out_proj(x, ya, yb, w, tm=1024, tn=1024):
    t, d = x.shape
    ka, kb = ya.shape[1], yb.shape[1]
    assert ka == kb and w.shape == (ka + kb, d)
    tm = min(tm, t)
    return pl.pallas_call(
        _out_proj_kernel,
        grid=(t // tm, d // tn),
        in_specs=[pl.BlockSpec((tm, tn), lambda i, j: (i, j)),
                  pl.BlockSpec((tm, ka), lambda i, j: (i, 0)),
                  pl.BlockSpec((tm, kb), lambda i, j: (i, 0)),
                  pl.BlockSpec((ka, tn), lambda i, j: (0, j)),
                  pl.BlockSpec((kb, tn), lambda i, j: (1, j))],
        out_specs=pl.BlockSpec((tm, tn), lambda i, j: (i, j)),
        out_shape=jax.ShapeDtypeStruct((t, d), F32),
        compiler_params=_params("parallel", "arbitrary"),
        name="out_proj",
    )(x, ya, yb, w, w)


def _layer_norm(x, g, b):
    mu = jnp.mean(x, axis=-1, keepdims=True)
    xc = x - mu
    var = jnp.mean(xc * xc, axis=-1, keepdims=True)
    return xc * lax.rsqrt(var + EPS) * g + b


def _gelu(x):
    return 0.5 * x * (1.0 + lax.erf(x * (2.0 ** -0.5)))


def _gmlp_kernel(u_ref, v_ref, lng_ref, lnb_ref, w_ref, bs_ref, o_ref):
    n_groups = w_ref.shape[0]
    group_cols = [slice(g * HEAD_DIM, (g + 1) * HEAD_DIM) for g in range(n_groups)]
    for c in range(u_ref.shape[0] // CHUNK):
        rows = slice(c * CHUNK, (c + 1) * CHUNK)
        v = _gelu(v_ref[rows, :].astype(F32))
        vn = _layer_norm(v, lng_ref[...], lnb_ref[...]).astype(BF16)
        mixed = [jnp.dot(w_ref[g], vn[:, cols], preferred_element_type=F32) for g, cols in enumerate(group_cols)]
        for g, cols in enumerate(group_cols):
            u = _gelu(u_ref[rows, cols].astype(F32))
            o_ref[rows, cols] = (u * (mixed[g] + bs_ref[:, cols])).astype(o_ref.dtype)


def _gmlp(proj, ln_g, ln_b, w_s, b_s):
    t = proj.shape[0]
    n_groups = w_s.shape[0]
    width = n_groups * HEAD_DIM
    causal = jnp.tril(jnp.ones((CHUNK, CHUNK), dtype=bool))
    w = jnp.where(causal[None], w_s, 0.0).astype(BF16)
    bs = jnp.repeat(b_s.T.astype(F32), HEAD_DIM, axis=1)
    tm = min(GMLP_ROWS, t)
    return pl.pallas_call(
        _gmlp_kernel,
        grid=(t // tm,),
        in_specs=[pl.BlockSpec((tm, width), lambda i: (i, 0)),
                  pl.BlockSpec((tm, width), lambda i: (i, 1)),
                  pl.BlockSpec((1, width), lambda i: (0, 0)),
                  pl.BlockSpec((1, width), lambda i: (0, 0)),
                  pl.BlockSpec((n_groups, CHUNK, CHUNK), lambda i: (0, 0, 0)),
                  pl.BlockSpec((CHUNK, width), lambda i: (0, 0))],
        out_specs=pl.BlockSpec((tm, width), lambda i: (i, 0)),
        out_shape=jax.ShapeDtypeStruct((t, width), BF16),
        compiler_params=_params("parallel"),
        name="gmlp",
    )(proj, proj, ln_g.reshape(1, width), ln_b.reshape(1, width), w, bs)


def _swa_kernel(slopes_ref, sinks_ref, q_ref, kc_ref, kp_ref, vc_ref, vp_ref, qg_ref, kg_ref, o_ref, *, kvs, grp):
    n = pl.program_id(0)
    first_head = pl.program_id(1) * kvs * grp
    qi = lax.broadcasted_iota(jnp.int32, (CHUNK, 2 * CHUNK), 0)
    kj = lax.broadcasted_iota(jnp.int32, (CHUNK, 2 * CHUNK), 1)
    dist = qi + CHUNK - kj
    valid = (dist >= 0) & (dist < CHUNK) & (n * CHUNK - CHUNK + kj >= 0)
    distf = dist.astype(F32)
    head_cols = lambda j: slice(j * HEAD_DIM, (j + 1) * HEAD_DIM)
    scores, values = [], []
    for kv in range(kvs):
        k = jnp.concatenate([kp_ref[:, head_cols(kv)], kc_ref[:, head_cols(kv)]], axis=0).astype(F32)
        k = k * lax.rsqrt(jnp.mean(k * k, axis=-1, keepdims=True) + EPS) * kg_ref[...]
        kb = k.astype(BF16)
        values.append(jnp.concatenate([vp_ref[:, head_cols(kv)], vc_ref[:, head_cols(kv)]], axis=0))
        for g in range(grp):
            q = q_ref[:, head_cols(kv * grp + g)].astype(F32)
            q = q * lax.rsqrt(jnp.mean(q * q, axis=-1, keepdims=True) + EPS) * qg_ref[...]
            scores.append(lax.dot_general(q.astype(BF16), kb, (((1,), (1,)), ((), ())),
                                          preferred_element_type=F32) * (HEAD_DIM ** -0.5))
    probs, denoms = [], []
    for j in range(kvs * grp):
        slope = slopes_ref[first_head + j]
        sink = sinks_ref[first_head + j]
        s = jnp.where(valid, scores[j] - slope * distf, -jnp.inf)
        m = jnp.maximum(jnp.max(s, axis=-1, keepdims=True), sink)
        p = jnp.exp(s - m)
        denoms.append(jnp.sum(p, axis=-1, keepdims=True) + jnp.exp(sink - m))
        probs.append(p.astype(BF16))
    for j in range(kvs * grp):
        o = jnp.dot(probs[j], values[j // grp], preferred_element_type=F32) / denoms[j]
        o_ref[:, head_cols(j)] = o.astype(o_ref.dtype)


def _swa(proj, q_gain, k_gain, sinks, n_heads, n_kv, col0):
    t = proj.shape[0]
    grp = n_heads // n_kv
    kvs = min(SWA_KV_HEADS, n_kv)
    q_width, kv_width = kvs * grp * HEAD_DIM, kvs * HEAD_DIM
    assert col0 % q_width == 0 and (col0 + n_heads * HEAD_DIM) % kv_width == 0 and n_kv % kvs == 0
    qb0 = col0 // q_width
    kb0 = (col0 + n_heads * HEAD_DIM) // kv_width
    vb0 = kb0 + n_kv // kvs
    slopes = 2.0 ** (-8.0 * jnp.arange(1, n_heads + 1, dtype=F32) / n_heads)
    prev = lambda n: jnp.maximum(n - 1, 0)
    return pl.pallas_call(
        functools.partial(_swa_kernel, kvs=kvs, grp=grp),
        grid_spec=pltpu.PrefetchScalarGridSpec(
            num_scalar_prefetch=2,
            grid=(t // CHUNK, n_kv // kvs),
            in_specs=[pl.BlockSpec((CHUNK, q_width), lambda n, h, *_: (n, qb0 + h)),
                      pl.BlockSpec((CHUNK, kv_width), lambda n, h, *_: (n, kb0 + h)),
                      pl.BlockSpec((CHUNK, kv_width), lambda n, h, *_: (prev(n), kb0 + h)),
                      pl.BlockSpec((CHUNK, kv_width), lambda n, h, *_: (n, vb0 + h)),
                      pl.BlockSpec((CHUNK, kv_width), lambda n, h, *_: (prev(n), vb0 + h)),
                      pl.BlockSpec((1, HEAD_DIM), lambda n, h, *_: (0, 0)),
                      pl.BlockSpec((1, HEAD_DIM), lambda n, h, *_: (0, 0))],
            out_specs=pl.BlockSpec((CHUNK, q_width), lambda n, h, *_: (n, h)),
        ),
        out_shape=jax.ShapeDtypeStruct((t, n_heads * HEAD_DIM), BF16),
        compiler_params=_params("parallel", "arbitrary"),
        name="swa",
    )(slopes, sinks.astype(F32), proj, proj, proj, proj, proj,
      q_gain.reshape(1, HEAD_DIM).astype(F32), k_gain.reshape(1, HEAD_DIM).astype(F32))


def _conformer_kernel(a_ref, g_ref, ap_ref, gp_ref, w_ref, cb_ref, lg_ref, lb_ref, o_ref, hbuf, ybuf):
    i = pl.program_id(0)
    tc, width = a_ref.shape
    span = CONV_HALO + tc
    hp = ap_ref[...].astype(F32) * jax.nn.sigmoid(gp_ref[...].astype(F32))
    hbuf[0, 0:CONV_HALO, :] = jnp.where(i > 0, hp, 0.0)
    hbuf[0, CONV_HALO:span, :] = a_ref[...].astype(F32) * jax.nn.sigmoid(g_ref[...].astype(F32))
    hbuf[0, span:span + SUBLANES, :] = jnp.zeros((SUBLANES, width), F32)
    for r in range(1, SUBLANES):
        hbuf[r, 0:span, :] = hbuf[0, r:r + span, :]
    base = CONV_HALO - (CONV_WIDTH - 1)
    for c in range(width // HEAD_DIM):
        cols = slice(c * HEAD_DIM, (c + 1) * HEAD_DIM)
        for t0 in range(0, tc, CONV_ROWS):
            acc = jnp.zeros((CONV_ROWS, HEAD_DIM), F32)
            for r in range(SUBLANES):
                taps = [j for j in range(CONV_WIDTH) if (base + j) % SUBLANES == r]
                first = t0 + base + taps[0] - r
                rows = hbuf[r, first:t0 + base + taps[-1] - r + CONV_ROWS, cols]
                for j in taps:
                    off = t0 + base + j - r - first
                    acc = acc + w_ref[j:j + 1, cols] * rows[off:off + CONV_ROWS, :]
            ybuf[t0:t0 + CONV_ROWS, cols] = acc
    y = _layer_norm(ybuf[...] + cb_ref[...], lg_ref[...], lb_ref[...])
    o_ref[...] = (y * jax.nn.sigmoid(y)).astype(o_ref.dtype)


def _conformer(proj, conv_w, conv_b, ln_g, ln_b, tc=256):
    t = proj.shape[0]
    width = conv_w.shape[1]
    tc = min(tc, t)
    halo_blocks = tc // CONV_HALO
    prev = lambda i: jnp.maximum(i * halo_blocks - 1, 0)
    vec = lambda: pl.BlockSpec((1, width), lambda i: (0, 0))
    return pl.pallas_call(
        _conformer_kernel,
        grid=(t // tc,),
        in_specs=[pl.BlockSpec((tc, width), lambda i: (i, 0)),
                  pl.BlockSpec((tc, width), lambda i: (i, 1)),
                  pl.BlockSpec((CONV_HALO, width), lambda i: (prev(i), 0)),
                  pl.BlockSpec((CONV_HALO, width), lambda i: (prev(i), 1)),
                  pl.BlockSpec((CONV_WIDTH, width), lambda i: (0, 0)),
                  vec(), vec(), vec()],
        out_specs=pl.BlockSpec((tc, width), lambda i: (i, 0)),
        out_shape=jax.ShapeDtypeStruct((t, width), BF16),
        scratch_shapes=[pltpu.VMEM((SUBLANES, CONV_HALO + tc + SUBLANES, width), F32),
                        pltpu.VMEM((tc, width), F32)],
        compiler_params=_params("parallel"),
        name="conformer",
    )(proj, proj, proj, proj, conv_w.astype(F32), conv_b.reshape(1, width).astype(F32),
      ln_g.reshape(1, width).astype(F32), ln_b.reshape(1, width).astype(F32))


def _stickbreak_kernel(q_ref, k_ref, v_ref, tri_ref, o_ref, *, heads):
    i = pl.program_id(1)
    tq = CHUNK
    subs = q_ref.shape[0] // tq
    col = lax.broadcasted_iota(jnp.int32, (tq, SB_KEYS), 1)
    row = lax.broadcasted_iota(jnp.int32, (tq, 1), 0)
    head_cols = [slice(g * HEAD_DIM, (g + 1) * HEAD_DIM) for g in range(heads)]
    chains = [(s, g) for s in range(subs) for g in range(heads)]

    def walk_tile(walked, cs, accs):
        starts, valids = [], []
        for s in range(subs):
            block = i * subs + s
            upper = (block + 1) * tq - walked * SB_KEYS
            start = pl.multiple_of(jnp.maximum(upper - SB_KEYS, 0), tq)
            starts.append(start)
            valids.append(col < jnp.minimum(block * tq + row, upper) - start)
        zs = [lax.dot_general(q_ref[s * tq:(s + 1) * tq, head_cols[g]], k_ref[pl.ds(starts[s], SB_KEYS), head_cols[g]],
                              (((1,), (1,)), ((), ())), preferred_element_type=F32) * (HEAD_DIM ** -0.5)
              for s, g in chains]
        log_wins, splits, new_cs = [], [], []
        for n, (s, g) in enumerate(chains):
            z = zs[n]
            softplus = jnp.maximum(z, 0.0) + jnp.log(1.0 + jnp.exp(-jnp.abs(z)))
            log_fail = jnp.where(valids[s], -softplus, 0.0)
            hi = log_fail.astype(BF16)
            lo = (log_fail - hi.astype(F32)).astype(BF16)
            splits.append(jnp.concatenate([hi, lo], axis=1))
            log_wins.append(z - softplus)
            new_cs.append(cs[n] + jnp.sum(log_fail, axis=-1, keepdims=True))
        later = jnp.dot(jnp.concatenate(splits, axis=0), tri_ref[...], preferred_element_type=F32)
        new_accs = []
        for n, (s, g) in enumerate(chains):
            a = jnp.where(valids[s], jnp.exp(log_wins[n] + later[n * tq:(n + 1) * tq, :] + cs[n]), 0.0)
            new_accs.append(accs[n] + jnp.dot(a.astype(BF16), v_ref[pl.ds(starts[s], SB_KEYS), head_cols[g]],
                                              preferred_element_type=F32))
        return walked + 1, new_cs, new_accs

    def cond(state):
        walked, cs, _ = state
        alive = functools.reduce(jnp.maximum, [jnp.max(c) for c in cs])
        keys_left = (i + 1) * subs * tq - walked * SB_KEYS > 0
        return jnp.logical_and(keys_left, alive > SB_LOG_ZERO)

    def body(state):
        return walk_tile(*state)

    init = (0,
            [jnp.zeros((tq, 1), F32) for _ in chains],
            [jnp.zeros((tq, HEAD_DIM), F32) for _ in chains])
    _, _, accs = lax.while_loop(cond, body, init)
    for n, (s, g) in enumerate(chains):
        o_ref[s * tq:(s + 1) * tq, head_cols[g]] = accs[n].astype(o_ref.dtype)


def _stickbreak(proj, n_heads, col0):
    t = proj.shape[0]
    width = SB_HEADS * HEAD_DIM
    qb0 = col0 // width
    kb0 = qb0 + n_heads // SB_HEADS
    vb0 = kb0 + n_heads // SB_HEADS
    s_later = jnp.arange(2 * SB_KEYS)[:, None] % SB_KEYS
    tri = (s_later > jnp.arange(SB_KEYS)[None, :]).astype(BF16)
    resident = lambda c0: pl.BlockSpec((t, width), lambda h, i: (0, c0 + h), pipeline_mode=pl.Buffered(1))
    return pl.pallas_call(
        functools.partial(_stickbreak_kernel, heads=SB_HEADS),
        grid=(n_heads // SB_HEADS, t // SB_QUERIES),
        in_specs=[pl.BlockSpec((SB_QUERIES, width), lambda h, i: (i, qb0 + h)),
                  resident(kb0), resident(vb0),
                  pl.BlockSpec((2 * SB_KEYS, SB_KEYS), lambda h, i: (0, 0))],
        out_specs=pl.BlockSpec((SB_QUERIES, width), lambda h, i: (i, h)),
        out_shape=jax.ShapeDtypeStruct((t, n_heads * HEAD_DIM), BF16),
        compiler_params=_params("parallel", "arbitrary"),
        name="stickbreak",
    )(proj, proj, proj, tri)


def _router_kernel(x_ref, g_ref, w_ref, b_ref, hp_ref, route_ref, counts_ref, carry):
    @pl.when(pl.program_id(0) == 0)
    def _():
        carry[...] = jnp.zeros_like(carry)

    x = x_ref[...]
    tm, d = x.shape
    h = x * lax.rsqrt(jnp.mean(x * x, axis=-1, keepdims=True) + EPS) * g_ref[...]
    h_hi = h.astype(BF16)
    h_hif = h_hi.astype(F32)
    h_lo = (h - h_hif).astype(BF16)
    hi_both = jnp.dot(h_hi, w_ref[...], preferred_element_type=F32)
    logits = (hi_both[:, :ROUTE_LANES] + hi_both[:, ROUTE_LANES:]
              + jnp.dot(h_lo, w_ref[:, :ROUTE_LANES], preferred_element_type=F32)) + b_ref[...]
    lane = lax.broadcasted_iota(jnp.int32, logits.shape, 1)
    neg = -jnp.inf

    def first_max(vals):
        m = jnp.max(vals, axis=-1, keepdims=True)
        idx = jnp.min(jnp.where(vals == m, lane, ROUTE_LANES), axis=-1, keepdims=True)
        return m, idx

    gl = jnp.where(lane < N_GROUPS, logits, neg)
    gmax, g_sel = first_max(gl)
    g_weight = 1.0 / jnp.sum(jnp.exp(gl - gmax), axis=-1, keepdims=True)
    expert = lane - N_GROUPS
    in_group = (expert >= 0) & (expert < N_EXPERTS) & ((expert >> 3) == g_sel)
    el = jnp.where(in_group, logits, neg)
    m1, i1 = first_max(el)
    m2, i2 = first_max(jnp.where(lane == i1, neg, el))
    e21 = jnp.exp(m2 - m1)
    w1 = g_weight / (1.0 + e21)
    w2 = w1 * e21
    sel1 = lane == i1
    sel2 = lane == i2
    chosen = jnp.where(jnp.logical_or(sel1, sel2), 1.0, 0.0)
    row = lax.broadcasted_iota(jnp.int32, (tm, tm), 0)
    col = lax.broadcasted_iota(jnp.int32, (tm, tm), 1)
    earlier = jnp.where(col < row, 1.0, 0.0).astype(BF16)
    before = jnp.dot(earlier, chosen.astype(BF16), preferred_element_type=F32) + carry[...]
    rank1 = jnp.sum(jnp.where(sel1, before, 0.0), axis=-1, keepdims=True)
    rank2 = jnp.sum(jnp.where(sel2, before, 0.0), axis=-1, keepdims=True)
    carry[...] = carry[...] + jnp.sum(chosen, axis=0, keepdims=True)
    counts_ref[...] = carry[...]
    fields = ((i1 - N_GROUPS).astype(F32), (i2 - N_GROUPS).astype(F32), w1, w2, rank1, rank2)
    rec = jnp.zeros_like(logits)
    for k, val in enumerate(fields):
        rec = jnp.where(lane == k, val, rec)
    route_ref[...] = rec
    bits = lax.bitcast_convert_type(h_hif, jnp.uint32)
    hp_ref[...] = (bits[:, :d // 2] >> 16) | (bits[:, d // 2:] & jnp.uint32(0xFFFF0000))


def _router(x, norm_g, wr_g, br_g, wr_e, br_e, tm=512):
    t, d = x.shape
    pad = ROUTE_LANES - N_GROUPS - N_EXPERTS
    w = jnp.concatenate([wr_g.astype(F32), wr_e.astype(F32), jnp.zeros((d, pad), F32)], axis=1)
    b = jnp.concatenate([br_g.astype(F32), br_e.astype(F32), jnp.zeros((pad,), F32)]).reshape(1, ROUTE_LANES)
    w_hi = w.astype(BF16)
    w_lo = (w - w_hi.astype(F32)).astype(BF16)
    tm = min(tm, t)
    return pl.pallas_call(
        _router_kernel,
        grid=(t // tm,),
        in_specs=[pl.BlockSpec((tm, d), lambda i: (i, 0)),
                  pl.BlockSpec((1, d), lambda i: (0, 0)),
                  pl.BlockSpec((d, 2 * ROUTE_LANES), lambda i: (0, 0)),
                  pl.BlockSpec((1, ROUTE_LANES), lambda i: (0, 0))],
        out_specs=[pl.BlockSpec((tm, d // 2), lambda i: (i, 0)),
                   pl.BlockSpec((tm, ROUTE_LANES), lambda i: (i, 0)),
                   pl.BlockSpec((1, ROUTE_LANES), lambda i: (0, 0))],
        out_shape=[jax.ShapeDtypeStruct((t, d // 2), jnp.uint32),
                   jax.ShapeDtypeStruct((t, ROUTE_LANES), F32),
                   jax.ShapeDtypeStruct((1, ROUTE_LANES), F32)],
        scratch_shapes=[pltpu.VMEM((1, ROUTE_LANES), F32)],
        compiler_params=_params("arbitrary"),
        name="router",
    )(x, norm_g.reshape(1, d), jnp.concatenate([w_hi, w_lo], axis=1), b)


def _lookup(table, idx):
    hit = idx[..., None] == jnp.arange(table.shape[0], dtype=jnp.int32)
    return jnp.sum(jnp.where(hit, table, 0), axis=-1)


def _dispatch_tables(route, counts, t):
    counts = counts[0, N_GROUPS:N_GROUPS + N_EXPERTS].astype(jnp.int32)
    padded = (counts + EXPERT_BLOCK - 1) // EXPERT_BLOCK * EXPERT_BLOCK
    padded_end = jnp.cumsum(padded)
    padded_start = padded_end - padded
    expert = route[:, 0:2].astype(jnp.int32)
    rank = route[:, 4:6].astype(jnp.int32)
    assign_slot = (_lookup(padded_start, expert) + rank).reshape(-1)

    n_slots = (-(-2 * t // EXPERT_BLOCK) + N_EXPERTS) * EXPERT_BLOCK
    n_items = N_EXPERTS + n_slots // MOE_MAX_ROWS
    chunks = (padded + MOE_MAX_ROWS - 1) // MOE_MAX_ROWS
    chunk_end = jnp.cumsum(chunks)
    n_used = chunk_end[-1]
    ids = jnp.arange(n_items, dtype=jnp.int32)
    item_e = jnp.minimum(jnp.sum(chunk_end[None, :] <= ids[:, None], axis=1), N_EXPERTS - 1).astype(jnp.int32)
    j = ids - _lookup(chunk_end - chunks, item_e)
    item_start = _lookup(padded_start, item_e) + j * MOE_MAX_ROWS
    item_rows = jnp.clip(_lookup(padded, item_e) - j * MOE_MAX_ROWS, 0, MOE_MAX_ROWS)
    used = ids < n_used
    last_e = jnp.sum(jnp.where(ids == n_used - 1, item_e, 0))
    item_e = jnp.where(used, item_e, last_e).astype(jnp.int32)
    item_start = jnp.where(used, item_start, 0).astype(jnp.int32)
    item_rows = jnp.where(used, item_rows, 0).astype(jnp.int32)
    total = padded_end[-1].reshape(1).astype(jnp.int32)
    return (assign_slot.astype(jnp.int32), padded_start.astype(jnp.int32), padded.astype(jnp.int32), total,
            item_e, item_start, item_rows, n_slots, n_items)


def _dispatch_kernel(as_ref, ps_ref, pd_ref, tot_ref, hp_ref, xs_hbm, zbuf, sem, zsem):
    i = pl.program_id(0)
    tm = hp_ref.shape[0]
    n_blocks = xs_hbm.shape[0] // EXPERT_BLOCK

    def zero_block(first_row):
        return pltpu.make_async_copy(zbuf, xs_hbm.at[pl.ds(pl.multiple_of(first_row, EXPERT_BLOCK), EXPERT_BLOCK)],
                                     zsem)

    def zero_fill(act):
        def per_expert(e, carry):
            @pl.when(pd_ref[e] > 0)
            def _():
                act(zero_block(ps_ref[e] + pd_ref[e] - EXPERT_BLOCK))
            return carry
        lax.fori_loop(0, N_EXPERTS, per_expert, 0)

        def per_tail_block(b, carry):
            act(zero_block(b * EXPERT_BLOCK))
            return carry
        lax.fori_loop(tot_ref[0] // EXPERT_BLOCK, n_blocks, per_tail_block, 0)

    @pl.when(i == 0)
    def _():
        zbuf[...] = jnp.zeros_like(zbuf)
        zero_fill(lambda cp: cp.start())
        zero_fill(lambda cp: cp.wait())

    def body(r, carry):
        a = 2 * (i * tm + r)
        for k in range(2):
            pltpu.make_async_copy(hp_ref.at[pl.ds(r, 1)], xs_hbm.at[pl.ds(as_ref[a + k], 1)], sem).start()
        return carry
    lax.fori_loop(0, tm, body, 0, unroll=8)
    for k in range(2):
        pltpu.make_async_copy(hp_ref, xs_hbm.at[pl.ds(0, tm)], sem).wait()


def _dispatch(hp, assign_slot, padded_start, padded, total, n_slots, tm=1024):
    t, words = hp.shape
    tm = min(tm, t)
    return pl.pallas_call(
        _dispatch_kernel,
        grid_spec=pltpu.PrefetchScalarGridSpec(
            num_scalar_prefetch=4,
            grid=(t // tm,),
            in_specs=[pl.BlockSpec((tm, words), lambda i, *_: (i, 0))],
            out_specs=pl.BlockSpec(memory_space=pl.ANY),
            scratch_shapes=[pltpu.VMEM((EXPERT_BLOCK, words), hp.dtype),
                            pltpu.SemaphoreType.DMA(()),
                            pltpu.SemaphoreType.DMA(())],
        ),
        out_shape=jax.ShapeDtypeStruct((n_slots, words), hp.dtype),
        compiler_params=_params("arbitrary"),
        name="moe_dispatch",
    )(assign_slot, padded_start, padded, total, hp)


def _unpack_pairs(words):
    lo = lax.bitcast_convert_type(words << 16, F32)
    hi = lax.bitcast_convert_type(words & jnp.uint32(0xFFFF0000), F32)
    return lo, hi


def _pack_pairs(lo, hi):
    lo_bits = lax.bitcast_convert_type(lo.astype(BF16).astype(F32), jnp.uint32)
    hi_bits = lax.bitcast_convert_type(hi.astype(BF16).astype(F32), jnp.uint32)
    return (lo_bits >> 16) | (hi_bits & jnp.uint32(0xFFFF0000))


def _expert_items_kernel(ie_ref, is_ref, ir_ref, tot_ref, in_hbm, *refs, n_weights, cast_weights, compute):
    w_refs = refs[:n_weights]
    out_hbm = refs[n_weights]
    inbuf, outbuf, zbuf, w_bf, sem_in, sem_out, sem_z = refs[n_weights + 1:]
    i = pl.program_id(0)
    n_items = pl.num_programs(0)
    max_blocks = inbuf.shape[1] // EXPERT_BLOCK
    n_out_blocks = out_hbm.shape[0] // EXPERT_BLOCK

    def rows_of(hbm, first_row):
        return hbm.at[pl.ds(pl.multiple_of(first_row, EXPERT_BLOCK), EXPERT_BLOCK)]

    def for_blocks(item, act):
        for c in range(max_blocks):
            @pl.when(c * EXPERT_BLOCK < ir_ref[item])
            def _():
                act(c, is_ref[item] + c * EXPERT_BLOCK)

    def in_copy(buf, act):
        return lambda c, row: act(pltpu.make_async_copy(
            rows_of(in_hbm, row), inbuf.at[buf, pl.ds(c * EXPERT_BLOCK, EXPERT_BLOCK)], sem_in.at[buf]))

    def out_copy(buf, act):
        return lambda c, row: act(pltpu.make_async_copy(
            outbuf.at[buf, pl.ds(c * EXPERT_BLOCK, EXPERT_BLOCK)], rows_of(out_hbm, row), sem_out.at[buf]))

    start = lambda cp: cp.start()
    wait = lambda cp: cp.wait()
    buf = i % 2

    @pl.when(i == 0)
    def _():
        for_blocks(0, in_copy(0, start))
        zbuf[...] = jnp.zeros_like(zbuf)

        def tail(act):
            def body(b, carry):
                act(pltpu.make_async_copy(zbuf, rows_of(out_hbm, b * EXPERT_BLOCK), sem_z))
                return carry
            lax.fori_loop(tot_ref[0] // EXPERT_BLOCK, n_out_blocks, body, 0)
        tail(start)
        tail(wait)

    @pl.when(i + 1 < n_items)
    def _():
        for_blocks(i + 1, in_copy(1 - buf, start))

    for_blocks(i, in_copy(buf, wait))

    @pl.when(jnp.logical_or(i == 0, ie_ref[i] != ie_ref[jnp.maximum(i - 1, 0)]))
    def _():
        cast_weights(w_refs, w_bf)

    @pl.when(i >= 2)
    def _():
        for_blocks(i - 2, out_copy(buf, wait))

    n_blk = ir_ref[i] // EXPERT_BLOCK
    for c in range(1, max_blocks + 1):
        @pl.when(n_blk == c)
        def _():
            rows = c * EXPERT_BLOCK
            outbuf[buf, 0:rows, :] = compute(inbuf[buf, 0:rows, :], w_bf)

    for_blocks(i, out_copy(buf, start))

    @pl.when(i == n_items - 1)
    def _():
        @pl.when(i >= 1)
        def _():
            for_blocks(i - 1, out_copy(1 - buf, wait))
        for_blocks(i, out_copy(buf, wait))


def _expert_items_call(name, items, in_rows, weights, out_width, out_dtype, n_w_cols, cast_weights, compute):
    item_e, item_start, item_rows, total = items
    n_slots, in_width = in_rows.shape
    k_dim = weights[0].shape[1]
    w_spec = lambda w: pl.BlockSpec((None,) + w.shape[1:], lambda i, ie, *_: (ie[i], 0, 0))
    return pl.pallas_call(
        functools.partial(_expert_items_kernel, n_weights=len(weights), cast_weights=cast_weights, compute=compute),
        grid_spec=pltpu.PrefetchScalarGridSpec(
            num_scalar_prefetch=4,
            grid=(item_e.shape[0],),
            in_specs=[pl.BlockSpec(memory_space=pl.ANY)] + [w_spec(w) for w in weights],
            out_specs=pl.BlockSpec(memory_space=pl.ANY),
            scratch_shapes=[pltpu.VMEM((2, MOE_MAX_ROWS, in_width), in_rows.dtype),
                            pltpu.VMEM((2, MOE_MAX_ROWS, out_width), out_dtype),
                            pltpu.VMEM((EXPERT_BLOCK, out_width), out_dtype),
                            pltpu.VMEM((k_dim, n_w_cols), BF16),
                            pltpu.SemaphoreType.DMA((2,)),
                            pltpu.SemaphoreType.DMA((2,)),
                            pltpu.SemaphoreType.DMA(())],
        ),
        out_shape=jax.ShapeDtypeStruct((n_slots, out_width), out_dtype),
        compiler_params=pltpu.CompilerParams(dimension_semantics=("arbitrary",),
                                             vmem_limit_bytes=V7X_VMEM_LIMIT_MOE),
        name=name,
    )(item_e, item_start, item_rows, total, in_rows, *weights)


def _cast_gate_up(w_refs, w_bf):
    d_exp = w_refs[0].shape[1]
    w_bf[:, 0:d_exp] = w_refs[0][...].astype(BF16)
    w_bf[:, d_exp:2 * d_exp] = w_refs[1][...].astype(BF16)


def _gate_up(words, w_bf):
    half = words.shape[1]
    d_exp = w_bf.shape[1] // 2
    lo, hi = _unpack_pairs(words)
    gu = (jnp.dot(lo.astype(BF16), w_bf[0:half, :], preferred_element_type=F32)
          + jnp.dot(hi.astype(BF16), w_bf[half:2 * half, :], preferred_element_type=F32))
    gate, up = gu[:, 0:d_exp], gu[:, d_exp:2 * d_exp]
    return (gate * jax.nn.sigmoid(gate) * up).astype(BF16)


def _cast_down(w_refs, w_bf):
    w_bf[...] = w_refs[0][...].astype(BF16)


def _down(hid, w_bf):
    half = w_bf.shape[1] // 2
    y = jnp.dot(hid, w_bf[...], preferred_element_type=F32)
    return _pack_pairs(y[:, 0:half], y[:, half:2 * half])


def _combine_kernel(as_ref, x_ref, route_ref, g_ref, y_hbm, o_ref, *rest, emit_norm):
    if emit_norm:
        h_ref, ybuf, sem = rest
    else:
        ybuf, sem = rest
    i = pl.program_id(0)
    tm, d = x_ref.shape
    half = d // 2

    def gather_row(tile, r, buf):
        a = 2 * (tile * tm + r)
        for k in range(2):
            pltpu.make_async_copy(y_hbm.at[pl.ds(as_ref[a + k], 1)], ybuf.at[buf, k, pl.ds(r, 1)],
                                  sem.at[buf]).start()

    def wait_gather(buf):
        for k in range(2):
            pltpu.make_async_copy(y_hbm.at[pl.ds(0, tm)], ybuf.at[buf, k], sem.at[buf]).wait()

    @pl.when(i == 0)
    def _():
        def body(r, carry):
            gather_row(0, r, 0)
            return carry
        lax.fori_loop(0, tm, body, 0, unroll=8)

    buf = i % 2
    last = pl.num_programs(0) - 1
    next_tile = jnp.minimum(i + 1, last)
    wait_gather(buf)

    def row_chunk(c, carry):
        for rr in range(COMBINE_ROWS):
            gather_row(next_tile, c * COMBINE_ROWS + rr, 1 - buf)
        rows = pl.ds(pl.multiple_of(c * COMBINE_ROWS, COMBINE_ROWS), COMBINE_ROWS)
        w1 = route_ref[rows, 2:3]
        w2 = route_ref[rows, 3:4]
        sq = jnp.zeros((COMBINE_ROWS, 1), F32)
        for c0 in range(0, half, COMBINE_COLS):
            cols = slice(c0, c0 + COMBINE_COLS)
            up_cols = slice(half + c0, half + c0 + COMBINE_COLS)
            lo1, hi1 = _unpack_pairs(ybuf[buf, 0, rows, cols])
            lo2, hi2 = _unpack_pairs(ybuf[buf, 1, rows, cols])
            out_lo = x_ref[rows, cols] + w1 * lo1 + w2 * lo2
            out_hi = x_ref[rows, up_cols] + w1 * hi1 + w2 * hi2
            o_ref[rows, cols] = out_lo
            o_ref[rows, up_cols] = out_hi
            if emit_norm:
                sq = sq + jnp.sum(out_lo * out_lo + out_hi * out_hi, axis=-1, keepdims=True)
        if emit_norm:
            inv = lax.rsqrt(sq * (1.0 / d) + EPS)
            for c0 in range(0, d, COMBINE_COLS):
                cols = slice(c0, c0 + COMBINE_COLS)
                h_ref[rows, cols] = (o_ref[rows, cols] * inv * g_ref[:, cols]).astype(h_ref.dtype)
        return carry
    lax.fori_loop(0, tm // COMBINE_ROWS, row_chunk, 0, unroll=4)

    @pl.when(i == last)
    def _():
        wait_gather(1 - buf)


def _moe(x, norm_g, wr_g, br_g, wr_e, br_e, w_gate, w_up, w_down, next_norm_g=None, tm_combine=256):
    t, d = x.shape
    d_exp = w_gate.shape[2]
    hp, route, counts = _router(x, norm_g, wr_g, br_g, wr_e, br_e)
    (assign_slot, padded_start, padded, total, item_e, item_start, item_rows,
     n_slots, _) = _dispatch_tables(route, counts, t)
    items = (item_e, item_start, item_rows, total)
    xs = _dispatch(hp, assign_slot, padded_start, padded, total, n_slots)
    hid = _expert_items_call("moe_up", items, xs, (w_gate, w_up), d_exp, BF16, 2 * d_exp,
                             _cast_gate_up, _gate_up)
    y = _expert_items_call("moe_down", items, hid, (w_down,), d // 2, jnp.uint32, d, _cast_down, _down)

    tm = min(tm_combine, t)
    emit_norm = next_norm_g is not None
    row_tile = pl.BlockSpec((tm, d), lambda i, a: (i, 0))
    gain = next_norm_g if emit_norm else jnp.ones((d,), F32)
    out = pl.pallas_call(
        functools.partial(_combine_kernel, emit_norm=emit_norm),
        grid_spec=pltpu.PrefetchScalarGridSpec(
            num_scalar_prefetch=1,
            grid=(t // tm,),
            in_specs=[row_tile,
                      pl.BlockSpec((tm, ROUTE_LANES), lambda i, a: (i, 0)),
                      pl.BlockSpec((1, d), lambda i, a: (0, 0)),
                      pl.BlockSpec(memory_space=pl.ANY)],
            out_specs=[row_tile, row_tile] if emit_norm else row_tile,
            scratch_shapes=[pltpu.VMEM((2, 2, tm, d // 2), jnp.uint32),
                            pltpu.SemaphoreType.DMA((2,))],
        ),
        out_shape=([jax.ShapeDtypeStruct((t, d), F32), jax.ShapeDtypeStruct((t, d), BF16)] if emit_norm
                   else jax.ShapeDtypeStruct((t, d), F32)),
        compiler_params=_params("arbitrary"),
        name="moe_combine",
    )(assign_slot, x, route, gain.reshape(1, d).astype(F32), y)
    return tuple(out) if emit_norm else (out, None)


def _even_layer_mixer(x, mix_norm, w_in, a_ln_g, a_ln_b, a_ws, a_bs, b_q_norm, b_k_norm, b_sinks, w_out):
    n_groups = a_ws.shape[0]
    a_width = n_groups * HEAD_DIM
    n_heads = b_sinks.shape[0]
    n_kv = (w_in.shape[1] - 2 * a_width - n_heads * HEAD_DIM) // (2 * HEAD_DIM)
    h = _rmsnorm(x, mix_norm)
    proj = _matmul(h, w_in.astype(BF16), tm=1024, tn=1024)
    y_a = _gmlp(proj, a_ln_g, a_ln_b, a_ws, a_bs)
    y_b = _swa(proj, b_q_norm, b_k_norm, b_sinks, n_heads, n_kv, col0=2 * a_width)
    return _out_proj(x, y_a, y_b, w_out.astype(BF16))


def _odd_layer_mixer(x, h, w_in, c_conv_w, c_conv_b, c_ln_g, c_ln_b, w_out):
    c_width = c_conv_w.shape[1]
    n_heads = (w_in.shape[1] - 2 * c_width) // (3 * HEAD_DIM)
    proj = _matmul(h, w_in.astype(BF16), tm=1024, tn=1024)
    y_c = _conformer(proj, c_conv_w, c_conv_b, c_ln_g, c_ln_b)
    y_d = _stickbreak(proj, n_heads, col0=2 * c_width)
    return _out_proj(x, y_c, y_d, w_out.astype(BF16))


def kernel(x, l0_mix_norm, l0_w_in, l0_a_ln_g, l0_a_ln_b, l0_a_ws, l0_a_bs, l0_b_q_norm, l0_b_k_norm, l0_b_sinks, l0_w_out, l0_ffn_norm, l0_router_group_w, l0_router_group_b, l0_router_expert_w, l0_router_expert_b, l0_w_gate, l0_w_up, l0_w_down, l1_mix_norm, l1_w_in, l1_c_conv_w, l1_c_conv_b, l1_c_ln_g, l1_c_ln_b, l1_w_out, l1_ffn_norm, l1_router_group_w, l1_router_group_b, l1_router_expert_w, l1_router_expert_b, l1_w_gate, l1_w_up, l1_w_down):
    bsz, seq, d = x.shape
    xt = x.reshape(bsz * seq, d)
    assert bsz == 1, "the mixers index positions by row; one sequence per call"
    xt = _even_layer_mixer(xt, l0_mix_norm, l0_w_in, l0_a_ln_g, l0_a_ln_b, l0_a_ws, l0_a_bs,
                           l0_b_q_norm, l0_b_k_norm, l0_b_sinks, l0_w_out)
    xt, h1 = _moe(xt, l0_ffn_norm, l0_router_group_w, l0_router_group_b, l0_router_expert_w,
                  l0_router_expert_b, l0_w_gate, l0_w_up, l0_w_down, next_norm_g=l1_mix_norm)
    xt = _odd_layer_mixer(xt, h1, l1_w_in, l1_c_conv_w, l1_c_conv_b, l1_c_ln_g, l1_c_ln_b, l1_w_out)
    xt, _ = _moe(xt, l1_ffn_norm, l1_router_group_w, l1_router_group_b, l1_router_expert_w,
                 l1_router_expert_b, l1_w_gate, l1_w_up, l1_w_down)
    return xt.reshape(bsz, seq, d)
```

```python
import functools

import jax
import jax.numpy as jnp
from jax import lax
from jax.experimental import pallas as pl
from jax.experimental.pallas import tpu as pltpu

F32 = jnp.float32
BF16 = jnp.bfloat16

HEAD_DIM = 128
CHUNK = 128
EPS = 1e-6
N_GROUPS = 8
EXPERTS_PER_GROUP = 8
N_EXPERTS = N_GROUPS * EXPERTS_PER_GROUP
EXPERT_BLOCK = 128
CONV_WIDTH = 31
SUBLANES = 8
CONV_ROWS = 128
CONV_HALO = 32
ROUTE_LANES = 128
GMLP_ROWS = 1024
SWA_KV_HEADS = 4
SB_LOG_ZERO = -104.0
SB_HEADS = 4
SB_QUERIES = 256
SB_KEYS = 384
COMBINE_ROWS = 16
COMBINE_COLS = 512
MOE_MAX_ROWS = 768
V7X_VMEM_LIMIT = 56 * 1024 * 1024
V7X_VMEM_LIMIT_MOE = 60 * 1024 * 1024


def _params(*sem):
    return pltpu.CompilerParams(dimension_semantics=sem, vmem_limit_bytes=V7X_VMEM_LIMIT)


def _rmsnorm_kernel(x_ref, g_ref, o_ref):
    x = x_ref[...]
    ms = jnp.mean(x * x, axis=-1, keepdims=True)
    o_ref[...] = (x * lax.rsqrt(ms + EPS) * g_ref[...]).astype(o_ref.dtype)


def _rmsnorm(x, g, tm=512):
    t, d = x.shape
    return pl.pallas_call(
        _rmsnorm_kernel,
        grid=(t // tm,),
        in_specs=[pl.BlockSpec((tm, d), lambda i: (i, 0)),
                  pl.BlockSpec((1, d), lambda i: (0, 0))],
        out_specs=pl.BlockSpec((tm, d), lambda i: (i, 0)),
        out_shape=jax.ShapeDtypeStruct((t, d), BF16),
        compiler_params=_params("parallel"),
        name="rmsnorm",
    )(x, g.reshape(1, d))


def _matmul_kernel(a_ref, w_ref, o_ref):
    o_ref[...] = jnp.dot(a_ref[...], w_ref[...], preferred_element_type=F32).astype(o_ref.dtype)


def _matmul(a, w, tm, tn):
    m, k = a.shape
    n = w.shape[1]
    tm = min(tm, m)
    return pl.pallas_call(
        _matmul_kernel,
        grid=(m // tm, n // tn),
        in_specs=[pl.BlockSpec((tm, k), lambda i, j: (i, 0)),
                  pl.BlockSpec((k, tn), lambda i, j: (0, j))],
        out_specs=pl.BlockSpec((tm, tn), lambda i, j: (i, j)),
        out_shape=jax.ShapeDtypeStruct((m, n), BF16),
        compiler_params=_params("parallel", "arbitrary"),
        name="in_proj",
    )(a, w)


def _out_proj_kernel(x_ref, a_ref, b_ref, wa_ref, wb_ref, o_ref):
    acc = jnp.dot(a_ref[...], wa_ref[...], preferred_element_type=F32)
    acc = acc + jnp.dot(b_ref[...], wb_ref[...], preferred_element_type=F32)
    o_ref[...] = x_ref[...] + acc


def _out_proj(x, ya, yb, w, tm=1024, tn=1024):
    t, d = x.shape
    ka, kb = ya.shape[1], yb.shape[1]
    assert ka == kb and w.shape == (ka + kb, d)
    tm = min(tm, t)
    return pl.pallas_call(
        _out_proj_kernel,
        grid=(t // tm, d // tn),
        in_specs=[pl.BlockSpec((tm, tn), lambda i, j: (i, j)),
                  pl.BlockSpec((tm, ka), lambda i, j: (i, 0)),
                  pl.BlockSpec((tm, kb), lambda i, j: (i, 0)),
                  pl.BlockSpec((ka, tn), lambda i, j: (0, j)),
                  pl.BlockSpec((kb, tn), lambda i, j: (1, j))],
        out_specs=pl.BlockSpec((tm, tn), lambda i, j: (i, j)),
        out_shape=jax.ShapeDtypeStruct((t, d), F32),
        compiler_params=_params("parallel", "arbitrary"),
        name="out_proj",
    )(x, ya, yb, w, w)


def _layer_norm(x, g, b):
    mu = jnp.mean(x, axis=-1, keepdims=True)
    xc = x - mu
    var = jnp.mean(xc * xc, axis=-1, keepdims=True)
    return xc * lax.rsqrt(var + EPS) * g + b


def _gelu(x):
    return 0.5 * x * (1.0 + lax.erf(x * (2.0 ** -0.5)))


def _gmlp_kernel(u_ref, v_ref, lng_ref, lnb_ref, w_ref, bs_ref, o_ref):
    n_groups = w_ref.shape[0]
    group_cols = [slice(g * HEAD_DIM, (g + 1) * HEAD_DIM) for g in range(n_groups)]
    for c in range(u_ref.shape[0] // CHUNK):
        rows = slice(c * CHUNK, (c + 1) * CHUNK)
        v = _gelu(v_ref[rows, :].astype(F32))
        vn = _layer_norm(v, lng_ref[...], lnb_ref[...]).astype(BF16)
        mixed = [jnp.dot(w_ref[g], vn[:, cols], preferred_element_type=F32) for g, cols in enumerate(group_cols)]
        for g, cols in enumerate(group_cols):
            u = _gelu(u_ref[rows, cols].astype(F32))
            o_ref[rows, cols] = (u * (mixed[g] + bs_ref[:, cols])).astype(o_ref.dtype)


def _gmlp(proj, ln_g, ln_b, w_s, b_s):
    t = proj.shape[0]
    n_groups = w_s.shape[0]
    width = n_groups * HEAD_DIM
    causal = jnp.tril(jnp.ones((CHUNK, CHUNK), dtype=bool))
    w = jnp.where(causal[None], w_s, 0.0).astype(BF16)
    bs = jnp.repeat(b_s.T.astype(F32), HEAD_DIM, axis=1)
    tm = min(GMLP_ROWS, t)
    return pl.pallas_call(
        _gmlp_kernel,
        grid=(t // tm,),
        in_specs=[pl.BlockSpec((tm, width), lambda i: (i, 0)),
                  pl.BlockSpec((tm, width), lambda i: (i, 1)),
                  pl.BlockSpec((1, width), lambda i: (0, 0)),
                  pl.BlockSpec((1, width), lambda i: (0, 0)),
                  pl.BlockSpec((n_groups, CHUNK, CHUNK), lambda i: (0, 0, 0)),
                  pl.BlockSpec((CHUNK, width), lambda i: (0, 0))],
        out_specs=pl.BlockSpec((tm, width), lambda i: (i, 0)),
        out_shape=jax.ShapeDtypeStruct((t, width), BF16),
        compiler_params=_params("parallel"),
        name="gmlp",
    )(proj, proj, ln_g.reshape(1, width), ln_b.reshape(1, width), w, bs)


def _swa_kernel(slopes_ref, sinks_ref, q_ref, kc_ref, kp_ref, vc_ref, vp_ref, qg_ref, kg_ref, o_ref, *, kvs, grp):
    n = pl.program_id(0)
    first_head = pl.program_id(1) * kvs * grp
    qi = lax.broadcasted_iota(jnp.int32, (CHUNK, 2 * CHUNK), 0)
    kj = lax.broadcasted_iota(jnp.int32, (CHUNK, 2 * CHUNK), 1)
    dist = qi + CHUNK - kj
    valid = (dist >= 0) & (dist < CHUNK) & (n * CHUNK - CHUNK + kj >= 0)
    distf = dist.astype(F32)
    head_cols = lambda j: slice(j * HEAD_DIM, (j + 1) * HEAD_DIM)
    scores, values = [], []
    for kv in range(kvs):
        k = jnp.concatenate([kp_ref[:, head_cols(kv)], kc_ref[:, head_cols(kv)]], axis=0).astype(F32)
        k = k * lax.rsqrt(jnp.mean(k * k, axis=-1, keepdims=True) + EPS) * kg_ref[...]
        kb = k.astype(BF16)
        values.append(jnp.concatenate([vp_ref[:, head_cols(kv)], vc_ref[:, head_cols(kv)]], axis=0))
        for g in range(grp):
            q = q_ref[:, head_cols(kv * grp + g)].astype(F32)
            q = q * lax.rsqrt(jnp.mean(q * q, axis=-1, keepdims=True) + EPS) * qg_ref[...]
            scores.append(lax.dot_general(q.astype(BF16), kb, (((1,), (1,)), ((), ())),
                                          preferred_element_type=F32) * (HEAD_DIM ** -0.5))
    probs, denoms = [], []
    for j in range(kvs * grp):
        slope = slopes_ref[first_head + j]
        sink = sinks_ref[first_head + j]
        s = jnp.where(valid, scores[j] - slope * distf, -jnp.inf)
        m = jnp.maximum(jnp.max(s, axis=-1, keepdims=True), sink)
        p = jnp.exp(s - m)
        denoms.append(jnp.sum(p, axis=-1, keepdims=True) + jnp.exp(sink - m))
        probs.append(p.astype(BF16))
    for j in range(kvs * grp):
        o = jnp.dot(probs[j], values[j // grp], preferred_element_type=F32) / denoms[j]
        o_ref[:, head_cols(j)] = o.astype(o_ref.dtype)


def _swa(proj, q_gain, k_gain, sinks, n_heads, n_kv, col0):
    t = proj.shape[0]
    grp = n_heads // n_kv
    kvs = min(SWA_KV_HEADS, n_kv)
    q_width, kv_width = kvs * grp * HEAD_DIM, kvs * HEAD_DIM
    assert col0 % q_width == 0 and (col0 + n_heads * HEAD_DIM) % kv_width == 0 and n_kv % kvs == 0
    qb0 = col0 // q_width
    kb0 = (col0 + n_heads * HEAD_DIM) // kv_width
    vb0 = kb0 + n_kv // kvs
    slopes = 2.0 ** (-8.0 * jnp.arange(1, n_heads + 1, dtype=F32) / n_heads)
    prev = lambda n: jnp.maximum(n - 1, 0)
    return pl.pallas_call(
        functools.partial(_swa_kernel, kvs=kvs, grp=grp),
        grid_spec=pltpu.PrefetchScalarGridSpec(
            num_scalar_prefetch=2,
            grid=(t // CHUNK, n_kv // kvs),
            in_specs=[pl.BlockSpec((CHUNK, q_width), lambda n, h, *_: (n, qb0 + h)),
                      pl.BlockSpec((CHUNK, kv_width), lambda n, h, *_: (n, kb0 + h)),
                      pl.BlockSpec((CHUNK, kv_width), lambda n, h, *_: (prev(n), kb0 + h)),
                      pl.BlockSpec((CHUNK, kv_width), lambda n, h, *_: (n, vb0 + h)),
                      pl.BlockSpec((CHUNK, kv_width), lambda n, h, *_: (prev(n), vb0 + h)),
                      pl.BlockSpec((1, HEAD_DIM), lambda n, h, *_: (0, 0)),
                      pl.BlockSpec((1, HEAD_DIM), lambda n, h, *_: (0, 0))],
            out_specs=pl.BlockSpec((CHUNK, q_width), lambda n, h, *_: (n, h)),
        ),
        out_shape=jax.ShapeDtypeStruct((t, n_heads * HEAD_DIM), BF16),
        compiler_params=_params("parallel", "arbitrary"),
        name="swa",
    )(slopes, sinks.astype(F32), proj, proj, proj, proj, proj,
      q_gain.reshape(1, HEAD_DIM).astype(F32), k_gain.reshape(1, HEAD_DIM).astype(F32))


def _conformer_kernel(a_ref, g_ref, ap_ref, gp_ref, w_ref, cb_ref, lg_ref, lb_ref, o_ref, hbuf, ybuf):
    i = pl.program_id(0)
    tc, width = a_ref.shape
    span = CONV_HALO + tc
    hp = ap_ref[...].astype(F32) * jax.nn.sigmoid(gp_ref[...].astype(F32))
    hbuf[0, 0:CONV_HALO, :] = jnp.where(i > 0, hp, 0.0)
    hbuf[0, CONV_HALO:span, :] = a_ref[...].astype(F32) * jax.nn.sigmoid(g_ref[...].astype(F32))
    hbuf[0, span:span + SUBLANES, :] = jnp.zeros((SUBLANES, width), F32)
    for r in range(1, SUBLANES):
        hbuf[r, 0:span, :] = hbuf[0, r:r + span, :]
    base = CONV_HALO - (CONV_WIDTH - 1)
    for c in range(width // HEAD_DIM):
        cols = slice(c * HEAD_DIM, (c + 1) * HEAD_DIM)
        for t0 in range(0, tc, CONV_ROWS):
            acc = jnp.zeros((CONV_ROWS, HEAD_DIM), F32)
            for r in range(SUBLANES):
                taps = [j for j in range(CONV_WIDTH) if (base + j) % SUBLANES == r]
                first = t0 + base + taps[0] - r
                rows = hbuf[r, first:t0 + base + taps[-1] - r + CONV_ROWS, cols]
                for j in taps:
                    off = t0 + base + j - r - first
                    acc = acc + w_ref[j:j + 1, cols] * rows[off:off + CONV_ROWS, :]
            ybuf[t0:t0 + CONV_ROWS, cols] = acc
    y = _layer_norm(ybuf[...] + cb_ref[...], lg_ref[...], lb_ref[...])
    o_ref[...] = (y * jax.nn.sigmoid(y)).astype(o_ref.dtype)


def _conformer(proj, conv_w, conv_b, ln_g, ln_b, tc=256):
    t = proj.shape[0]
    width = conv_w.shape[1]
    tc = min(tc, t)
    halo_blocks = tc // CONV_HALO
    prev = lambda i: jnp.maximum(i * halo_blocks - 1, 0)
    vec = lambda: pl.BlockSpec((1, width), lambda i: (0, 0))
    return pl.pallas_call(
        _conformer_kernel,
        grid=(t // tc,),
        in_specs=[pl.BlockSpec((tc, width), lambda i: (i, 0)),
                  pl.BlockSpec((tc, width), lambda i: (i, 1)),
                  pl.BlockSpec((CONV_HALO, width), lambda i: (prev(i), 0)),
                  pl.BlockSpec((CONV_HALO, width), lambda i: (prev(i), 1)),
                  pl.BlockSpec((CONV_WIDTH, width), lambda i: (0, 0)),
                  vec(), vec(), vec()],
        out_specs=pl.BlockSpec((tc, width), lambda i: (i, 0)),
        out_shape=jax.ShapeDtypeStruct((t, width), BF16),
        scratch_shapes=[pltpu.VMEM((SUBLANES, CONV_HALO + tc + SUBLANES, width), F32),
                        pltpu.VMEM((tc, width), F32)],
        compiler_params=_params("parallel"),
        name="conformer",
    )(proj, proj, proj, proj, conv_w.astype(F32), conv_b.reshape(1, width).astype(F32),
      ln_g.reshape(1, width).astype(F32), ln_b.reshape(1, width).astype(F32))


def _stickbreak_kernel(q_ref, k_ref, v_ref, tri_ref, o_ref, *, heads):
    i = pl.program_id(1)
    tq = CHUNK
    subs = q_ref.shape[0] // tq
    col = lax.broadcasted_iota(jnp.int32, (tq, SB_KEYS), 1)
    row = lax.broadcasted_iota(jnp.int32, (tq, 1), 0)
    head_cols = [slice(g * HEAD_DIM, (g + 1) * HEAD_DIM) for g in range(heads)]
    chains = [(s, g) for s in range(subs) for g in range(heads)]

    def walk_tile(walked, cs, accs):
        starts, valids = [], []
        for s in range(subs):
            block = i * subs + s
            upper = (block + 1) * tq - walked * SB_KEYS
            start = pl.multiple_of(jnp.maximum(upper - SB_KEYS, 0), tq)
            starts.append(start)
            valids.append(col < jnp.minimum(block * tq + row, upper) - start)
        zs = [lax.dot_general(q_ref[s * tq:(s + 1) * tq, head_cols[g]], k_ref[pl.ds(starts[s], SB_KEYS), head_cols[g]],
                              (((1,), (1,)), ((), ())), preferred_element_type=F32) * (HEAD_DIM ** -0.5)
              for s, g in chains]
        log_wins, splits, new_cs = [], [], []
        for n, (s, g) in enumerate(chains):
            z = zs[n]
            softplus = jnp.maximum(z, 0.0) + jnp.log(1.0 + jnp.exp(-jnp.abs(z)))
            log_fail = jnp.where(valids[s], -softplus, 0.0)
            hi = log_fail.astype(BF16)
            lo = (log_fail - hi.astype(F32)).astype(BF16)
            splits.append(jnp.concatenate([hi, lo], axis=1))
            log_wins.append(z - softplus)
            new_cs.append(cs[n] + jnp.sum(log_fail, axis=-1, keepdims=True))
        later = jnp.dot(jnp.concatenate(splits, axis=0), tri_ref[...], preferred_element_type=F32)
        new_accs = []
        for n, (s, g) in enumerate(chains):
            a = jnp.where(valids[s], jnp.exp(log_wins[n] + later[n * tq:(n + 1) * tq, :] + cs[n]), 0.0)
            new_accs.append(accs[n] + jnp.dot(a.astype(BF16), v_ref[pl.ds(starts[s], SB_KEYS), head_cols[g]],
                                              preferred_element_type=F32))
        return walked + 1, new_cs, new_accs

    def cond(state):
        walked, cs, _ = state
        alive = functools.reduce(jnp.maximum, [jnp.max(c) for c in cs])
        keys_left = (i + 1) * subs * tq - walked * SB_KEYS > 0
        return jnp.logical_and(keys_left, alive > SB_LOG_ZERO)

    def body(state):
        return walk_tile(*state)

    init = (0,
            [jnp.zeros((tq, 1), F32) for _ in chains],
            [jnp.zeros((tq, HEAD_DIM), F32) for _ in chains])
    _, _, accs = lax.while_loop(cond, body, init)
    for n, (s, g) in enumerate(chains):
        o_ref[s * tq:(s + 1) * tq, head_cols[g]] = accs[n].astype(o_ref.dtype)


def _stickbreak(proj, n_heads, col0):
    t = proj.shape[0]
    width = SB_HEADS * HEAD_DIM
    qb0 = col0 // width
    kb0 = qb0 + n_heads // SB_HEADS
    vb0 = kb0 + n_heads // SB_HEADS
    s_later = jnp.arange(2 * SB_KEYS)[:, None] % SB_KEYS
    tri = (s_later > jnp.arange(SB_KEYS)[None, :]).astype(BF16)
    resident = lambda c0: pl.BlockSpec((t, width), lambda h, i: (0, c0 + h), pipeline_mode=pl.Buffered(1))
    return pl.pallas_call(
        functools.partial(_stickbreak_kernel, heads=SB_HEADS),
        grid=(n_heads // SB_HEADS, t // SB_QUERIES),
        in_specs=[pl.BlockSpec((SB_QUERIES, width), lambda h, i: (i, qb0 + h)),
                  resident(kb0), resident(vb0),
                  pl.BlockSpec((2 * SB_KEYS, SB_KEYS), lambda h, i: (0, 0))],
        out_specs=pl.BlockSpec((SB_QUERIES, width), lambda h, i: (i, h)),
        out_shape=jax.ShapeDtypeStruct((t, n_heads * HEAD_DIM), BF16),
        compiler_params=_params("parallel", "arbitrary"),
        name="stickbreak",
    )(proj, proj, proj, tri)


def _router_kernel(x_ref, g_ref, w_ref, b_ref, hp_ref, route_ref, counts_ref, carry):
    @pl.when(pl.program_id(0) == 0)
    def _():
        carry[...] = jnp.zeros_like(carry)

    x = x_ref[...]
    tm, d = x.shape
    h = x * lax.rsqrt(jnp.mean(x * x, axis=-1, keepdims=True) + EPS) * g_ref[...]
    h_hi = h.astype(BF16)
    h_hif = h_hi.astype(F32)
    h_lo = (h - h_hif).astype(BF16)
    hi_both = jnp.dot(h_hi, w_ref[...], preferred_element_type=F32)
    logits = (hi_both[:, :ROUTE_LANES] + hi_both[:, ROUTE_LANES:]
              + jnp.dot(h_lo, w_ref[:, :ROUTE_LANES], preferred_element_type=F32)) + b_ref[...]
    lane = lax.broadcasted_iota(jnp.int32, logits.shape, 1)
    neg = -jnp.inf

    def first_max(vals):
        m = jnp.max(vals, axis=-1, keepdims=True)
        idx = jnp.min(jnp.where(vals == m, lane, ROUTE_LANES), axis=-1, keepdims=True)
        return m, idx

    gl = jnp.where(lane < N_GROUPS, logits, neg)
    gmax, g_sel = first_max(gl)
    g_weight = 1.0 / jnp.sum(jnp.exp(gl - gmax), axis=-1, keepdims=True)
    expert = lane - N_GROUPS
    in_group = (expert >= 0) & (expert < N_EXPERTS) & ((expert >> 3) == g_sel)
    el = jnp.where(in_group, logits, neg)
    m1, i1 = first_max(el)
    m2, i2 = first_max(jnp.where(lane == i1, neg, el))
    e21 = jnp.exp(m2 - m1)
    w1 = g_weight / (1.0 + e21)
    w2 = w1 * e21
    sel1 = lane == i1
    sel2 = lane == i2
    chosen = jnp.where(jnp.logical_or(sel1, sel2), 1.0, 0.0)
    row = lax.broadcasted_iota(jnp.int32, (tm, tm), 0)
    col = lax.broadcasted_iota(jnp.int32, (tm, tm), 1)
    earlier = jnp.where(col < row, 1.0, 0.0).astype(BF16)
    before = jnp.dot(earlier, chosen.astype(BF16), preferred_element_type=F32) + carry[...]
    rank1 = jnp.sum(jnp.where(sel1, before, 0.0), axis=-1, keepdims=True)
    rank2 = jnp.sum(jnp.where(sel2, before, 0.0), axis=-1, keepdims=True)
    carry[...] = carry[...] + jnp.sum(chosen, axis=0, keepdims=True)
    counts_ref[...] = carry[...]
    fields = ((i1 - N_GROUPS).astype(F32), (i2 - N_GROUPS).astype(F32), w1, w2, rank1, rank2)
    rec = jnp.zeros_like(logits)
    for k, val in enumerate(fields):
        rec = jnp.where(lane == k, val, rec)
    route_ref[...] = rec
    bits = lax.bitcast_convert_type(h_hif, jnp.uint32)
    hp_ref[...] = (bits[:, :d // 2] >> 16) | (bits[:, d // 2:] & jnp.uint32(0xFFFF0000))


def _router(x, norm_g, wr_g, br_g, wr_e, br_e, tm=512):
    t, d = x.shape
    pad = ROUTE_LANES - N_GROUPS - N_EXPERTS
    w = jnp.concatenate([wr_g.astype(F32), wr_e.astype(F32), jnp.zeros((d, pad), F32)], axis=1)
    b = jnp.concatenate([br_g.astype(F32), br_e.astype(F32), jnp.zeros((pad,), F32)]).reshape(1, ROUTE_LANES)
    w_hi = w.astype(BF16)
    w_lo = (w - w_hi.astype(F32)).astype(BF16)
    tm = min(tm, t)
    return pl.pallas_call(
        _router_kernel,
        grid=(t // tm,),
        in_specs=[pl.BlockSpec((tm, d), lambda i: (i, 0)),
                  pl.BlockSpec((1, d), lambda i: (0, 0)),
                  pl.BlockSpec((d, 2 * ROUTE_LANES), lambda i: (0, 0)),
                  pl.BlockSpec((1, ROUTE_LANES), lambda i: (0, 0))],
        out_specs=[pl.BlockSpec((tm, d // 2), lambda i: (i, 0)),
                   pl.BlockSpec((tm, ROUTE_LANES), lambda i: (i, 0)),
                   pl.BlockSpec((1, ROUTE_LANES), lambda i: (0, 0))],
        out_shape=[jax.ShapeDtypeStruct((t, d // 2), jnp.uint32),
                   jax.ShapeDtypeStruct((t, ROUTE_LANES), F32),
                   jax.ShapeDtypeStruct((1, ROUTE_LANES), F32)],
        scratch_shapes=[pltpu.VMEM((1, ROUTE_LANES), F32)],
        compiler_params=_params("arbitrary"),
        name="router",
    )(x, norm_g.reshape(1, d), jnp.concatenate([w_hi, w_lo], axis=1), b)


def _lookup(table, idx):
    hit = idx[..., None] == jnp.arange(table.shape[0], dtype=jnp.int32)
    return jnp.sum(jnp.where(hit, table, 0), axis=-1)


def _dispatch_tables(route, counts, t):
    counts = counts[0, N_GROUPS:N_GROUPS + N_EXPERTS].astype(jnp.int32)
    padded = (counts + EXPERT_BLOCK - 1) // EXPERT_BLOCK * EXPERT_BLOCK
    padded_end = jnp.cumsum(padded)
    padded_start = padded_end - padded
    expert = route[:, 0:2].astype(jnp.int32)
    rank = route[:, 4:6].astype(jnp.int32)
    assign_slot = (_lookup(padded_start, expert) + rank).reshape(-1)

    n_slots = (-(-2 * t // EXPERT_BLOCK) + N_EXPERTS) * EXPERT_BLOCK
    n_items = N_EXPERTS + n_slots // MOE_MAX_ROWS
    chunks = (padded + MOE_MAX_ROWS - 1) // MOE_MAX_ROWS
    chunk_end = jnp.cumsum(chunks)
    n_used = chunk_end[-1]
    ids = jnp.arange(n_items, dtype=jnp.int32)
    item_e = jnp.minimum(jnp.sum(chunk_end[None, :] <= ids[:, None], axis=1), N_EXPERTS - 1).astype(jnp.int32)
    j = ids - _lookup(chunk_end - chunks, item_e)
    item_start = _lookup(padded_start, item_e) + j * MOE_MAX_ROWS
    item_rows = jnp.clip(_lookup(padded, item_e) - j * MOE_MAX_ROWS, 0, MOE_MAX_ROWS)
    used = ids < n_used
    last_e = jnp.sum(jnp.where(ids == n_used - 1, item_e, 0))
    item_e = jnp.where(used, item_e, last_e).astype(jnp.int32)
    item_start = jnp.where(used, item_start, 0).astype(jnp.int32)
    item_rows = jnp.where(used, item_rows, 0).astype(jnp.int32)
    total = padded_end[-1].reshape(1).astype(jnp.int32)
    return (assign_slot.astype(jnp.int32), padded_start.astype(jnp.int32), padded.astype(jnp.int32), total,
            item_e, item_start, item_rows, n_slots, n_items)


def _dispatch_kernel(as_ref, ps_ref, pd_ref, tot_ref, hp_ref, xs_hbm, zbuf, sem, zsem):
    i = pl.program_id(0)
    tm = hp_ref.shape[0]
    n_blocks = xs_hbm.shape[0] // EXPERT_BLOCK

    def zero_block(first_row):
        return pltpu.make_async_copy(zbuf, xs_hbm.at[pl.ds(pl.multiple_of(first_row, EXPERT_BLOCK), EXPERT_BLOCK)],
                                     zsem)

    def zero_fill(act):
        def per_expert(e, carry):
            @pl.when(pd_ref[e] > 0)
            def _():
                act(zero_block(ps_ref[e] + pd_ref[e] - EXPERT_BLOCK))
            return carry
        lax.fori_loop(0, N_EXPERTS, per_expert, 0)

        def per_tail_block(b, carry):
            act(zero_block(b * EXPERT_BLOCK))
            return carry
        lax.fori_loop(tot_ref[0] // EXPERT_BLOCK, n_blocks, per_tail_block, 0)

    @pl.when(i == 0)
    def _():
        zbuf[...] = jnp.zeros_like(zbuf)
        zero_fill(lambda cp: cp.start())
        zero_fill(lambda cp: cp.wait())

    def body(r, carry):
        a = 2 * (i * tm + r)
        for k in range(2):
            pltpu.make_async_copy(hp_ref.at[pl.ds(r, 1)], xs_hbm.at[pl.ds(as_ref[a + k], 1)], sem).start()
        return carry
    lax.fori_loop(0, tm, body, 0, unroll=8)
    for k in range(2):
        pltpu.make_async_copy(hp_ref, xs_hbm.at[pl.ds(0, tm)], sem).wait()


def _dispatch(hp, assign_slot, padded_start, padded, total, n_slots, tm=1024):
    t, words = hp.shape
    tm = min(tm, t)
    return pl.pallas_call(
        _dispatch_kernel,
        grid_spec=pltpu.PrefetchScalarGridSpec(
            num_scalar_prefetch=4,
            grid=(t // tm,),
            in_specs=[pl.BlockSpec((tm, words), lambda i, *_: (i, 0))],
            out_specs=pl.BlockSpec(memory_space=pl.ANY),
            scratch_shapes=[pltpu.VMEM((EXPERT_BLOCK, words), hp.dtype),
                            pltpu.SemaphoreType.DMA(()),
                            pltpu.SemaphoreType.DMA(())],
        ),
        out_shape=jax.ShapeDtypeStruct((n_slots, words), hp.dtype),
        compiler_params=_params("arbitrary"),
        name="moe_dispatch",
    )(assign_slot, padded_start, padded, total, hp)


def _unpack_pairs(words):
    lo = lax.bitcast_convert_type(words << 16, F32)
    hi = lax.bitcast_convert_type(words & jnp.uint32(0xFFFF0000), F32)
    return lo, hi


def _pack_pairs(lo, hi):
    lo_bits = lax.bitcast_convert_type(lo.astype(BF16).astype(F32), jnp.uint32)
    hi_bits = lax.bitcast_convert_type(hi.astype(BF16).astype(F32), jnp.uint32)
    return (lo_bits >> 16) | (hi_bits & jnp.uint32(0xFFFF0000))


def _expert_items_kernel(ie_ref, is_ref, ir_ref, tot_ref, in_hbm, *refs, n_weights, cast_weights, compute):
    w_refs = refs[:n_weights]
    out_hbm = refs[n_weights]
    inbuf, outbuf, zbuf, w_bf, sem_in, sem_out, sem_z = refs[n_weights + 1:]
    i = pl.program_id(0)
    n_items = pl.num_programs(0)
    max_blocks = inbuf.shape[1] // EXPERT_BLOCK
    n_out_blocks = out_hbm.shape[0] // EXPERT_BLOCK

    def rows_of(hbm, first_row):
        return hbm.at[pl.ds(pl.multiple_of(first_row, EXPERT_BLOCK), EXPERT_BLOCK)]

    def for_blocks(item, act):
        for c in range(max_blocks):
            @pl.when(c * EXPERT_BLOCK < ir_ref[item])
            def _():
                act(c, is_ref[item] + c * EXPERT_BLOCK)

    def in_copy(buf, act):
        return lambda c, row: act(pltpu.make_async_copy(
            rows_of(in_hbm, row), inbuf.at[buf, pl.ds(c * EXPERT_BLOCK, EXPERT_BLOCK)], sem_in.at[buf]))

    def out_copy(buf, act):
        return lambda c, row: act(pltpu.make_async_copy(
            outbuf.at[buf, pl.ds(c * EXPERT_BLOCK, EXPERT_BLOCK)], rows_of(out_hbm, row), sem_out.at[buf]))

    start = lambda cp: cp.start()
    wait = lambda cp: cp.wait()
    buf = i % 2

    @pl.when(i == 0)
    def _():
        for_blocks(0, in_copy(0, start))
        zbuf[...] = jnp.zeros_like(zbuf)

        def tail(act):
            def body(b, carry):
                act(pltpu.make_async_copy(zbuf, rows_of(out_hbm, b * EXPERT_BLOCK), sem_z))
                return carry
            lax.fori_loop(tot_ref[0] // EXPERT_BLOCK, n_out_blocks, body, 0)
        tail(start)
        tail(wait)

    @pl.when(i + 1 < n_items)
    def _():
        for_blocks(i + 1, in_copy(1 - buf, start))

    for_blocks(i, in_copy(buf, wait))

    @pl.when(jnp.logical_or(i == 0, ie_ref[i] != ie_ref[jnp.maximum(i - 1, 0)]))
    def _():
        cast_weights(w_refs, w_bf)

    @pl.when(i >= 2)
    def _():
        for_blocks(i - 2, out_copy(buf, wait))

    n_blk = ir_ref[i] // EXPERT_BLOCK
    for c in range(1, max_blocks + 1):
        @pl.when(n_blk == c)
        def _():
            rows = c * EXPERT_BLOCK
            outbuf[buf, 0:rows, :] = compute(inbuf[buf, 0:rows, :], w_bf)

    for_blocks(i, out_copy(buf, start))

    @pl.when(i == n_items - 1)
    def _():
        @pl.when(i >= 1)
        def _():
            for_blocks(i - 1, out_copy(1 - buf, wait))
        for_blocks(i, out_copy(buf, wait))


def _expert_items_call(name, items, in_rows, weights, out_width, out_dtype, n_w_cols, cast_weights, compute):
    item_e, item_start, item_rows, total = items
    n_slots, in_width = in_rows.shape
    k_dim = weights[0].shape[1]
    w_spec = lambda w: pl.BlockSpec((None,) + w.shape[1:], lambda i, ie, *_: (ie[i], 0, 0))
    return pl.pallas_call(
        functools.partial(_expert_items_kernel, n_weights=len(weights), cast_weights=cast_weights, compute=compute),
        grid_spec=pltpu.PrefetchScalarGridSpec(
            num_scalar_prefetch=4,
            grid=(item_e.shape[0],),
            in_specs=[pl.BlockSpec(memory_space=pl.ANY)] + [w_spec(w) for w in weights],
            out_specs=pl.BlockSpec(memory_space=pl.ANY),
            scratch_shapes=[pltpu.VMEM((2, MOE_MAX_ROWS, in_width), in_rows.dtype),
                            pltpu.VMEM((2, MOE_MAX_ROWS, out_width), out_dtype),
                            pltpu.VMEM((EXPERT_BLOCK, out_width), out_dtype),
                            pltpu.VMEM((k_dim, n_w_cols), BF16),
                            pltpu.SemaphoreType.DMA((2,)),
                            pltpu.SemaphoreType.DMA((2,)),
                            pltpu.SemaphoreType.DMA(())],
        ),
        out_shape=jax.ShapeDtypeStruct((n_slots, out_width), out_dtype),
        compiler_params=pltpu.CompilerParams(dimension_semantics=("arbitrary",),
                                             vmem_limit_bytes=V7X_VMEM_LIMIT_MOE),
        name=name,
    )(item_e, item_start, item_rows, total, in_rows, *weights)


def _cast_gate_up(w_refs, w_bf):
    d_exp = w_refs[0].shape[1]
    w_bf[:, 0:d_exp] = w_refs[0][...].astype(BF16)
    w_bf[:, d_exp:2 * d_exp] = w_refs[1][...].astype(BF16)


def _gate_up(words, w_bf):
    half = words.shape[1]
    d_exp = w_bf.shape[1] // 2
    lo, hi = _unpack_pairs(words)
    gu = (jnp.dot(lo.astype(BF16), w_bf[0:half, :], preferred_element_type=F32)
          + jnp.dot(hi.astype(BF16), w_bf[half:2 * half, :], preferred_element_type=F32))
    gate, up = gu[:, 0:d_exp], gu[:, d_exp:2 * d_exp]
    return (gate * jax.nn.sigmoid(gate) * up).astype(BF16)


def _cast_down(w_refs, w_bf):
    w_bf[...] = w_refs[0][...].astype(BF16)


def _down(hid, w_bf):
    half = w_bf.shape[1] // 2
    y = jnp.dot(hid, w_bf[...], preferred_element_type=F32)
    return _pack_pairs(y[:, 0:half], y[:, half:2 * half])


def _combine_kernel(as_ref, x_ref, route_ref, g_ref, y_hbm, o_ref, *rest, emit_norm):
    if emit_norm:
        h_ref, ybuf, sem = rest
    else:
        ybuf, sem = rest
    i = pl.program_id(0)
    tm, d = x_ref.shape
    half = d // 2

    def gather_row(tile, r, buf):
        a = 2 * (tile * tm + r)
        for k in range(2):
            pltpu.make_async_copy(y_hbm.at[pl.ds(as_ref[a + k], 1)], ybuf.at[buf, k, pl.ds(r, 1)],
                                  sem.at[buf]).start()

    def wait_gather(buf):
        for k in range(2):
            pltpu.make_async_copy(y_hbm.at[pl.ds(0, tm)], ybuf.at[buf, k], sem.at[buf]).wait()

    @pl.when(i == 0)
    def _():
        def body(r, carry):
            gather_row(0, r, 0)
            return carry
        lax.fori_loop(0, tm, body, 0, unroll=8)

    buf = i % 2
    last = pl.num_programs(0) - 1
    next_tile = jnp.minimum(i + 1, last)
    wait_gather(buf)

    def row_chunk(c, carry):
        for rr in range(COMBINE_ROWS):
            gather_row(next_tile, c * COMBINE_ROWS + rr, 1 - buf)
        rows = pl.ds(pl.multiple_of(c * COMBINE_ROWS, COMBINE_ROWS), COMBINE_ROWS)
        w1 = route_ref[rows, 2:3]
        w2 = route_ref[rows, 3:4]
        sq = jnp.zeros((COMBINE_ROWS, 1), F32)
        for c0 in range(0, half, COMBINE_COLS):
            cols = slice(c0, c0 + COMBINE_COLS)
            up_cols = slice(half + c0, half + c0 + COMBINE_COLS)
            lo1, hi1 = _unpack_pairs(ybuf[buf, 0, rows, cols])
            lo2, hi2 = _unpack_pairs(ybuf[buf, 1, rows, cols])
            out_lo = x_ref[rows, cols] + w1 * lo1 + w2 * lo2
            out_hi = x_ref[rows, up_cols] + w1 * hi1 + w2 * hi2
            o_ref[rows, cols] = out_lo
            o_ref[rows, up_cols] = out_hi
            if emit_norm:
                sq = sq + jnp.sum(out_lo * out_lo + out_hi * out_hi, axis=-1, keepdims=True)
        if emit_norm:
            inv = lax.rsqrt(sq * (1.0 / d) + EPS)
            for c0 in range(0, d, COMBINE_COLS):
                cols = slice(c0, c0 + COMBINE_COLS)
                h_ref[rows, cols] = (o_ref[rows, cols] * inv * g_ref[:, cols]).astype(h_ref.dtype)
        return carry
    lax.fori_loop(0, tm // COMBINE_ROWS, row_chunk, 0, unroll=4)

    @pl.when(i == last)
    def _():
        wait_gather(1 - buf)


def _moe(x, norm_g, wr_g, br_g, wr_e, br_e, w_gate, w_up, w_down, next_norm_g=None, tm_combine=256):
    t, d = x.shape
    d_exp = w_gate.shape[2]
    hp, route, counts = _router(x, norm_g, wr_g, br_g, wr_e, br_e)
    (assign_slot, padded_start, padded, total, item_e, item_start, item_rows,
     n_slots, _) = _dispatch_tables(route, counts, t)
    items = (item_e, item_start, item_rows, total)
    xs = _dispatch(hp, assign_slot, padded_start, padded, total, n_slots)
    hid = _expert_items_call("moe_up", items, xs, (w_gate, w_up), d_exp, BF16, 2 * d_exp,
                             _cast_gate_up, _gate_up)
    y = _expert_items_call("moe_down", items, hid, (w_down,), d // 2, jnp.uint32, d, _cast_down, _down)

    tm = min(tm_combine, t)
    emit_norm = next_norm_g is not None
    row_tile = pl.BlockSpec((tm, d), lambda i, a: (i, 0))
    gain = next_norm_g if emit_norm else jnp.ones((d,), F32)
    out = pl.pallas_call(
        functools.partial(_combine_kernel, emit_norm=emit_norm),
        grid_spec=pltpu.PrefetchScalarGridSpec(
            num_scalar_prefetch=1,
            grid=(t // tm,),
            in_specs=[row_tile,
                      pl.BlockSpec((tm, ROUTE_LANES), lambda i, a: (i, 0)),
                      pl.BlockSpec((1, d), lambda i, a: (0, 0)),
                      pl.BlockSpec(memory_space=pl.ANY)],
            out_specs=[row_tile, row_tile] if emit_norm else row_tile,
            scratch_shapes=[pltpu.VMEM((2, 2, tm, d // 2), jnp.uint32),
                            pltpu.SemaphoreType.DMA((2,))],
        ),
        out_shape=([jax.ShapeDtypeStruct((t, d), F32), jax.ShapeDtypeStruct((t, d), BF16)] if emit_norm
                   else jax.ShapeDtypeStruct((t, d), F32)),
        compiler_params=_params("arbitrary"),
        name="moe_combine",
    )(assign_slot, x, route, gain.reshape(1, d).astype(F32), y)
    return tuple(out) if emit_norm else (out, None)


def _even_layer_mixer(x, mix_norm, w_in, a_ln_g, a_ln_b, a_ws, a_bs, b_q_norm, b_k_norm, b_sinks, w_out):
    n_groups = a_ws.shape[0]
    a_width = n_groups * HEAD_DIM
    n_heads = b_sinks.shape[0]
    n_kv = (w_in.shape[1] - 2 * a_width - n_heads * HEAD_DIM) // (2 * HEAD_DIM)
    h = _rmsnorm(x, mix_norm)
    proj = _matmul(h, w_in.astype(BF16), tm=1024, tn=1024)
    y_a = _gmlp(proj, a_ln_g, a_ln_b, a_ws, a_bs)
    y_b = _swa(proj, b_q_norm, b_k_norm, b_sinks, n_heads, n_kv, col0=2 * a_width)
    return _out_proj(x, y_a, y_b, w_out.astype(BF16))


def _odd_layer_mixer(x, h, w_in, c_conv_w, c_conv_b, c_ln_g, c_ln_b, w_out):
    c_width = c_conv_w.shape[1]
    n_heads = (w_in.shape[1] - 2 * c_width) // (3 * HEAD_DIM)
    proj = _matmul(h, w_in.astype(BF16), tm=1024, tn=1024)
    y_c = _conformer(proj, c_conv_w, c_conv_b, c_ln_g, c_ln_b)
    y_d = _stickbreak(proj, n_heads, col0=2 * c_width)
    return _out_proj(x, y_c, y_d, w_out.astype(BF16))


def kernel(x, l0_mix_norm, l0_w_in, l0_a_ln_g, l0_a_ln_b, l0_a_ws, l0_a_bs, l0_b_q_norm, l0_b_k_norm, l0_b_sinks, l0_w_out, l0_ffn_norm, l0_router_group_w, l0_router_group_b, l0_router_expert_w, l0_router_expert_b, l0_w_gate, l0_w_up, l0_w_down, l1_mix_norm, l1_w_in, l1_c_conv_w, l1_c_conv_b, l1_c_ln_g, l1_c_ln_b, l1_w_out, l1_ffn_norm, l1_router_group_w, l1_router_group_b, l1_router_expert_w, l1_router_expert_b, l1_w_gate, l1_w_up, l1_w_down):
    bsz, seq, d = x.shape
    xt = x.reshape(bsz * seq, d)
    assert bsz == 1, "the mixers index positions by row; one sequence per call"
    xt = _even_layer_mixer(xt, l0_mix_norm, l0_w_in, l0_a_ln_g, l0_a_ln_b, l0_a_ws, l0_a_bs,
                           l0_b_q_norm, l0_b_k_norm, l0_b_sinks, l0_w_out)
    xt, h1 = _moe(xt, l0_ffn_norm, l0_router_group_w, l0_router_group_b, l0_router_expert_w,
                  l0_router_expert_b, l0_w_gate, l0_w_up, l0_w_down, next_norm_g=l1_mix_norm)
    xt = _odd_layer_mixer(xt, h1, l1_w_in, l1_c_conv_w, l1_c_conv_b, l1_c_ln_g, l1_c_ln_b, l1_w_out)
    xt, _ = _moe(xt, l1_ffn_norm, l1_router_group_w, l1_router_group_b, l1_router_expert_w,
                 l1_router_expert_b, l1_w_gate, l1_w_up, l1_w_down)
    return xt.reshape(bsz, seq, d)
```
